```python
import jax, jax.numpy as jnp
from jax import lax
import numpy as np

D_MODEL = 1024
BATCH = 16
SEQ = 2048
DEPTH = 2

GRID_W = 64
CTX_LEN = 256
EPS = 1e-6
RET_HEADS = 4
RET_DK = 64
RET_DV = 128
RET_CHUNK = 128
RET_DECAY_EXP = (5.0, 7.0, 9.0, 11.0)
RET_ROPE_BASE = 10000.0
POOL_GROUPS = 4
POOL_GW = 128
POOL_WINDOWS = (2, 4, 8, 16)
ATT_HEADS = 8
ATT_KV_HEADS = 2
ATT_HD = 64
ATT_WINDOW = 128
ATT_BLOCK = 128
ROPE_BASE = 10000.0
MLP_HIDDEN = 4 * D_MODEL
N_BRANCH = 3

RET_QK_W = RET_HEADS * RET_DK
RET_V_W = RET_HEADS * RET_DV
ATT_Q_W = ATT_HEADS * ATT_HD
ATT_KV_W = ATT_KV_HEADS * ATT_HD
POOL_W = POOL_GROUPS * POOL_GW
CTX_SIDE_SIZES = (RET_QK_W, RET_V_W, ATT_KV_W, ATT_KV_W)
QUERY_SIDE_SIZES = (RET_QK_W, RET_V_W, ATT_Q_W, POOL_W, N_BRANCH * D_MODEL)
IN_SIZES = CTX_SIDE_SIZES + QUERY_SIDE_SIZES
CTX_SIDE_COLS = sum(CTX_SIDE_SIZES)
IN_COLS = sum(IN_SIZES)

kernel_name = 'hybrid_retention_pool_swa_dit_block'

F32 = jnp.float32


def rmsnorm(x, w=None):
    xf = x.astype(F32)
    y = xf * lax.rsqrt(jnp.mean(xf * xf, axis=-1, keepdims=True) + EPS)
    if w is not None:
        y = y * w.astype(F32)
    return y.astype(x.dtype)


def modulate(h, shift, scale):
    return h * (1 + scale) + shift


def split_cols(z, sizes):
    out, start = [], 0
    for s in sizes:
        out.append(z[..., start:start + s])
        start += s
    return out


def heads(t, n):
    return t.reshape(t.shape[0], t.shape[1], n, -1)


def rope(x, pos, base):
    half = x.shape[-1] // 2
    freq = base ** (-jnp.arange(half, dtype=F32) / half)
    ang = pos.astype(F32)[:, None] * freq[None, :]
    cos, sin = jnp.cos(ang)[:, None, :], jnp.sin(ang)[:, None, :]
    x1, x2 = x[..., :half].astype(F32), x[..., half:].astype(F32)
    return jnp.concatenate([x1 * cos - x2 * sin, x1 * sin + x2 * cos], axis=-1).astype(x.dtype)


def axial_rope(x, row, col):
    h = x.shape[-1] // 2
    return jnp.concatenate([rope(x[..., :h], row, ROPE_BASE), rope(x[..., h:], col, ROPE_BASE)], axis=-1)


def retention_scan(q, k, v, log_gamma, r0):
    B, L, H, dk = q.shape
    dv = v.shape[-1]
    C = RET_CHUNK
    n = L // C
    qc = q.reshape(B, n, C, H, dk)
    kc = k.reshape(B, n, C, H, dk)
    vc = v.reshape(B, n, C, H, dv)
    idx = jnp.arange(C, dtype=F32)
    rel = idx[:, None] - idx[None, :]
    decay = jnp.where(rel[None] >= 0, jnp.exp(jnp.maximum(rel, 0.0)[None] * log_gamma[:, None, None]), 0.0).astype(q.dtype)
    scores = jnp.einsum('bnihd,bnjhd->bnhij', qc, kc) * decay
    y_inner = jnp.einsum('bnhij,bnjhe->bnihe', scores, vc)
    zeta = jnp.exp((C - 1 - idx)[:, None] * log_gamma[None, :]).astype(q.dtype)
    u = jnp.einsum('bnjhd,jh,bnjhe->nbhde', kc, zeta, vc)
    chunk_decay = jnp.exp(C * log_gamma).astype(q.dtype)[:, None, None]

    def step(r, u_i):
        return chunk_decay * r + u_i, r

    r_final, r_prev = lax.scan(step, r0, u)
    xi = jnp.exp((idx + 1)[:, None] * log_gamma[None, :]).astype(q.dtype)
    y_cross = jnp.einsum('bnihd,ih,nbhde->bnihe', qc, xi, r_prev)
    return (y_inner + y_cross).reshape(B, L, H, dv), r_final


def retention_final_state(k, v, log_gamma):
    L = k.shape[1]
    w = jnp.exp((L - 1 - jnp.arange(L, dtype=F32))[:, None] * log_gamma[None, :]).astype(k.dtype)
    return jnp.einsum('blhd,lh,blhe->bhde', k, w, v)


def bidir_retention(q, k, v, log_gamma, r0_f, r0_b):
    y_f, r_f = retention_scan(q, k, v, log_gamma[0], r0_f)
    y_b, r_b = retention_scan(q[:, ::-1], k[:, ::-1], v[:, ::-1], log_gamma[1], r0_b)
    return y_f + y_b[:, ::-1], r_f, r_b


def retention_output(y, g):
    B, L = y.shape[:2]
    return jax.nn.silu(g) * rmsnorm(y).reshape(B, L, RET_V_W)


def multiscale_pool(u, w_grp, scale):
    B, L, _ = u.shape
    ug = u.reshape(B, L, POOL_GROUPS, POOL_GW)
    cs = jnp.cumsum(ug.astype(F32), axis=1)
    cs = jnp.concatenate([jnp.zeros_like(cs[:, :1]), cs], axis=1)
    t = jnp.arange(L)
    means = []
    for g, w in enumerate(POOL_WINDOWS):
        lo = jnp.clip(t - w // 2, 0, L)
        hi = jnp.clip(t + w // 2, 0, L)
        csg = cs[:, :, g]
        means.append((csg[:, hi] - csg[:, lo]) / (hi - lo).astype(F32)[:, None])
    pooled = jnp.stack(means, axis=2).astype(u.dtype)
    mixed = jnp.einsum('blgc,gcd->blgd', pooled - ug, w_grp)
    return mixed.reshape(B, L, POOL_W) * scale


def softmax_with_sink(logits, sink):
    lf = logits.astype(F32)
    s = jnp.broadcast_to(sink.astype(F32).reshape(ATT_KV_HEADS, -1, 1, 1), lf.shape[:-1] + (1,))
    p = jax.nn.softmax(jnp.concatenate([lf, s], axis=-1), axis=-1)
    return p[..., :-1]


def context_attention(q, k, v, sink):
    B, L, H, dh = q.shape
    G = H // ATT_KV_HEADS
    qg = q.reshape(B, L, ATT_KV_HEADS, G, dh)
    s = jnp.einsum('bikgd,bjkd->bkgij', qg, k) * dh ** -0.5
    p = softmax_with_sink(s, sink).astype(v.dtype)
    return jnp.einsum('bkgij,bjkd->bikgd', p, v).reshape(B, L, H * dh)


def windowed_attention(q, k, v, kc, vc, sink):
    B, S, H, dh = q.shape
    nb = S // ATT_BLOCK
    G = H // ATT_KV_HEADS
    scale = dh ** -0.5
    qb = q.reshape(B, nb, ATT_BLOCK, ATT_KV_HEADS, G, dh)

    def band(t):
        tb = t.reshape(B, nb, ATT_BLOCK, ATT_KV_HEADS, dh)
        tp = jnp.pad(tb, ((0, 0), (1, 1), (0, 0), (0, 0), (0, 0)))
        return jnp.concatenate([tp[:, :-2], tp[:, 1:-1], tp[:, 2:]], axis=2)

    kb, vb = band(k), band(v)
    blk = jnp.arange(nb)[:, None]
    qpos = blk * ATT_BLOCK + jnp.arange(ATT_BLOCK)[None, :]
    kpos = (blk - 1) * ATT_BLOCK + jnp.arange(3 * ATT_BLOCK)[None, :]
    valid = ((jnp.abs(qpos[:, :, None] - kpos[:, None, :]) <= ATT_WINDOW)
             & (kpos[:, None, :] >= 0) & (kpos[:, None, :] < S))
    s_loc = jnp.einsum('bnikgd,bnjkd->bnkgij', qb, kb) * scale
    s_loc = jnp.where(valid[None, :, None, None], s_loc, -jnp.inf)
    s_ctx = jnp.einsum('bnikgd,bjkd->bnkgij', qb, kc) * scale
    p = softmax_with_sink(jnp.concatenate([s_loc, s_ctx], axis=-1), sink).astype(v.dtype)
    nl = 3 * ATT_BLOCK
    o = (jnp.einsum('bnkgij,bnjkd->bnikgd', p[..., :nl], vb)
         + jnp.einsum('bnkgij,bjkd->bnikgd', p[..., nl:], vc))
    return o.reshape(B, S, H * dh)


def merge_branches(y_ret, y_pool, y_att, gates, w_ret_out, w_pool_out, w_attn_out, w_out):
    g_r, g_p, g_a = split_cols(jax.nn.sigmoid(gates), (D_MODEL,) * N_BRANCH)
    y = g_r * (y_ret @ w_ret_out) + g_p * (y_pool @ w_pool_out) + g_a * (y_att @ w_attn_out)
    return y @ w_out


def sq_relu_mlp(h, w1, w2):
    return jnp.square(jax.nn.relu(h @ w1)) @ w2


def setup_inputs(seed: int = 0) -> dict:
    key = jax.random.key(seed)
    ks = jax.random.split(key, 21)

    def nrm(k, shape, fan_in):
        return jax.random.normal(k, shape, F32) * fan_in ** -0.5

    def near_one(k, shape):
        return 1.0 + 0.02 * jax.random.normal(k, shape, F32)

    return {
        'x': jax.random.normal(ks[0], (BATCH, SEQ, D_MODEL), F32),
        'c': jax.random.normal(ks[1], (BATCH, D_MODEL), F32),
        'ctx': jax.random.normal(ks[2], (BATCH, CTX_LEN, D_MODEL), F32),
        'c_ctx': jax.random.normal(ks[3], (D_MODEL,), F32),
        'norm1_w': near_one(ks[4], (DEPTH, D_MODEL)),
        'norm2_w': near_one(ks[5], (DEPTH, D_MODEL)),
        'ada_w': nrm(ks[6], (DEPTH, D_MODEL, 6 * D_MODEL), D_MODEL),
        'ada_b': 0.02 * jax.random.normal(ks[7], (DEPTH, 6 * D_MODEL), F32),
        'w_in': nrm(ks[8], (DEPTH, D_MODEL, IN_COLS), D_MODEL),
        'ret_decay': jnp.asarray(RET_DECAY_EXP, F32)[None, None, :]
                     + 0.1 * jax.random.normal(ks[9], (DEPTH, 2, RET_HEADS), F32),
        'pool_w': nrm(ks[10], (DEPTH, POOL_GROUPS, POOL_GW, POOL_GW), POOL_GW),
        'pool_scale': near_one(ks[11], (DEPTH, POOL_W)),
        'q_norm_w': near_one(ks[12], (DEPTH, ATT_HD)),
        'k_norm_w': near_one(ks[13], (DEPTH, ATT_HD)),
        'attn_sink': 0.5 * jax.random.normal(ks[14], (DEPTH, ATT_HEADS), F32),
        'w_ret_out': nrm(ks[15], (DEPTH, RET_V_W, D_MODEL), RET_V_W),
        'w_pool_out': nrm(ks[16], (DEPTH, POOL_W, D_MODEL), POOL_W),
        'w_attn_out': nrm(ks[17], (DEPTH, ATT_Q_W, D_MODEL), ATT_Q_W),
        'w_out': nrm(ks[18], (DEPTH, D_MODEL, D_MODEL), D_MODEL),
        'w_mlp1': nrm(ks[19], (DEPTH, D_MODEL, MLP_HIDDEN), D_MODEL),
        'w_mlp2': nrm(ks[20], (DEPTH, MLP_HIDDEN, D_MODEL), MLP_HIDDEN),
    }


def reference(x, c, ctx, c_ctx, norm1_w, norm2_w, ada_w, ada_b, w_in, ret_decay, pool_w, pool_scale,
              q_norm_w, k_norm_w, attn_sink, w_ret_out, w_pool_out, w_attn_out, w_out, w_mlp1, w_mlp2):
    B, S, _ = x.shape
    ROWS = S // GRID_W
    row = jnp.repeat(jnp.arange(ROWS), GRID_W)
    col = jnp.tile(jnp.arange(GRID_W), ROWS)
    seq_pos = jnp.arange(S)
    xc = ctx
    for l in range(DEPTH):
        last = l == DEPTH - 1
        log_gamma = jnp.log1p(-jnp.exp2(-ret_decay[l].astype(F32)))
        mod_x = jax.nn.silu(c) @ ada_w[l] + ada_b[l]
        sh1, sc1, g1, sh2, sc2, g2 = [m[:, None, :] for m in split_cols(mod_x, (D_MODEL,) * 6)]
        mod_c = jax.nn.silu(c_ctx) @ ada_w[l] + ada_b[l]
        csh1, csc1, cg1, csh2, csc2, cg2 = split_cols(mod_c, (D_MODEL,) * 6)

        uc = modulate(rmsnorm(xc, norm1_w[l]), csh1, csc1)
        if last:
            rk_c, rv_c, ak_c, av_c = split_cols(uc @ w_in[l][:, :CTX_SIDE_COLS], CTX_SIDE_SIZES)
        else:
            rk_c, rv_c, ak_c, av_c, rq_c, rg_c, aq_c, pu_c, gt_c = split_cols(uc @ w_in[l], IN_SIZES)
        rk_c = heads(rk_c, RET_HEADS) * RET_DK ** -0.5
        rv_c = heads(rv_c, RET_HEADS)
        ak_c = rmsnorm(heads(ak_c, ATT_KV_HEADS), k_norm_w[l])
        av_c = heads(av_c, ATT_KV_HEADS)
        if last:
            r_f = retention_final_state(rk_c, rv_c, log_gamma[0])
            r_b = retention_final_state(rk_c[:, ::-1], rv_c[:, ::-1], log_gamma[1])
        else:
            zero = jnp.zeros((B, RET_HEADS, RET_DK, RET_DV), rv_c.dtype)
            yr_c, r_f, r_b = bidir_retention(heads(rq_c, RET_HEADS), rk_c, rv_c, log_gamma, zero, zero)
            ya_c = context_attention(rmsnorm(heads(aq_c, ATT_HEADS), q_norm_w[l]), ak_c, av_c, attn_sink[l])
            yp_c = multiscale_pool(pu_c, pool_w[l], pool_scale[l])
            mix_c = merge_branches(retention_output(yr_c, rg_c), yp_c, ya_c, gt_c,
                                   w_ret_out[l], w_pool_out[l], w_attn_out[l], w_out[l])

        ux = modulate(rmsnorm(x, norm1_w[l]), sh1, sc1)
        rk, rv, ak, av, rq, rg, aq, pu, gt = split_cols(ux @ w_in[l], IN_SIZES)
        rq = rope(heads(rq, RET_HEADS), seq_pos, RET_ROPE_BASE)
        rk = rope(heads(rk, RET_HEADS), seq_pos, RET_ROPE_BASE) * RET_DK ** -0.5
        yr, _, _ = bidir_retention(rq, rk, heads(rv, RET_HEADS), log_gamma, r_f, r_b)
        aq = axial_rope(rmsnorm(heads(aq, ATT_HEADS), q_norm_w[l]), row, col)
        ak = axial_rope(rmsnorm(heads(ak, ATT_KV_HEADS), k_norm_w[l]), row, col)
        ya = windowed_attention(aq, ak, heads(av, ATT_KV_HEADS), ak_c, av_c, attn_sink[l])
        yp = multiscale_pool(pu, pool_w[l], pool_scale[l])
        mix = merge_branches(retention_output(yr, rg), yp, ya, gt,
                             w_ret_out[l], w_pool_out[l], w_attn_out[l], w_out[l])
        x = x + g1 * mix
        x = x + g2 * sq_relu_mlp(modulate(rmsnorm(x, norm2_w[l]), sh2, sc2), w_mlp1[l], w_mlp2[l])

        if not last:
            xc = xc + cg1 * mix_c
            xc = xc + cg2 * sq_relu_mlp(modulate(rmsnorm(xc, norm2_w[l]), csh2, csc2), w_mlp1[l], w_mlp2[l])
    return x
```

```python
import functools

import jax
import jax.numpy as jnp
from jax import lax
from jax.experimental import pallas as pl
from jax.experimental.pallas import tpu as pltpu

F32 = jnp.float32
BF16 = jnp.bfloat16

EPS = 1e-6
LANES = 128
RET_HEADS, RET_DK, RET_DV, RET_CHUNK = 4, 64, 128, 128
RET_ROPE_BASE = 10000.0
POOL_GROUPS, POOL_GW = 4, 128
POOL_WINDOWS = (2, 4, 8, 16)
ATT_HEADS, ATT_KV_HEADS, ATT_HD = 8, 2, 64
ATT_WINDOW = ATT_BLOCK = 128
ROPE_BASE = 10000.0
GRID_W = 64
N_BRANCH = 3
NEG_BIG = -1e30
VMEM_LIMIT = 56 * 1024 * 1024


def _cparams(n_axes):
    return pltpu.CompilerParams(dimension_semantics=("arbitrary",) * n_axes,
                                vmem_limit_bytes=VMEM_LIMIT)


def _resident(shape):
    zeros = (0,) * len(shape)
    return pl.BlockSpec(shape, lambda *_: zeros, pipeline_mode=pl.Buffered(1))


def _sigmoid(x):
    return 1.0 / (1.0 + jnp.exp(-x))


def _mod_kernel(cc_ref, w_ref, b_ref, o_ref):
    cc = cc_ref[...]
    s = (cc * _sigmoid(cc)).astype(BF16)
    o_ref[0] = jnp.dot(s, w_ref[0].astype(BF16), preferred_element_type=F32) + b_ref[0]


def _modulation(cc, ada_w, ada_b):
    depth, d, n = ada_w.shape
    r = cc.shape[0]
    tn = 1024
    return pl.pallas_call(
        _mod_kernel,
        grid=(depth, n // tn),
        in_specs=[pl.BlockSpec((r, d), lambda l, j: (0, 0)),
                  pl.BlockSpec((1, d, tn), lambda l, j: (l, 0, j)),
                  pl.BlockSpec((1, 1, tn), lambda l, j: (l, 0, j))],
        out_specs=pl.BlockSpec((1, r, tn), lambda l, j: (l, 0, j)),
        out_shape=jax.ShapeDtypeStruct((depth, r, n), F32),
        compiler_params=_cparams(2),
        name="adaln_mod",
    )(cc, ada_w, ada_b.reshape(depth, 1, n))


def _swap_halves(x, half):
    lane = lax.broadcasted_iota(jnp.int32, x.shape, 1)
    first = (lane % (2 * half)) < half
    return jnp.where(first, pltpu.roll(x, LANES - half, 1), pltpu.roll(x, half, 1))


def _rope(x, cos, sin, half):
    return x * cos + _swap_halves(x, half) * sin


def _inproj_kernel(*refs, d, ctx_only, use_rope):
    x_ref, mod_ref, nw_ref, w_ref, qnw_ref, knw_ref, gmat_ref = refs[:7]
    pos = 7
    if use_rope:
        cr_ref, sr_ref, ca_ref, sa_ref = refs[7:11]
        pos = 11
    outs = refs[pos:]

    x = x_ref[0]
    mod = mod_ref[0]
    shift, scale = mod[:, 0:d], mod[:, d:2 * d]
    ms = jnp.mean(x * x, axis=-1, keepdims=True)
    u = ((x * lax.rsqrt(ms + EPS) * nw_ref[...]) * (1.0 + scale) + shift).astype(BF16)

    def proj(a, b):
        return jnp.dot(u, w_ref[:, a:b], preferred_element_type=F32)

    def qk_norm(a, w):
        msq = jnp.dot((a * a).astype(BF16), gmat_ref[...], preferred_element_type=F32)
        return a * lax.rsqrt(msq + EPS) * w

    def ret_rope(a):
        return _rope(a, cr_ref[...], sr_ref[...], RET_DK // 2) if use_rope else a

    def att_rope(a):
        return _rope(a, ca_ref[...], sa_ref[...], ATT_HD // 4) if use_rope else a

    lane = lax.broadcasted_iota(jnp.int32, (x.shape[0], LANES), 1)
    low = lane < ATT_HD

    def dup_heads(a):
        sw = pltpu.roll(a, ATT_HD, 1)
        return jnp.concatenate([jnp.where(low, a, sw), jnp.where(low, sw, a)], axis=1)

    rk_ref, rv_ref, ak_ref, av_ref = outs[:4]
    for s in range(2):
        a = proj(s * LANES, (s + 1) * LANES)
        rk_ref[0, :, s * LANES:(s + 1) * LANES] = (ret_rope(a) * RET_DK ** -0.5).astype(BF16)
    rv_ref[0] = proj(256, 768).astype(BF16)
    ak_ref[0] = dup_heads(att_rope(qk_norm(proj(768, 896), knw_ref[...]))).astype(BF16)
    av_ref[0] = dup_heads(proj(896, 1024)).astype(BF16)
    if ctx_only:
        return
    rq_ref, rg_ref, aq_ref, pu_ref, gt_ref = outs[4:]
    for s in range(2):
        a = proj(1024 + s * LANES, 1024 + (s + 1) * LANES)
        rq_ref[0, :, s * LANES:(s + 1) * LANES] = ret_rope(a).astype(BF16)
    rg_ref[0] = proj(1280, 1792).astype(BF16)
    for s in range(4):
        a = proj(1792 + s * LANES, 1792 + (s + 1) * LANES)
        a = att_rope(qk_norm(a, qnw_ref[...])) * ATT_HD ** -0.5
        aq_ref[0, :, s * LANES:(s + 1) * LANES] = a.astype(BF16)
    pu_ref[0] = proj(2304, 2816).astype(BF16)
    for s in range(6):
        gt_ref[0, :, s * 512:(s + 1) * 512] = proj(2816 + s * 512, 2816 + (s + 1) * 512).astype(BF16)


def _inproj(x3, mod3, mod_row, nw, w, qnw, knw, gmat, tables, *, tm, ctx_only):
    bx, l, d = x3.shape
    ncol = w.shape[1]
    use_rope = tables is not None
    grid = (bx, l // tm)
    row = lambda b, i: (b, i, 0)
    in_specs = [pl.BlockSpec((1, tm, d), row),
                pl.BlockSpec((1, 1, mod3.shape[2]), (lambda b, i: (b, 0, 0)) if mod_row is None
                             else (lambda b, i: (mod_row, 0, 0))),
                pl.BlockSpec((1, d), lambda b, i: (0, 0)),
                _resident((d, ncol)),
                pl.BlockSpec((1, LANES), lambda b, i: (0, 0)),
                pl.BlockSpec((1, LANES), lambda b, i: (0, 0)),
                pl.BlockSpec((LANES, LANES), lambda b, i: (0, 0))]
    args = [x3, mod3, nw, w, qnw, knw, gmat]
    if use_rope:
        in_specs += [pl.BlockSpec((tm, LANES), lambda b, i: (i, 0))] * 4
        args += list(tables)
    widths = [256, 512, 256, 256] + ([] if ctx_only else [256, 512, 512, 512, 3072])
    out_specs = [pl.BlockSpec((1, tm, wd), row) for wd in widths]
    out_shape = [jax.ShapeDtypeStruct((bx, l, wd), BF16) for wd in widths]
    return pl.pallas_call(
        functools.partial(_inproj_kernel, d=d, ctx_only=ctx_only, use_rope=use_rope),
        grid=grid, in_specs=in_specs, out_specs=out_specs, out_shape=out_shape,
        compiler_params=_cparams(2),
        name="inproj_ctx" if not use_rope else "inproj",
    )(*args)


def _ret_kernel(*refs, nc, n, ctx_out):
    lg_ref = refs[0]
    if ctx_out:
        qc_ref, kc_ref, vc_ref, gc_ref, q_ref, k_ref, v_ref, g_ref, yc_ref, y_ref = refs[1:11]
        scr = refs[11:]
    else:
        kc_ref, vc_ref, q_ref, k_ref, v_ref, g_ref, y_ref = refs[1:8]
        qc_ref = gc_ref = yc_ref = None
        scr = refs[8:]
    dmat_ref, zf_ref, zb_ref, xf_ref, xb_ref, gf_ref, gb_ref, rf_ref, rb_ref, rbp_ref = scr
    C = RET_CHUNK

    ri = lax.broadcasted_iota(jnp.int32, (C, LANES), 0)
    ci = lax.broadcasted_iota(jnp.int32, (C, LANES), 1)
    rif = ri.astype(F32)
    rel = (ri - ci).astype(F32)
    low = ci < RET_DK
    for h in range(RET_HEADS):
        lgf, lgb = lg_ref[0, h], lg_ref[1, h]
        dmat_ref[h] = jnp.where(rel >= 0, jnp.exp(jnp.maximum(rel, 0.0) * lgf), 0.0) \
            + jnp.where(rel <= 0, jnp.exp(jnp.maximum(-rel, 0.0) * lgb), 0.0)
        zf_ref[h] = jnp.exp((C - 1 - rif) * lgf)
        zb_ref[h] = jnp.exp(rif * lgb)
        gf_ref[h] = jnp.exp(jnp.full((C, LANES), C, F32) * lgf)
        gb_ref[h] = jnp.exp(jnp.full((C, LANES), C, F32) * lgb)
        rf_ref[h] = jnp.zeros((C, LANES), F32)
        rb_ref[h] = jnp.zeros((C, LANES), F32)
    for s in range(RET_HEADS // 2):
        lgf = jnp.where(low, lg_ref[0, 2 * s], lg_ref[0, 2 * s + 1])
        lgb = jnp.where(low, lg_ref[1, 2 * s], lg_ref[1, 2 * s + 1])
        xf_ref[s] = jnp.exp((rif + 1.0) * lgf)
        xb_ref[s] = jnp.exp((C - rif) * lgb)

    def head_rows(h):
        return (ri < RET_DK) if h % 2 == 0 else (ri >= RET_DK)

    def state_increment(kslab, v_h, z, h):
        vz = (v_h.astype(F32) * z).astype(BF16)
        inc = lax.dot_general(kslab, vz, (((0,), (0,)), ((), ())), preferred_element_type=F32)
        return jnp.where(head_rows(h), inc, 0.0)

    def bwd_chunk(k_blk, v_blk, t):
        for h in range(RET_HEADS):
            s = h // 2
            rbp_ref[t, h] = rb_ref[h].astype(BF16)
            inc = state_increment(k_blk[:, s * LANES:(s + 1) * LANES], v_blk[:, h * RET_DV:(h + 1) * RET_DV],
                                  zb_ref[h], h)
            rb_ref[h] = gb_ref[h] * rb_ref[h] + inc

    def fwd_chunk(q_blk, k_blk, v_blk, g_blk, t, store):
        for h in range(RET_HEADS):
            s = h // 2
            qs = q_blk[:, s * LANES:(s + 1) * LANES]
            ks = k_blk[:, s * LANES:(s + 1) * LANES]
            v_h = v_blk[:, h * RET_DV:(h + 1) * RET_DV]
            own = low if h % 2 == 0 else jnp.logical_not(low)
            km = jnp.where(own, ks, jnp.zeros_like(ks))
            sc = lax.dot_general(qs, km, (((1,), (1,)), ((), ())), preferred_element_type=F32) * dmat_ref[h]
            y = jnp.dot(sc.astype(BF16), v_h, preferred_element_type=F32)
            qf = qs.astype(F32)
            qx = jnp.concatenate([(qf * xf_ref[s]).astype(BF16), (qf * xb_ref[s]).astype(BF16)], axis=1)
            st = jnp.concatenate([rf_ref[h].astype(BF16), rbp_ref[t, h]], axis=0)
            y = y + jnp.dot(qx, st, preferred_element_type=F32)
            if store is not None:
                gh = g_blk[:, h * RET_DV:(h + 1) * RET_DV].astype(F32)
                yn = y * lax.rsqrt(jnp.mean(y * y, axis=-1, keepdims=True) + EPS)
                store(h, (gh * _sigmoid(gh) * yn).astype(BF16))
            rf_ref[h] = gf_ref[h] * rf_ref[h] + state_increment(ks, v_h, zf_ref[h], h)

    for t in reversed(range(nc)):
        bwd_chunk(kc_ref[0, t * C:(t + 1) * C, :], vc_ref[0, t * C:(t + 1) * C, :], t)

    def bwd_body(i, carry):
        j = n - 1 - i
        r0 = pl.multiple_of(j * C, C)
        bwd_chunk(k_ref[0, pl.ds(r0, C), :], v_ref[0, pl.ds(r0, C), :], nc + j)
        return carry
    lax.fori_loop(0, n, bwd_body, 0)

    for t in range(nc):
        sl = slice(t * C, (t + 1) * C)
        if ctx_out:
            def store_c(h, val, sl=sl):
                yc_ref[0, sl, h * RET_DV:(h + 1) * RET_DV] = val
            fwd_chunk(qc_ref[0, sl, :], kc_ref[0, sl, :], vc_ref[0, sl, :], gc_ref[0, sl, :], t, store_c)
        else:
            for h in range(RET_HEADS):
                s = h // 2
                rf_ref[h] = gf_ref[h] * rf_ref[h] + state_increment(
                    kc_ref[0, sl, s * LANES:(s + 1) * LANES], vc_ref[0, sl, h * RET_DV:(h + 1) * RET_DV],
                    zf_ref[h], h)

    def fwd_body(j, carry):
        r0 = pl.multiple_of(j * C, C)
        rows = pl.ds(r0, C)

        def store(h, val):
            y_ref[0, rows, h * RET_DV:(h + 1) * RET_DV] = val
        fwd_chunk(q_ref[0, rows, :], k_ref[0, rows, :], v_ref[0, rows, :], g_ref[0, rows, :], nc + j, store)
        return carry
    lax.fori_loop(0, n, fwd_body, 0)


def _retention(log_gamma, ctx_parts, lat_parts, *, ctx_out):
    b, s, _ = lat_parts[0].shape
    lc = ctx_parts[-1].shape[1]
    nc, n = lc // RET_CHUNK, s // RET_CHUNK
    full = lambda a: pl.BlockSpec((1,) + a.shape[1:], lambda i: (i, 0, 0))
    in_specs = [pl.BlockSpec(memory_space=pltpu.SMEM)] + [full(a) for a in (*ctx_parts, *lat_parts)]
    vw = RET_HEADS * RET_DV
    out_shape = [jax.ShapeDtypeStruct((b, s, vw), BF16)]
    if ctx_out:
        out_shape = [jax.ShapeDtypeStruct((b, lc, vw), BF16)] + out_shape
    out_specs = [full(o) for o in out_shape]
    tile = (RET_CHUNK, LANES)
    scratch = [pltpu.VMEM((RET_HEADS,) + tile, F32),
               pltpu.VMEM((RET_HEADS,) + tile, F32), pltpu.VMEM((RET_HEADS,) + tile, F32),
               pltpu.VMEM((RET_HEADS // 2,) + tile, F32), pltpu.VMEM((RET_HEADS // 2,) + tile, F32),
               pltpu.VMEM((RET_HEADS,) + tile, F32), pltpu.VMEM((RET_HEADS,) + tile, F32),
               pltpu.VMEM((RET_HEADS,) + tile, F32), pltpu.VMEM((RET_HEADS,) + tile, F32),
               pltpu.VMEM((nc + n, RET_HEADS) + tile, BF16)]
    res = pl.pallas_call(
        functools.partial(_ret_kernel, nc=nc, n=n, ctx_out=ctx_out),
        grid=(b,), in_specs=in_specs, out_specs=out_specs, out_shape=out_shape,
        scratch_shapes=scratch, compiler_params=_cparams(1),
        name="retention",
    )(log_gamma, *ctx_parts, *lat_parts)
    return res if ctx_out else (None, res[0])


def _pool_kernel(u_ref, w_ref, sc_ref, o_ref, pa_ref, pb_ref, *, l):
    pad = 16
    t = lax.broadcasted_iota(jnp.int32, (l, LANES), 0)
    zeros_edge = jnp.zeros((pad, LANES), F32)
    ext = l + 16
    for g, w in enumerate(POOL_WINDOWS):
        ug = u_ref[0, :, g * POOL_GW:(g + 1) * POOL_GW].astype(F32)
        for buf in (pa_ref, pb_ref):
            buf[0:pad, :] = zeros_edge
            buf[pad + l:pad + l + pad, :] = zeros_edge
        pa_ref[pad:pad + l, :] = ug
        pb_ref[8:8 + ext, :] = pa_ref[7:7 + ext, :] + pa_ref[8:8 + ext, :]
        src, dst = pb_ref, pa_ref
        step = 1
        while 2 * step < w:
            dst[8:8 + ext, :] = src[8 - step:8 - step + ext, :] + src[8 + step:8 + step + ext, :]
            src, dst = dst, src
            step *= 2
        hw = w // 2
        cnt = (jnp.minimum(t + hw, l) - jnp.maximum(t - hw, 0)).astype(F32)
        pooled = src[pad:pad + l, :] / cnt
        mixed = jnp.dot((pooled - ug).astype(BF16), w_ref[g], preferred_element_type=F32)
        o_ref[0, :, g * POOL_GW:(g + 1) * POOL_GW] = (mixed * sc_ref[:, g * POOL_GW:(g + 1) * POOL_GW]).astype(BF16)


def _pool(u, w_grp, scale):
    b, l, pw = u.shape
    return pl.pallas_call(
        functools.partial(_pool_kernel, l=l),
        grid=(b,),
        in_specs=[pl.BlockSpec((1, l, pw), lambda i: (i, 0, 0)),
                  pl.BlockSpec(w_grp.shape, lambda i: (0, 0, 0)),
                  pl.BlockSpec((1, pw), lambda i: (0, 0))],
        out_specs=pl.BlockSpec((1, l, pw), lambda i: (i, 0, 0)),
        out_shape=jax.ShapeDtypeStruct((b, l, pw), BF16),
        scratch_shapes=[pltpu.VMEM((l + 32, LANES), F32), pltpu.VMEM((l + 32, LANES), F32)],
        compiler_params=_cparams(1),
        name="pool",
    )(u, w_grp, scale)


def _attn_kernel(*refs, local, nb):
    sink_ref = refs[0]
    if local:
        q_ref, k_ref, v_ref, kc_ref, vc_ref, o_ref = refs[1:]
    else:
        q_ref, kc_ref, vc_ref, o_ref = refs[1:]
    BLK = ATT_BLOCK
    nblk = pl.program_id(1)
    lc = kc_ref.shape[1]

    if local:
        prev = jnp.maximum(nblk - 1, 0)
        nxt = jnp.minimum(nblk + 1, nb - 1)
        starts = [pl.multiple_of(prev * BLK, BLK), pl.multiple_of(nblk * BLK, BLK), pl.multiple_of(nxt * BLK, BLK)]
        kcat = jnp.concatenate([k_ref[0, pl.ds(st, BLK), :] for st in starts] + [kc_ref[0]], axis=0)
        vcat = jnp.concatenate([v_ref[0, pl.ds(st, BLK), :] for st in starts] + [vc_ref[0]], axis=0)
        nk = 3 * BLK + lc
        qi = lax.broadcasted_iota(jnp.int32, (BLK, nk), 0)
        kj = lax.broadcasted_iota(jnp.int32, (BLK, nk), 1)
        kpos = (nblk - 1) * BLK + kj
        ok = (jnp.abs(qi + BLK - kj) <= ATT_WINDOW) & (kpos >= 0) & (kpos < nb * BLK)
        ok = ok | (kj >= 3 * BLK)
        bias = jnp.where(ok, 0.0, NEG_BIG)
        bias = jnp.concatenate([bias, bias], axis=0)
    else:
        kcat, vcat = kc_ref[0], vc_ref[0]
        nk = lc
        bias = None

    lane = lax.broadcasted_iota(jnp.int32, (nk, LANES), 1)
    klow = lane < ATT_HD
    row2 = lax.broadcasted_iota(jnp.int32, (2 * BLK, 1), 0)
    olane = lax.broadcasted_iota(jnp.int32, (2 * BLK, LANES), 1)

    for kh in range(ATT_KV_HEADS):
        ks = kcat[:, kh * LANES:(kh + 1) * LANES]
        vs = vcat[:, kh * LANES:(kh + 1) * LANES]
        zero = jnp.zeros_like(ks)
        k2 = jnp.concatenate([jnp.where(klow, ks, zero), jnp.where(klow, zero, ks)], axis=0)
        v2 = jnp.concatenate([jnp.where(klow, vs, zero), jnp.where(klow, zero, vs)], axis=0)
        q2 = jnp.concatenate([q_ref[0, :, (2 * kh + s) * LANES:(2 * kh + s + 1) * LANES] for s in range(2)], axis=0)
        sc = lax.dot_general(q2, k2, (((1,), (1,)), ((), ())), preferred_element_type=F32)
        ps, inv = [], []
        for e in range(2):
            se = sc[:, e * nk:(e + 1) * nk]
            if bias is not None:
                se = se + bias
            sink = jnp.where(row2 < BLK, sink_ref[kh * 4 + e], sink_ref[kh * 4 + 2 + e])
            m = jnp.maximum(jnp.max(se, axis=-1, keepdims=True), sink)
            p = jnp.exp(se - m)
            den = jnp.sum(p, axis=-1, keepdims=True) + jnp.exp(sink - m)
            ps.append(p.astype(BF16))
            inv.append(1.0 / den)
        o2 = jnp.dot(jnp.concatenate(ps, axis=1), v2, preferred_element_type=F32)
        o2 = o2 * jnp.where(olane < ATT_HD, inv[0], inv[1])
        for s in range(2):
            o_ref[0, :, (2 * kh + s) * LANES:(2 * kh + s + 1) * LANES] = o2[s * BLK:(s + 1) * BLK].astype(BF16)


def _attention(sink, q, k, v, kc, vc):
    b, l, qw = q.shape
    local = k is not None
    nb = l // ATT_BLOCK
    full = lambda a: pl.BlockSpec((1,) + a.shape[1:], lambda i, j: (i, 0, 0))
    blk = pl.BlockSpec((1, ATT_BLOCK, qw), lambda i, j: (i, j, 0))
    args = [sink, q] + ([k, v] if local else []) + [kc, vc]
    in_specs = [pl.BlockSpec(memory_space=pltpu.SMEM), blk] + [full(a) for a in args[2:]]
    return pl.pallas_call(
        functools.partial(_attn_kernel, local=local, nb=nb),
        grid=(b, nb), in_specs=in_specs, out_specs=blk,
        out_shape=jax.ShapeDtypeStruct((b, l, qw), BF16),
        compiler_params=_cparams(2),
        name="attn_window" if local else "attn_ctx",
    )(*args)


def _merge_mlp_kernel(x_ref, yr_ref, yp_ref, ya_ref, gt_ref, mod_ref, nw_ref,
                      wr_ref, wp_ref, wa_ref, wo_ref, w1_ref, w2_ref, o_ref, *, d, hid_chunk):
    mod = mod_ref[0]
    g1, sh2, sc2, g2 = mod[:, 2 * d:3 * d], mod[:, 3 * d:4 * d], mod[:, 4 * d:5 * d], mod[:, 5 * d:6 * d]
    y = None
    for i, (y_ref, w_ref) in enumerate(((yr_ref, wr_ref), (yp_ref, wp_ref), (ya_ref, wa_ref))):
        proj = jnp.dot(y_ref[0], w_ref[...], preferred_element_type=F32)
        term = _sigmoid(gt_ref[0, :, i * d:(i + 1) * d].astype(F32)) * proj
        y = term if y is None else y + term
    mix = jnp.dot(y.astype(BF16), wo_ref[...], preferred_element_type=F32)
    x1 = x_ref[0] + g1 * mix
    ms = jnp.mean(x1 * x1, axis=-1, keepdims=True)
    h = ((x1 * lax.rsqrt(ms + EPS) * nw_ref[...]) * (1.0 + sc2) + sh2).astype(BF16)
    acc = None
    hidden = w1_ref.shape[1]
    for c in range(hidden // hid_chunk):
        a = jnp.dot(h, w1_ref[:, c * hid_chunk:(c + 1) * hid_chunk], preferred_element_type=F32)
        a = jnp.maximum(a, 0.0)
        part = jnp.dot((a * a).astype(BF16), w2_ref[c * hid_chunk:(c + 1) * hid_chunk, :], preferred_element_type=F32)
        acc = part if acc is None else acc + part
    o_ref[0] = x1 + g2 * acc


def _merge_mlp(x3, y_ret, y_pool, y_att, gates, mod3, mod_row, nw, weights, *, tm):
    bx, l, d = x3.shape
    row = lambda b, i: (b, i, 0)
    bw = y_ret.shape[2]
    in_specs = [pl.BlockSpec((1, tm, d), row)] + [pl.BlockSpec((1, tm, bw), row)] * 3 + [
        pl.BlockSpec((1, tm, N_BRANCH * d), row),
        pl.BlockSpec((1, 1, mod3.shape[2]), (lambda b, i: (b, 0, 0)) if mod_row is None
                     else (lambda b, i: (mod_row, 0, 0))),
        pl.BlockSpec((1, d), lambda b, i: (0, 0))] + [_resident(w.shape) for w in weights]
    return pl.pallas_call(
        functools.partial(_merge_mlp_kernel, d=d, hid_chunk=1024),
        grid=(bx, l // tm), in_specs=in_specs, out_specs=pl.BlockSpec((1, tm, d), row),
        out_shape=jax.ShapeDtypeStruct((bx, l, d), F32),
        compiler_params=_cparams(2),
        name="merge_mlp",
    )(x3, y_ret, y_pool, y_att, gates, mod3, nw, *weights)


def _rope_tables(s):
    pos = jnp.arange(s, dtype=F32)
    half = RET_DK // 2
    ang = pos[:, None] * (RET_ROPE_BASE ** (-jnp.arange(half, dtype=F32) / half))[None, :]
    cos_r = jnp.tile(jnp.concatenate([jnp.cos(ang), jnp.cos(ang)], axis=1), (1, LANES // RET_DK))
    sin_r = jnp.tile(jnp.concatenate([-jnp.sin(ang), jnp.sin(ang)], axis=1), (1, LANES // RET_DK))
    q = ATT_HD // 4
    freq = ROPE_BASE ** (-jnp.arange(q, dtype=F32) / q)
    ar = (jnp.arange(s) // GRID_W).astype(F32)[:, None] * freq[None, :]
    ac = (jnp.arange(s) % GRID_W).astype(F32)[:, None] * freq[None, :]
    cos_a = jnp.concatenate([jnp.cos(ar), jnp.cos(ar), jnp.cos(ac), jnp.cos(ac)], axis=1)
    sin_a = jnp.concatenate([-jnp.sin(ar), jnp.sin(ar), -jnp.sin(ac), jnp.sin(ac)], axis=1)
    return cos_r, sin_r, jnp.tile(cos_a, (1, LANES // ATT_HD)), jnp.tile(sin_a, (1, LANES // ATT_HD))


def kernel(x, c, ctx, c_ctx, norm1_w, norm2_w, ada_w, ada_b, w_in, ret_decay, pool_w, pool_scale, q_norm_w, k_norm_w, attn_sink, w_ret_out, w_pool_out, w_attn_out, w_out, w_mlp1, w_mlp2):
    b, s, d = x.shape
    lc = ctx.shape[1]
    depth = w_in.shape[0]
    tm = 512 if s % 512 == 0 else 256
    tmc = 512 if (b * lc) % 512 == 0 else 256

    rows = -(-(b + 1) // 8) * 8
    cc = jnp.zeros((rows, d), F32).at[:b].set(c).at[b].set(c_ctx)
    mod = _modulation(cc, ada_w, ada_b)

    tables = _rope_tables(s)
    gi = jnp.arange(LANES) // ATT_HD
    gmat = jnp.where(gi[:, None] == gi[None, :], 1.0 / ATT_HD, 0.0).astype(BF16)
    xc = ctx.reshape(1, b * lc, d)

    for l in range(depth):
        last = l == depth - 1
        mod3 = mod[l].reshape(rows, 1, 6 * d)
        log_gamma = jnp.log1p(-jnp.exp2(-ret_decay[l].astype(F32)))
        w_l = w_in[l].astype(BF16)
        nw1, nw2 = norm1_w[l].reshape(1, d), norm2_w[l].reshape(1, d)
        qnw = jnp.tile(q_norm_w[l], LANES // ATT_HD).reshape(1, LANES)
        knw = jnp.tile(k_norm_w[l], LANES // ATT_HD).reshape(1, LANES)
        pw = pool_w[l].astype(BF16)
        psc = pool_scale[l].reshape(1, -1)
        weights = [w_ret_out[l].astype(BF16), w_pool_out[l].astype(BF16), w_attn_out[l].astype(BF16),
                   w_out[l].astype(BF16), w_mlp1[l].astype(BF16), w_mlp2[l].astype(BF16)]

        cw = w_l[:, :1024] if last else w_l
        cparts = _inproj(xc, mod3, b, nw1, cw, qnw, knw, gmat, None, tm=tmc, ctx_only=last)
        cparts = [p.reshape(b, lc, p.shape[-1]) for p in cparts]
        rk_c, rv_c, ak_c, av_c = cparts[:4]

        rk, rv, ak, av, rq, rg, aq, pu, gt = _inproj(x, mod3, None, nw1, w_l, qnw, knw, gmat, tables,
                                                     tm=tm, ctx_only=False)
        if last:
            _, y_ret = _retention(log_gamma, (rk_c, rv_c), (rq, rk, rv, rg), ctx_out=False)
        else:
            rq_c, rg_c, aq_c, pu_c, gt_c = cparts[4:]
            y_ret_c, y_ret = _retention(log_gamma, (rq_c, rk_c, rv_c, rg_c), (rq, rk, rv, rg), ctx_out=True)
        y_att = _attention(attn_sink[l], aq, ak, av, ak_c, av_c)
        y_pool = _pool(pu, pw, psc)
        x = _merge_mlp(x, y_ret, y_pool, y_att, gt, mod3, None, nw2, weights, tm=tm)

        if not last:
            y_att_c = _attention(attn_sink[l], aq_c, None, None, ak_c, av_c)
            y_pool_c = _pool(pu_c, pw, psc)
            flat = lambda a: a.reshape(1, b * lc, a.shape[-1])
            xc = _merge_mlp(xc, flat(y_ret_c), flat(y_pool_c), flat(y_att_c), flat(gt_c), mod3, b, nw2,
                            weights, tm=tmc)
    return x
```

```python
import functools

import jax
import jax.numpy as jnp
from jax import lax
from jax.experimental import pallas as pl
from jax.experimental.pallas import tpu as pltpu

F32 = jnp.float32
BF16 = jnp.bfloat16

EPS = 1e-6
LOG2E = 1.4426950408889634
LANES = 128
RET_HEADS, RET_DK, RET_DV, RET_CHUNK = 4, 64, 128, 128
RET_SLABS = RET_HEADS // 2
RET_ROPE_BASE = 10000.0
POOL_GROUPS, POOL_GW = 4, 128
POOL_WINDOWS = (2, 4, 8, 16)
ATT_HEADS, ATT_KV_HEADS, ATT_HD = 8, 2, 64
ATT_WINDOW = ATT_BLOCK = 128
ROPE_BASE = 10000.0
GRID_W = 64
N_BRANCH = 3
NEG_BIG = -1e30
VMEM_LIMIT = 56 * 1024 * 1024


def _cparams(n_axes):
    return pltpu.CompilerParams(dimension_semantics=("arbitrary",) * n_axes,
                                vmem_limit_bytes=VMEM_LIMIT)


def _resident(shape):
    zeros = (0,) * len(shape)
    return pl.BlockSpec(shape, lambda *_: zeros, pipeline_mode=pl.Buffered(1))


def _sigmoid(x):
    return 1.0 / (1.0 + jnp.exp(-x))


def _mod_kernel(cc_ref, w_ref, b_ref, o_ref):
    cc = cc_ref[...]
    s = (cc * _sigmoid(cc)).astype(BF16)
    o_ref[0] = jnp.dot(s, w_ref[0].astype(BF16), preferred_element_type=F32) + b_ref[0]


def _modulation(cc, ada_w, ada_b):
    depth, d, n = ada_w.shape
    r = cc.shape[0]
    tn = 1024
    return pl.pallas_call(
        _mod_kernel,
        grid=(depth, n // tn),
        in_specs=[pl.BlockSpec((r, d), lambda l, j: (0, 0)),
                  pl.BlockSpec((1, d, tn), lambda l, j: (l, 0, j)),
                  pl.BlockSpec((1, 1, tn), lambda l, j: (l, 0, j))],
        out_specs=pl.BlockSpec((1, r, tn), lambda l, j: (l, 0, j)),
        out_shape=jax.ShapeDtypeStruct((depth, r, n), F32),
        compiler_params=_cparams(2),
        name="adaln_mod",
    )(cc, ada_w, ada_b.reshape(depth, 1, n))


def _swap_halves(x, half):
    lane = lax.broadcasted_iota(jnp.int32, x.shape, 1)
    first = (lane % (2 * half)) < half
    return jnp.where(first, pltpu.roll(x, LANES - half, 1), pltpu.roll(x, half, 1))


def _rope(x, cos, sin, half):
    return x * cos + _swap_halves(x, half) * sin


def _inproj_kernel(*refs, d, ctx_only, use_rope):
    x_ref, mod_ref, nw_ref, w_ref, qnw_ref, knw_ref = refs[:6]
    pos = 6
    if use_rope:
        cr_ref, sr_ref, ca_ref, sa_ref = refs[6:10]
        pos = 10
    outs = refs[pos:]
    tm = x_ref.shape[1]

    x = x_ref[0]
    mod = mod_ref[0]
    shift, scale = mod[:, 0:d], mod[:, d:2 * d]
    ms = jnp.mean(x * x, axis=-1, keepdims=True)
    u = ((x * lax.rsqrt(ms + EPS) * nw_ref[...]) * (1.0 + scale) + shift).astype(BF16)

    def proj(a, b):
        return jnp.dot(u, w_ref[:, a:b], preferred_element_type=F32)

    def slab(a, s):
        return a[:, s * LANES:(s + 1) * LANES]

    lane = lax.broadcasted_iota(jnp.int32, (tm, LANES), 1)
    low = lane < ATT_HD

    def qk_norm(a, w):
        sq = a * a
        lo = jnp.sum(jnp.where(low, sq, 0.0), axis=-1, keepdims=True)
        hi = jnp.sum(jnp.where(low, 0.0, sq), axis=-1, keepdims=True)
        msq = jnp.where(low, lo, hi) * (1.0 / ATT_HD)
        return a * lax.rsqrt(msq + EPS) * w

    def ret_rope(a):
        return _rope(a, cr_ref[...], sr_ref[...], RET_DK // 2) if use_rope else a

    def att_rope(a):
        return _rope(a, ca_ref[...], sa_ref[...], ATT_HD // 4) if use_rope else a

    def store_kv(a, out_ref):
        sw = pltpu.roll(a, ATT_HD, 1)
        zero = jnp.zeros_like(a)
        lo_hi = ((jnp.where(low, a, zero), jnp.where(low, zero, sw)),
                 (jnp.where(low, sw, zero), jnp.where(low, zero, a)))
        for kh in range(ATT_KV_HEADS):
            for j in range(tm // ATT_BLOCK):
                for e in range(2):
                    out_ref[kh, 0, (2 * j + e) * ATT_BLOCK:(2 * j + e + 1) * ATT_BLOCK, :] = \
                        lo_hi[kh][e][j * ATT_BLOCK:(j + 1) * ATT_BLOCK].astype(BF16)

    rk_ref, rv_ref, ak_ref, av_ref = outs[:4]
    a = proj(0, 256)
    for s in range(RET_SLABS):
        rk_ref[0, :, s * LANES:(s + 1) * LANES] = (ret_rope(slab(a, s)) * RET_DK ** -0.5).astype(BF16)
    rv_ref[0] = proj(256, 768).astype(BF16)
    a = proj(768, 1024)
    store_kv(att_rope(qk_norm(slab(a, 0), knw_ref[...])), ak_ref)
    store_kv(slab(a, 1), av_ref)
    if ctx_only:
        return
    rq_ref, rg_ref, aq_ref, pu_ref, gt_ref = outs[4:]
    a = proj(1024, 1280)
    for s in range(RET_SLABS):
        rq_ref[0, :, s * LANES:(s + 1) * LANES] = ret_rope(slab(a, s)).astype(BF16)
    rg_ref[0] = proj(1280, 1792).astype(BF16)
    a = proj(1792, 2304)
    for s in range(ATT_HEADS // 2):
        qs = att_rope(qk_norm(slab(a, s), qnw_ref[...])) * (ATT_HD ** -0.5 * LOG2E)
        aq_ref[0, :, s * LANES:(s + 1) * LANES] = qs.astype(BF16)
    pu_ref[0] = proj(2304, 2816).astype(BF16)
    for s in range(6):
        gt_ref[0, :, s * 512:(s + 1) * 512] = proj(2816 + s * 512, 2816 + (s + 1) * 512).astype(BF16)


def _inproj(x3, mod3, mod_row, nw, w, qnw, knw, tables, *, tm, ctx_only):
    bx, l, d = x3.shape
    ncol = w.shape[1]
    use_rope = tables is not None
    grid = (bx, l // tm)
    row = lambda b, i: (b, i, 0)
    in_specs = [pl.BlockSpec((1, tm, d), row),
                pl.BlockSpec((1, 1, mod3.shape[2]), (lambda b, i: (b, 0, 0)) if mod_row is None
                             else (lambda b, i: (mod_row, 0, 0))),
                pl.BlockSpec((1, d), lambda b, i: (0, 0)),
                _resident((d, ncol)),
                pl.BlockSpec((1, LANES), lambda b, i: (0, 0)),
                pl.BlockSpec((1, LANES), lambda b, i: (0, 0))]
    args = [x3, mod3, nw, w, qnw, knw]
    if use_rope:
        in_specs += [pl.BlockSpec((tm, LANES), lambda b, i: (i, 0))] * 4
        args += list(tables)
    kv_spec = pl.BlockSpec((ATT_KV_HEADS, 1, 2 * tm, LANES), lambda b, i: (0, b, i, 0))
    kv_shape = jax.ShapeDtypeStruct((ATT_KV_HEADS, bx, 2 * l, LANES), BF16)
    tok = lambda wd: (pl.BlockSpec((1, tm, wd), row), jax.ShapeDtypeStruct((bx, l, wd), BF16))
    outs = [tok(256), tok(512), (kv_spec, kv_shape), (kv_spec, kv_shape)]
    if not ctx_only:
        outs += [tok(256), tok(512), tok(512), tok(512), tok(3072)]
    return pl.pallas_call(
        functools.partial(_inproj_kernel, d=d, ctx_only=ctx_only, use_rope=use_rope),
        grid=grid, in_specs=in_specs, out_specs=[o[0] for o in outs], out_shape=[o[1] for o in outs],
        compiler_params=_cparams(2),
        name="inproj_ctx" if not use_rope else "inproj",
    )(*args)


def _ret_kernel(*refs, nc, n, ctx_out):
    lg_ref = refs[0]
    if ctx_out:
        qc_ref, kc_ref, vc_ref, gc_ref, q_ref, k_ref, v_ref, g_ref, yc_ref, y_ref = refs[1:11]
        scr = refs[11:]
    else:
        kc_ref, vc_ref, q_ref, k_ref, v_ref, g_ref, y_ref = refs[1:8]
        qc_ref = gc_ref = yc_ref = None
        scr = refs[8:]
    dmat_ref, zf_ref, zb_ref, xf_ref, xb_ref, gf_ref, gb_ref, rf_ref, rb_ref, st_ref = scr
    C = RET_CHUNK
    W2 = 2 * RET_DV

    ri = lax.broadcasted_iota(jnp.int32, (C, LANES), 0)
    ci = lax.broadcasted_iota(jnp.int32, (C, LANES), 1)
    low = ci < RET_DK
    r2 = lax.broadcasted_iota(jnp.int32, (C, W2), 0)
    c2 = lax.broadcasted_iota(jnp.int32, (C, W2), 1)
    even_col = c2 < RET_DV
    diag = (r2 < RET_DK) == even_col
    r2f = r2.astype(F32)
    rel = (r2 - (c2 % RET_DV)).astype(F32)
    for s in range(RET_SLABS):
        lgf2 = jnp.where(even_col, lg_ref[0, 2 * s], lg_ref[0, 2 * s + 1])
        lgb2 = jnp.where(even_col, lg_ref[1, 2 * s], lg_ref[1, 2 * s + 1])
        dmat_ref[s] = jnp.where(rel >= 0, jnp.exp(jnp.maximum(rel, 0.0) * lgf2), 0.0) \
            + jnp.where(rel <= 0, jnp.exp(jnp.maximum(-rel, 0.0) * lgb2), 0.0)
        zf_ref[s] = jnp.exp((C - 1 - r2f) * lgf2)
        zb_ref[s] = jnp.exp(r2f * lgb2)
        gf_ref[s] = jnp.exp(jnp.full((C, W2), C, F32) * lgf2)
        gb_ref[s] = jnp.exp(jnp.full((C, W2), C, F32) * lgb2)
        lgf1 = jnp.where(low, lg_ref[0, 2 * s], lg_ref[0, 2 * s + 1])
        lgb1 = jnp.where(low, lg_ref[1, 2 * s], lg_ref[1, 2 * s + 1])
        rif = ri.astype(F32)
        xf_ref[s] = jnp.exp((rif + 1.0) * lgf1)
        xb_ref[s] = jnp.exp((C - rif) * lgb1)
        rf_ref[s] = jnp.zeros((C, W2), F32)
        rb_ref[s] = jnp.zeros((C, W2), F32)

    def state_increment(ks, v2, z):
        vz = (v2.astype(F32) * z).astype(BF16)
        inc = lax.dot_general(ks, vz, (((0,), (0,)), ((), ())), preferred_element_type=F32)
        return jnp.where(diag, inc, 0.0)

    def bwd_chunk(k_blk, v_blk, t):
        for s in range(RET_SLABS):
            st_ref[t, s, C:2 * C, :] = rb_ref[s].astype(BF16)
            inc = state_increment(k_blk[:, s * LANES:(s + 1) * LANES], v_blk[:, s * W2:(s + 1) * W2], zb_ref[s])
            rb_ref[s] = gb_ref[s] * rb_ref[s] + inc

    def fwd_chunk(q_blk, k_blk, v_blk, g_blk, t, store):
        for s in range(RET_SLABS):
            qs = q_blk[:, s * LANES:(s + 1) * LANES]
            ks = k_blk[:, s * LANES:(s + 1) * LANES]
            v2 = v_blk[:, s * W2:(s + 1) * W2]
            zk = jnp.zeros_like(ks)
            k2 = jnp.concatenate([jnp.where(low, ks, zk), jnp.where(low, zk, ks)], axis=0)
            sc = lax.dot_general(qs, k2, (((1,), (1,)), ((), ())), preferred_element_type=F32) * dmat_ref[s]
            zv = jnp.zeros_like(v2)
            vv = jnp.concatenate([jnp.where(even_col, v2, zv), jnp.where(even_col, zv, v2)], axis=0)
            y2 = jnp.dot(sc.astype(BF16), vv, preferred_element_type=F32)
            qf = qs.astype(F32)
            qx = jnp.concatenate([(qf * xf_ref[s]).astype(BF16), (qf * xb_ref[s]).astype(BF16)], axis=1)
            st_ref[t, s, 0:C, :] = rf_ref[s].astype(BF16)
            y2 = y2 + jnp.dot(qx, st_ref[t, s], preferred_element_type=F32)
            if store is not None:
                for e in range(2):
                    h = 2 * s + e
                    y = y2[:, e * RET_DV:(e + 1) * RET_DV]
                    gh = g_blk[:, h * RET_DV:(h + 1) * RET_DV].astype(F32)
                    yn = y * lax.rsqrt(jnp.mean(y * y, axis=-1, keepdims=True) + EPS)
                    store(h, (gh * _sigmoid(gh) * yn).astype(BF16))
            rf_ref[s] = gf_ref[s] * rf_ref[s] + state_increment(ks, v2, zf_ref[s])

    for t in reversed(range(nc)):
        bwd_chunk(kc_ref[0, t * C:(t + 1) * C, :], vc_ref[0, t * C:(t + 1) * C, :], t)

    def bwd_body(i, carry):
        j = n - 1 - i
        r0 = pl.multiple_of(j * C, C)
        bwd_chunk(k_ref[0, pl.ds(r0, C), :], v_ref[0, pl.ds(r0, C), :], nc + j)
        return carry
    lax.fori_loop(0, n, bwd_body, 0)

    for t in range(nc):
        sl = slice(t * C, (t + 1) * C)
        if ctx_out:
            def store_c(h, val, sl=sl):
                yc_ref[0, sl, h * RET_DV:(h + 1) * RET_DV] = val
            fwd_chunk(qc_ref[0, sl, :], kc_ref[0, sl, :], vc_ref[0, sl, :], gc_ref[0, sl, :], t, store_c)
        else:
            for s in range(RET_SLABS):
                rf_ref[s] = gf_ref[s] * rf_ref[s] + state_increment(
                    kc_ref[0, sl, s * LANES:(s + 1) * LANES], vc_ref[0, sl, s * W2:(s + 1) * W2], zf_ref[s])

    def fwd_body(j, carry):
        r0 = pl.multiple_of(j * C, C)
        rows = pl.ds(r0, C)

        def store(h, val):
            y_ref[0, rows, h * RET_DV:(h + 1) * RET_DV] = val
        fwd_chunk(q_ref[0, rows, :], k_ref[0, rows, :], v_ref[0, rows, :], g_ref[0, rows, :], nc + j, store)
        return carry
    lax.fori_loop(0, n, fwd_body, 0)


def _retention(log_gamma, ctx_parts, lat_parts, *, ctx_out):
    b, s, _ = lat_parts[0].shape
    lc = ctx_parts[-1].shape[1]
    nc, n = lc // RET_CHUNK, s // RET_CHUNK
    full = lambda a: pl.BlockSpec((1,) + a.shape[1:], lambda i: (i, 0, 0))
    in_specs = [pl.BlockSpec(memory_space=pltpu.SMEM)] + [full(a) for a in (*ctx_parts, *lat_parts)]
    vw = RET_HEADS * RET_DV
    out_shape = [jax.ShapeDtypeStruct((b, s, vw), BF16)]
    if ctx_out:
        out_shape = [jax.ShapeDtypeStruct((b, lc, vw), BF16)] + out_shape
    out_specs = [full(o) for o in out_shape]
    wide = (RET_SLABS, RET_CHUNK, 2 * RET_DV)
    narrow = (RET_SLABS, RET_CHUNK, LANES)
    scratch = [pltpu.VMEM(wide, F32),
               pltpu.VMEM(wide, F32), pltpu.VMEM(wide, F32),
               pltpu.VMEM(narrow, F32), pltpu.VMEM(narrow, F32),
               pltpu.VMEM(wide, F32), pltpu.VMEM(wide, F32),
               pltpu.VMEM(wide, F32), pltpu.VMEM(wide, F32),
               pltpu.VMEM((nc + n, RET_SLABS, 2 * RET_CHUNK, 2 * RET_DV), BF16)]
    res = pl.pallas_call(
        functools.partial(_ret_kernel, nc=nc, n=n, ctx_out=ctx_out),
        grid=(b,), in_specs=in_specs, out_specs=out_specs, out_shape=out_shape,
        scratch_shapes=scratch, compiler_params=_cparams(1),
        name="retention",
    )(log_gamma, *ctx_parts, *lat_parts)
    return res if ctx_out else (None, res[0])


def _pool_kernel(u_ref, w_ref, sc_ref, o_ref, pa_ref, pb_ref, *, l):
    pad = 16
    t = lax.broadcasted_iota(jnp.int32, (l, LANES), 0)
    zeros_edge = jnp.zeros((pad, LANES), F32)
    ext = l + 16
    for g, w in enumerate(POOL_WINDOWS):
        ug = u_ref[0, :, g * POOL_GW:(g + 1) * POOL_GW].astype(F32)
        for buf in (pa_ref, pb_ref):
            buf[0:pad, :] = zeros_edge
            buf[pad + l:pad + l + pad, :] = zeros_edge
        pa_ref[pad:pad + l, :] = ug
        pb_ref[8:8 + ext, :] = pa_ref[7:7 + ext, :] + pa_ref[8:8 + ext, :]
        src, dst = pb_ref, pa_ref
        step = 1
        while 2 * step < w:
            dst[8:8 + ext, :] = src[8 - step:8 - step + ext, :] + src[8 + step:8 + step + ext, :]
            src, dst = dst, src
            step *= 2
        hw = w // 2
        cnt = (jnp.minimum(t + hw, l) - jnp.maximum(t - hw, 0)).astype(F32)
        pooled = src[pad:pad + l, :] / cnt
        mixed = jnp.dot((pooled - ug).astype(BF16), w_ref[g], preferred_element_type=F32)
        o_ref[0, :, g * POOL_GW:(g + 1) * POOL_GW] = (mixed * sc_ref[:, g * POOL_GW:(g + 1) * POOL_GW]).astype(BF16)


def _pool(u, w_grp, scale):
    b, l, pw = u.shape
    return pl.pallas_call(
        functools.partial(_pool_kernel, l=l),
        grid=(b,),
        in_specs=[pl.BlockSpec((1, l, pw), lambda i: (i, 0, 0)),
                  pl.BlockSpec(w_grp.shape, lambda i: (0, 0, 0)),
                  pl.BlockSpec((1, pw), lambda i: (0, 0))],
        out_specs=pl.BlockSpec((1, l, pw), lambda i: (i, 0, 0)),
        out_shape=jax.ShapeDtypeStruct((b, l, pw), BF16),
        scratch_shapes=[pltpu.VMEM((l + 32, LANES), F32), pltpu.VMEM((l + 32, LANES), F32)],
        compiler_params=_cparams(1),
        name="pool",
    )(u, w_grp, scale)


def _attn_kernel(*refs, local, nq, ncb):
    sink_ref = refs[0]
    if local:
        q_ref, k_ref, v_ref, kc_ref, vc_ref, o_ref = refs[1:]
    else:
        q_ref, kc_ref, vc_ref, o_ref = refs[1:]
    BLK = ATT_BLOCK
    qi = lax.broadcasted_iota(jnp.int32, (2 * BLK, BLK), 0) % BLK
    kj = lax.broadcasted_iota(jnp.int32, (2 * BLK, BLK), 1)
    tri_prev = jnp.where(kj >= qi, 0.0, NEG_BIG)
    tri_next = jnp.where(kj <= qi, 0.0, NEG_BIG)
    row2 = lax.broadcasted_iota(jnp.int32, (2 * BLK, 1), 0)
    olane = lax.broadcasted_iota(jnp.int32, (2 * BLK, LANES), 1)
    nt = (((1,), (1,)), ((), ()))

    def block(nblk, has_prev, has_next):
        r0 = nblk * BLK if isinstance(nblk, int) else pl.multiple_of(nblk * BLK, BLK)
        rows = pl.ds(r0, BLK)
        kinds = ((["prev"] if has_prev else []) + ["cur"] + (["next"] if has_next else [])) if local else []
        for kh in range(ATT_KV_HEADS):
            q2 = jnp.concatenate([q_ref[0, rows, (2 * kh + s) * LANES:(2 * kh + s + 1) * LANES] for s in range(2)],
                                 axis=0)
            chunks = ([], [])
            if local:
                w0 = (nblk - int(has_prev)) * 2 * BLK
                if not isinstance(w0, int):
                    w0 = pl.multiple_of(w0, 2 * BLK)
                win = pl.ds(w0, len(kinds) * 2 * BLK)
                s_loc = lax.dot_general(q2, k_ref[kh, 0, win, :], nt, preferred_element_type=F32)
                for j, kind in enumerate(kinds):
                    for e in range(2):
                        c = s_loc[:, (2 * j + e) * BLK:(2 * j + e + 1) * BLK]
                        if kind == "prev":
                            c = c + tri_prev
                        elif kind == "next":
                            c = c + tri_next
                        chunks[e].append(c)
            s_ctx = lax.dot_general(q2, kc_ref[kh, 0], nt, preferred_element_type=F32)
            for j in range(ncb):
                for e in range(2):
                    chunks[e].append(s_ctx[:, (2 * j + e) * BLK:(2 * j + e + 1) * BLK])
            probs, inv = [], []
            for e in range(2):
                sink = jnp.where(row2 < BLK, sink_ref[kh * 4 + e], sink_ref[kh * 4 + 2 + e]) * LOG2E
                mx = functools.reduce(jnp.maximum, chunks[e])
                m = jnp.maximum(jnp.max(mx, axis=-1, keepdims=True), sink)
                ps = [jnp.exp2(c - m) for c in chunks[e]]
                den = jnp.sum(functools.reduce(jnp.add, ps), axis=-1, keepdims=True) + jnp.exp2(sink - m)
                probs.append([p.astype(BF16) for p in ps])
                inv.append(1.0 / den)
            nloc = len(kinds)
            p_ctx = jnp.concatenate([probs[e][nloc + j] for j in range(ncb) for e in range(2)], axis=1)
            o2 = jnp.dot(p_ctx, vc_ref[kh, 0], preferred_element_type=F32)
            if local:
                p_loc = jnp.concatenate([probs[e][j] for j in range(nloc) for e in range(2)], axis=1)
                o2 = o2 + jnp.dot(p_loc, v_ref[kh, 0, win, :], preferred_element_type=F32)
            o2 = o2 * jnp.where(olane < ATT_HD, inv[0], inv[1])
            for s in range(2):
                o_ref[0, rows, (2 * kh + s) * LANES:(2 * kh + s + 1) * LANES] = o2[s * BLK:(s + 1) * BLK].astype(BF16)

    if not local:
        for nblk in range(nq):
            block(nblk, False, False)
    else:
        block(0, False, nq > 1)
        if nq > 2:
            def body(nblk, carry):
                block(nblk, True, True)
                return carry
            lax.fori_loop(1, nq - 1, body, 0, unroll=2)
        if nq > 1:
            block(nq - 1, True, False)


def _attention(sink, q, k, v, kc, vc):
    b, l, qw = q.shape
    local = k is not None
    tok = pl.BlockSpec((1, l, qw), lambda i: (i, 0, 0))
    kv = lambda a: pl.BlockSpec((ATT_KV_HEADS, 1) + a.shape[2:], lambda i: (0, i, 0, 0))
    args = [sink, q] + ([k, v] if local else []) + [kc, vc]
    in_specs = [pl.BlockSpec(memory_space=pltpu.SMEM), tok] + [kv(a) for a in args[2:]]
    return pl.pallas_call(
        functools.partial(_attn_kernel, local=local, nq=l // ATT_BLOCK, ncb=kc.shape[2] // (2 * ATT_BLOCK)),
        grid=(b,), in_specs=in_specs, out_specs=tok,
        out_shape=jax.ShapeDtypeStruct((b, l, qw), BF16),
        compiler_params=_cparams(1),
        name="attn_window" if local else "attn_ctx",
    )(*args)


def _merge_mlp_kernel(x_ref, yr_ref, yp_ref, ya_ref, gt_ref, mod_ref, nw_ref,
                      wr_ref, wp_ref, wa_ref, wo_ref, w1_ref, w2_ref, o_ref, *, d, hid_chunk):
    mod = mod_ref[0]
    g1, sh2, sc2, g2 = mod[:, 2 * d:3 * d], mod[:, 3 * d:4 * d], mod[:, 4 * d:5 * d], mod[:, 5 * d:6 * d]
    y = None
    for i, (y_ref, w_ref) in enumerate(((yr_ref, wr_ref), (yp_ref, wp_ref), (ya_ref, wa_ref))):
        proj = jnp.dot(y_ref[0], w_ref[...], preferred_element_type=F32)
        term = _sigmoid(gt_ref[0, :, i * d:(i + 1) * d].astype(F32)) * proj
        y = term if y is None else y + term
    mix = jnp.dot(y.astype(BF16), wo_ref[...], preferred_element_type=F32)
    x1 = x_ref[0] + g1 * mix
    ms = jnp.mean(x1 * x1, axis=-1, keepdims=True)
    h = ((x1 * lax.rsqrt(ms + EPS) * nw_ref[...]) * (1.0 + sc2) + sh2).astype(BF16)
    acc = None
    hidden = w1_ref.shape[1]
    for c in range(hidden // hid_chunk):
        a = jnp.dot(h, w1_ref[:, c * hid_chunk:(c + 1) * hid_chunk], preferred_element_type=F32)
        a = jnp.maximum(a, 0.0)
        part = jnp.dot((a * a).astype(BF16), w2_ref[c * hid_chunk:(c + 1) * hid_chunk, :], preferred_element_type=F32)
        acc = part if acc is None else acc + part
    o_ref[0] = x1 + g2 * acc


def _merge_mlp(x3, y_ret, y_pool, y_att, gates, mod3, mod_row, nw, weights, *, tm):
    bx, l, d = x3.shape
    row = lambda b, i: (b, i, 0)
    bw = y_ret.shape[2]
    in_specs = [pl.BlockSpec((1, tm, d), row)] + [pl.BlockSpec((1, tm, bw), row)] * 3 + [
        pl.BlockSpec((1, tm, N_BRANCH * d), row),
        pl.BlockSpec((1, 1, mod3.shape[2]), (lambda b, i: (b, 0, 0)) if mod_row is None
                     else (lambda b, i: (mod_row, 0, 0))),
        pl.BlockSpec((1, d), lambda b, i: (0, 0))] + [_resident(w.shape) for w in weights]
    return pl.pallas_call(
        functools.partial(_merge_mlp_kernel, d=d, hid_chunk=1024),
        grid=(bx, l // tm), in_specs=in_specs, out_specs=pl.BlockSpec((1, tm, d), row),
        out_shape=jax.ShapeDtypeStruct((bx, l, d), F32),
        compiler_params=_cparams(2),
        name="merge_mlp",
    )(x3, y_ret, y_pool, y_att, gates, mod3, nw, *weights)


def _rope_tables(s):
    pos = jnp.arange(s, dtype=F32)
    half = RET_DK // 2
    ang = pos[:, None] * (RET_ROPE_BASE ** (-jnp.arange(half, dtype=F32) / half))[None, :]
    cos_r = jnp.tile(jnp.concatenate([jnp.cos(ang), jnp.cos(ang)], axis=1), (1, LANES // RET_DK))
    sin_r = jnp.tile(jnp.concatenate([-jnp.sin(ang), jnp.sin(ang)], axis=1), (1, LANES // RET_DK))
    q = ATT_HD // 4
    freq = ROPE_BASE ** (-jnp.arange(q, dtype=F32) / q)
    ar = (jnp.arange(s) // GRID_W).astype(F32)[:, None] * freq[None, :]
    ac = (jnp.arange(s) % GRID_W).astype(F32)[:, None] * freq[None, :]
    cos_a = jnp.concatenate([jnp.cos(ar), jnp.cos(ar), jnp.cos(ac), jnp.cos(ac)], axis=1)
    sin_a = jnp.concatenate([-jnp.sin(ar), jnp.sin(ar), -jnp.sin(ac), jnp.sin(ac)], axis=1)
    return cos_r, sin_r, jnp.tile(cos_a, (1, LANES // ATT_HD)), jnp.tile(sin_a, (1, LANES // ATT_HD))


def kernel(x, c, ctx, c_ctx, norm1_w, norm2_w, ada_w, ada_b, w_in, ret_decay, pool_w, pool_scale, q_norm_w, k_norm_w, attn_sink, w_ret_out, w_pool_out, w_attn_out, w_out, w_mlp1, w_mlp2):
    b, s, d = x.shape
    lc = ctx.shape[1]
    depth = w_in.shape[0]
    tm = 512 if s % 512 == 0 else 256
    tmc = 512 if (b * lc) % 512 == 0 else 256

    rows = -(-(b + 1) // 8) * 8
    cc = jnp.zeros((rows, d), F32).at[:b].set(c).at[b].set(c_ctx)
    mod = _modulation(cc, ada_w, ada_b)

    tables = _rope_tables(s)
    xc = ctx.reshape(1, b * lc, d)

    for l in range(depth):
        last = l == depth - 1
        mod3 = mod[l].reshape(rows, 1, 6 * d)
        log_gamma = jnp.log1p(-jnp.exp2(-ret_decay[l].astype(F32)))
        w_l = w_in[l].astype(BF16)
        nw1, nw2 = norm1_w[l].reshape(1, d), norm2_w[l].reshape(1, d)
        qnw = jnp.tile(q_norm_w[l], LANES // ATT_HD).reshape(1, LANES)
        knw = jnp.tile(k_norm_w[l], LANES // ATT_HD).reshape(1, LANES)
        pw = pool_w[l].astype(BF16)
        psc = pool_scale[l].reshape(1, -1)
        weights = [w_ret_out[l].astype(BF16), w_pool_out[l].astype(BF16), w_attn_out[l].astype(BF16),
                   w_out[l].astype(BF16), w_mlp1[l].astype(BF16), w_mlp2[l].astype(BF16)]
        per_batch = lambda p: p.reshape(b, lc, p.shape[-1])
        kv_per_batch = lambda p: p.reshape(ATT_KV_HEADS, b, 2 * lc, LANES)

        cw = w_l[:, :1024] if last else w_l
        cparts = _inproj(xc, mod3, b, nw1, cw, qnw, knw, None, tm=tmc, ctx_only=last)
        rk_c, rv_c = per_batch(cparts[0]), per_batch(cparts[1])
        ak_c, av_c = kv_per_batch(cparts[2]), kv_per_batch(cparts[3])

        rk, rv, ak, av, rq, rg, aq, pu, gt = _inproj(x, mod3, None, nw1, w_l, qnw, knw, tables,
                                                     tm=tm, ctx_only=False)
        if last:
            _, y_ret = _retention(log_gamma, (rk_c, rv_c), (rq, rk, rv, rg), ctx_out=False)
        else:
            rq_c, rg_c, aq_c, pu_c, gt_c = [per_batch(p) for p in cparts[4:]]
            y_ret_c, y_ret = _retention(log_gamma, (rq_c, rk_c, rv_c, rg_c), (rq, rk, rv, rg), ctx_out=True)
        y_att = _attention(attn_sink[l], aq, ak, av, ak_c, av_c)
        y_pool = _pool(pu, pw, psc)
        x = _merge_mlp(x, y_ret, y_pool, y_att, gt, mod3, None, nw2, weights, tm=tm)

        if not last:
            y_att_c = _attention(attn_sink[l], aq_c, None, None, ak_c, av_c)
            y_pool_c = _pool(pu_c, pw, psc)
            flat = lambda a: a.reshape(1, b * lc, a.shape[-1])
            xc = _merge_mlp(xc, flat(y_ret_c), flat(y_pool_c), flat(y_att_c), flat(gt_c), mod3, b, nw2,
                            weights, tm=tmc)
    return x
```

```python
import functools

import jax
import jax.numpy as jnp
from jax import lax
from jax.experimental import pallas as pl
from jax.experimental.pallas import tpu as pltpu

F32 = jnp.float32
BF16 = jnp.bfloat16

EPS = 1e-6
LOG2E = 1.4426950408889634
LANES = 128
RET_HEADS, RET_DK, RET_DV, RET_CHUNK = 4, 64, 128, 128
RET_SLABS = RET_HEADS // 2
RET_ROPE_BASE = 10000.0
POOL_GROUPS, POOL_GW = 4, 128
POOL_WINDOWS = (2, 4, 8, 16)
ATT_HEADS, ATT_KV_HEADS, ATT_HD = 8, 2, 64
ATT_WINDOW = ATT_BLOCK = 128
ROPE_BASE = 10000.0
GRID_W = 64
N_BRANCH = 3
NEG_BIG = -1e30
VMEM_LIMIT = 56 * 1024 * 1024


def _cparams(n_axes):
    return pltpu.CompilerParams(dimension_semantics=("arbitrary",) * n_axes,
                                vmem_limit_bytes=VMEM_LIMIT)


def _resident(shape):
    zeros = (0,) * len(shape)
    return pl.BlockSpec(shape, lambda *_: zeros, pipeline_mode=pl.Buffered(1))


def _sigmoid(x):
    return 1.0 / (1.0 + jnp.exp(-x))


def _mod_kernel(cc_ref, w_ref, b_ref, o_ref):
    cc = cc_ref[...]
    s = (cc * _sigmoid(cc)).astype(BF16)
    o_ref[0] = jnp.dot(s, w_ref[0].astype(BF16), preferred_element_type=F32) + b_ref[0]


def _modulation(cc, ada_w, ada_b):
    depth, d, n = ada_w.shape
    r = cc.shape[0]
    tn = 1024
    return pl.pallas_call(
        _mod_kernel,
        grid=(depth, n // tn),
        in_specs=[pl.BlockSpec((r, d), lambda l, j: (0, 0)),
                  pl.BlockSpec((1, d, tn), lambda l, j: (l, 0, j)),
                  pl.BlockSpec((1, 1, tn), lambda l, j: (l, 0, j))],
        out_specs=pl.BlockSpec((1, r, tn), lambda l, j: (l, 0, j)),
        out_shape=jax.ShapeDtypeStruct((depth, r, n), F32),
        compiler_params=_cparams(2),
        name="adaln_mod",
    )(cc, ada_w, ada_b.reshape(depth, 1, n))


def _swap_halves(x, half):
    lane = lax.broadcasted_iota(jnp.int32, x.shape, 1)
    first = (lane % (2 * half)) < half
    return jnp.where(first, pltpu.roll(x, LANES - half, 1), pltpu.roll(x, half, 1))


def _rope(x, cos, sin, half):
    return x * cos + _swap_halves(x, half) * sin


def _inproj_kernel(*refs, d, ctx_only, use_rope):
    x_ref, mod_ref, nw_ref, w_ref, qnw_ref, knw_ref = refs[:6]
    pos = 6
    if use_rope:
        cr_ref, sr_ref, ca_ref, sa_ref = refs[6:10]
        pos = 10
    outs = refs[pos:]
    tm = x_ref.shape[1]

    x = x_ref[0]
    mod = mod_ref[0]
    shift, scale = mod[:, 0:d], mod[:, d:2 * d]
    ms = jnp.mean(x * x, axis=-1, keepdims=True)
    u = ((x * lax.rsqrt(ms + EPS) * nw_ref[...]) * (1.0 + scale) + shift).astype(BF16)

    def proj(a, b):
        return jnp.dot(u, w_ref[:, a:b], preferred_element_type=F32)

    def slab(a, s):
        return a[:, s * LANES:(s + 1) * LANES]

    lane = lax.broadcasted_iota(jnp.int32, (tm, LANES), 1)
    low = lane < ATT_HD

    def qk_norm(a, w):
        sq = a * a
        lo = jnp.sum(jnp.where(low, sq, 0.0), axis=-1, keepdims=True)
        hi = jnp.sum(jnp.where(low, 0.0, sq), axis=-1, keepdims=True)
        msq = jnp.where(low, lo, hi) * (1.0 / ATT_HD)
        return a * lax.rsqrt(msq + EPS) * w

    def ret_rope(a):
        return _rope(a, cr_ref[...], sr_ref[...], RET_DK // 2) if use_rope else a

    def att_rope(a):
        return _rope(a, ca_ref[...], sa_ref[...], ATT_HD // 4) if use_rope else a

    def store_kv(a, out_ref):
        sw = pltpu.roll(a, ATT_HD, 1)
        zero = jnp.zeros_like(a)
        lo_hi = ((jnp.where(low, a, zero), jnp.where(low, zero, sw)),
                 (jnp.where(low, sw, zero), jnp.where(low, zero, a)))
        for kh in range(ATT_KV_HEADS):
            for j in range(tm // ATT_BLOCK):
                for e in range(2):
                    out_ref[kh, 0, (2 * j + e) * ATT_BLOCK:(2 * j + e + 1) * ATT_BLOCK, :] = \
                        lo_hi[kh][e][j * ATT_BLOCK:(j + 1) * ATT_BLOCK].astype(BF16)

    rk_ref, rv_ref, ak_ref, av_ref = outs[:4]
    a = proj(0, 256)
    for s in range(RET_SLABS):
        rk_ref[0, :, s * LANES:(s + 1) * LANES] = (ret_rope(slab(a, s)) * RET_DK ** -0.5).astype(BF16)
    rv_ref[0] = proj(256, 768).astype(BF16)
    a = proj(768, 1024)
    store_kv(att_rope(qk_norm(slab(a, 0), knw_ref[...])), ak_ref)
    store_kv(slab(a, 1), av_ref)
    if ctx_only:
        return
    rq_ref, rg_ref, aq_ref, pu_ref, gt_ref = outs[4:]
    a = proj(1024, 1280)
    for s in range(RET_SLABS):
        rq_ref[0, :, s * LANES:(s + 1) * LANES] = ret_rope(slab(a, s)).astype(BF16)
    rg_ref[0] = proj(1280, 1792).astype(BF16)
    a = proj(1792, 2304)
    for s in range(ATT_HEADS // 2):
        qs = att_rope(qk_norm(slab(a, s), qnw_ref[...])) * (ATT_HD ** -0.5 * LOG2E)
        aq_ref[0, :, s * LANES:(s + 1) * LANES] = qs.astype(BF16)
    pu_ref[0] = proj(2304, 2816).astype(BF16)
    for s in range(6):
        gt_ref[0, :, s * 512:(s + 1) * 512] = proj(2816 + s * 512, 2816 + (s + 1) * 512).astype(BF16)


def _inproj(x3, mod3, mod_row, nw, w, qnw, knw, tables, *, tm, ctx_only):
    bx, l, d = x3.shape
    ncol = w.shape[1]
    use_rope = tables is not None
    grid = (bx, l // tm)
    row = lambda b, i: (b, i, 0)
    in_specs = [pl.BlockSpec((1, tm, d), row),
                pl.BlockSpec((1, 1, mod3.shape[2]), (lambda b, i: (b, 0, 0)) if mod_row is None
                             else (lambda b, i: (mod_row, 0, 0))),
                pl.BlockSpec((1, d), lambda b, i: (0, 0)),
                _resident((d, ncol)),
                pl.BlockSpec((1, LANES), lambda b, i: (0, 0)),
                pl.BlockSpec((1, LANES), lambda b, i: (0, 0))]
    args = [x3, mod3, nw, w, qnw, knw]
    if use_rope:
        in_specs += [pl.BlockSpec((tm, LANES), lambda b, i: (i, 0))] * 4
        args += list(tables)
    kv_spec = pl.BlockSpec((ATT_KV_HEADS, 1, 2 * tm, LANES), lambda b, i: (0, b, i, 0))
    kv_shape = jax.ShapeDtypeStruct((ATT_KV_HEADS, bx, 2 * l, LANES), BF16)
    tok = lambda wd: (pl.BlockSpec((1, tm, wd), row), jax.ShapeDtypeStruct((bx, l, wd), BF16))
    outs = [tok(256), tok(512), (kv_spec, kv_shape), (kv_spec, kv_shape)]
    if not ctx_only:
        outs += [tok(256), tok(512), tok(512), tok(512), tok(3072)]
    return pl.pallas_call(
        functools.partial(_inproj_kernel, d=d, ctx_only=ctx_only, use_rope=use_rope),
        grid=grid, in_specs=in_specs, out_specs=[o[0] for o in outs], out_shape=[o[1] for o in outs],
        compiler_params=_cparams(2),
        name="inproj_ctx" if not use_rope else "inproj",
    )(*args)


def _ret_kernel(*refs, nc, n, ctx_out):
    lg_ref = refs[0]
    if ctx_out:
        qc_ref, kc_ref, vc_ref, gc_ref, q_ref, k_ref, v_ref, g_ref, yc_ref, y_ref = refs[1:11]
        scr = refs[11:]
    else:
        kc_ref, vc_ref, q_ref, k_ref, v_ref, g_ref, y_ref = refs[1:8]
        qc_ref = gc_ref = yc_ref = None
        scr = refs[8:]
    dmat_ref, zf_ref, zb_ref, xf_ref, xb_ref, gf_ref, gb_ref, rf_ref, rb_ref, st_ref, uf_ref = scr
    C = RET_CHUNK
    W2 = 2 * RET_DV

    ri = lax.broadcasted_iota(jnp.int32, (C, LANES), 0)
    ci = lax.broadcasted_iota(jnp.int32, (C, LANES), 1)
    low = ci < RET_DK
    r2 = lax.broadcasted_iota(jnp.int32, (C, W2), 0)
    c2 = lax.broadcasted_iota(jnp.int32, (C, W2), 1)
    even_col = c2 < RET_DV
    diag = (r2 < RET_DK) == even_col
    r2f = r2.astype(F32)
    rel = (r2 - (c2 % RET_DV)).astype(F32)
    for s in range(RET_SLABS):
        lgf2 = jnp.where(even_col, lg_ref[0, 2 * s], lg_ref[0, 2 * s + 1])
        lgb2 = jnp.where(even_col, lg_ref[1, 2 * s], lg_ref[1, 2 * s + 1])
        dmat_ref[s] = jnp.where(rel >= 0, jnp.exp(jnp.maximum(rel, 0.0) * lgf2), 0.0) \
            + jnp.where(rel <= 0, jnp.exp(jnp.maximum(-rel, 0.0) * lgb2), 0.0)
        zf_ref[s] = jnp.exp((C - 1 - r2f) * lgf2)
        zb_ref[s] = jnp.exp(r2f * lgb2)
        gf_ref[s] = jnp.exp(jnp.full((C, W2), C, F32) * lgf2)
        gb_ref[s] = jnp.exp(jnp.full((C, W2), C, F32) * lgb2)
        lgf1 = jnp.where(low, lg_ref[0, 2 * s], lg_ref[0, 2 * s + 1])
        lgb1 = jnp.where(low, lg_ref[1, 2 * s], lg_ref[1, 2 * s + 1])
        rif = ri.astype(F32)
        xf_ref[s] = jnp.exp((rif + 1.0) * lgf1)
        xb_ref[s] = jnp.exp((C - rif) * lgb1)
        rf_ref[s] = jnp.zeros((C, W2), F32)
        rb_ref[s] = jnp.zeros((C, W2), F32)

    def bwd_chunk(k_blk, v_blk, t):
        for s in range(RET_SLABS):
            st_ref[t, s, C:2 * C, :] = rb_ref[s].astype(BF16)
            vf = v_blk[:, s * W2:(s + 1) * W2].astype(F32)
            vz = jnp.concatenate([(vf * zf_ref[s]).astype(BF16), (vf * zb_ref[s]).astype(BF16)], axis=1)
            inc = lax.dot_general(k_blk[:, s * LANES:(s + 1) * LANES], vz, (((0,), (0,)), ((), ())),
                                  preferred_element_type=F32)
            uf_ref[t, s] = jnp.where(diag, inc[:, 0:W2], 0.0)
            rb_ref[s] = gb_ref[s] * rb_ref[s] + jnp.where(diag, inc[:, W2:2 * W2], 0.0)

    def fwd_chunk(q_blk, k_blk, v_blk, g_blk, t, store):
        for s in range(RET_SLABS):
            qs = q_blk[:, s * LANES:(s + 1) * LANES]
            ks = k_blk[:, s * LANES:(s + 1) * LANES]
            v2 = v_blk[:, s * W2:(s + 1) * W2]
            zk = jnp.zeros_like(ks)
            k2 = jnp.concatenate([jnp.where(low, ks, zk), jnp.where(low, zk, ks)], axis=0)
            sc = lax.dot_general(qs, k2, (((1,), (1,)), ((), ())), preferred_element_type=F32) * dmat_ref[s]
            zv = jnp.zeros_like(v2)
            vv = jnp.concatenate([jnp.where(even_col, v2, zv), jnp.where(even_col, zv, v2)], axis=0)
            y2 = jnp.dot(sc.astype(BF16), vv, preferred_element_type=F32)
            qf = qs.astype(F32)
            qx = jnp.concatenate([(qf * xf_ref[s]).astype(BF16), (qf * xb_ref[s]).astype(BF16)], axis=1)
            st_ref[t, s, 0:C, :] = rf_ref[s].astype(BF16)
            y2 = y2 + jnp.dot(qx, st_ref[t, s], preferred_element_type=F32)
            if store is not None:
                for e in range(2):
                    h = 2 * s + e
                    y = y2[:, e * RET_DV:(e + 1) * RET_DV]
                    gh = g_blk[:, h * RET_DV:(h + 1) * RET_DV].astype(F32)
                    yn = y * lax.rsqrt(jnp.mean(y * y, axis=-1, keepdims=True) + EPS)
                    store(h, (gh * _sigmoid(gh) * yn).astype(BF16))
            rf_ref[s] = gf_ref[s] * rf_ref[s] + uf_ref[t, s]

    for t in reversed(range(nc)):
        bwd_chunk(kc_ref[0, t * C:(t + 1) * C, :], vc_ref[0, t * C:(t + 1) * C, :], t)

    def bwd_body(i, carry):
        j = n - 1 - i
        r0 = pl.multiple_of(j * C, C)
        bwd_chunk(k_ref[0, pl.ds(r0, C), :], v_ref[0, pl.ds(r0, C), :], nc + j)
        return carry
    lax.fori_loop(0, n, bwd_body, 0)

    for t in range(nc):
        sl = slice(t * C, (t + 1) * C)
        if ctx_out:
            def store_c(h, val, sl=sl):
                yc_ref[0, sl, h * RET_DV:(h + 1) * RET_DV] = val
            fwd_chunk(qc_ref[0, sl, :], kc_ref[0, sl, :], vc_ref[0, sl, :], gc_ref[0, sl, :], t, store_c)
        else:
            for s in range(RET_SLABS):
                rf_ref[s] = gf_ref[s] * rf_ref[s] + uf_ref[t, s]

    def fwd_body(j, carry):
        r0 = pl.multiple_of(j * C, C)
        rows = pl.ds(r0, C)

        def store(h, val):
            y_ref[0, rows, h * RET_DV:(h + 1) * RET_DV] = val
        fwd_chunk(q_ref[0, rows, :], k_ref[0, rows, :], v_ref[0, rows, :], g_ref[0, rows, :], nc + j, store)
        return carry
    lax.fori_loop(0, n, fwd_body, 0, unroll=2)


def _retention(log_gamma, ctx_parts, lat_parts, *, ctx_out):
    b, s, _ = lat_parts[0].shape
    lc = ctx_parts[-1].shape[1]
    nc, n = lc // RET_CHUNK, s // RET_CHUNK
    full = lambda a: pl.BlockSpec((1,) + a.shape[1:], lambda i: (i, 0, 0))
    in_specs = [pl.BlockSpec(memory_space=pltpu.SMEM)] + [full(a) for a in (*ctx_parts, *lat_parts)]
    vw = RET_HEADS * RET_DV
    out_shape = [jax.ShapeDtypeStruct((b, s, vw), BF16)]
    if ctx_out:
        out_shape = [jax.ShapeDtypeStruct((b, lc, vw), BF16)] + out_shape
    out_specs = [full(o) for o in out_shape]
    wide = (RET_SLABS, RET_CHUNK, 2 * RET_DV)
    narrow = (RET_SLABS, RET_CHUNK, LANES)
    scratch = [pltpu.VMEM(wide, F32),
               pltpu.VMEM(wide, F32), pltpu.VMEM(wide, F32),
               pltpu.VMEM(narrow, F32), pltpu.VMEM(narrow, F32),
               pltpu.VMEM(wide, F32), pltpu.VMEM(wide, F32),
               pltpu.VMEM(wide, F32), pltpu.VMEM(wide, F32),
               pltpu.VMEM((nc + n, RET_SLABS, 2 * RET_CHUNK, 2 * RET_DV), BF16),
               pltpu.VMEM((nc + n, RET_SLABS, RET_CHUNK, 2 * RET_DV), F32)]
    res = pl.pallas_call(
        functools.partial(_ret_kernel, nc=nc, n=n, ctx_out=ctx_out),
        grid=(b,), in_specs=in_specs, out_specs=out_specs, out_shape=out_shape,
        scratch_shapes=scratch, compiler_params=_cparams(1),
        name="retention",
    )(log_gamma, *ctx_parts, *lat_parts)
    return res if ctx_out else (None, res[0])


def _pool_kernel(u_ref, w_ref, sc_ref, o_ref, pa_ref, pb_ref, *, l):
    pad = 16
    t = lax.broadcasted_iota(jnp.int32, (l, LANES), 0)
    zeros_edge = jnp.zeros((pad, LANES), F32)
    ext = l + 16
    for g, w in enumerate(POOL_WINDOWS):
        ug = u_ref[0, :, g * POOL_GW:(g + 1) * POOL_GW].astype(F32)
        for buf in (pa_ref, pb_ref):
            buf[0:pad, :] = zeros_edge
            buf[pad + l:pad + l + pad, :] = zeros_edge
        pa_ref[pad:pad + l, :] = ug
        pb_ref[8:8 + ext, :] = pa_ref[7:7 + ext, :] + pa_ref[8:8 + ext, :]
        src, dst = pb_ref, pa_ref
        step = 1
        while 2 * step < w:
            dst[8:8 + ext, :] = src[8 - step:8 - step + ext, :] + src[8 + step:8 + step + ext, :]
            src, dst = dst, src
            step *= 2
        hw = w // 2
        cnt = (jnp.minimum(t + hw, l) - jnp.maximum(t - hw, 0)).astype(F32)
        pooled = src[pad:pad + l, :] / cnt
        mixed = jnp.dot((pooled - ug).astype(BF16), w_ref[g], preferred_element_type=F32)
        o_ref[0, :, g * POOL_GW:(g + 1) * POOL_GW] = (mixed * sc_ref[:, g * POOL_GW:(g + 1) * POOL_GW]).astype(BF16)


def _pool(u, w_grp, scale):
    b, l, pw = u.shape
    return pl.pallas_call(
        functools.partial(_pool_kernel, l=l),
        grid=(b,),
        in_specs=[pl.BlockSpec((1, l, pw), lambda i: (i, 0, 0)),
                  pl.BlockSpec(w_grp.shape, lambda i: (0, 0, 0)),
                  pl.BlockSpec((1, pw), lambda i: (0, 0))],
        out_specs=pl.BlockSpec((1, l, pw), lambda i: (i, 0, 0)),
        out_shape=jax.ShapeDtypeStruct((b, l, pw), BF16),
        scratch_shapes=[pltpu.VMEM((l + 32, LANES), F32), pltpu.VMEM((l + 32, LANES), F32)],
        compiler_params=_cparams(1),
        name="pool",
    )(u, w_grp, scale)


def _attn_kernel(*refs, local, nq, ncb):
    sink_ref = refs[0]
    if local:
        q_ref, k_ref, v_ref, kc_ref, vc_ref, o_ref, p_ref, sa_ref, sb_ref = refs[1:]
    else:
        q_ref, kc_ref, vc_ref, o_ref, p_ref, sa_ref = refs[1:]
        k_ref = v_ref = None
    BLK = ATT_BLOCK
    KB = 2 * BLK
    nloc = 3 if local else 0
    qi = lax.broadcasted_iota(jnp.int32, (2 * BLK, BLK), 0) % BLK
    kj = lax.broadcasted_iota(jnp.int32, (2 * BLK, BLK), 1)
    tri_prev = jnp.where(kj >= qi, 0.0, NEG_BIG)
    tri_next = jnp.where(kj <= qi, 0.0, NEG_BIG)
    nt = (((1,), (1,)), ((), ()))

    def block_rows(nblk):
        if isinstance(nblk, int):
            nb = min(nblk, nq - 1)
            ds = lambda blk, size: pl.ds(blk * size, size)
            neigh = [max(nb - 1, 0), nb, min(nb + 1, nq - 1)]
        else:
            nb = jnp.minimum(nblk, nq - 1)
            ds = lambda blk, size: pl.ds(pl.multiple_of(blk * size, size), size)
            neigh = [jnp.maximum(nb - 1, 0), nb, jnp.minimum(nb + 1, nq - 1)]
        return nb, ds(nb, BLK), [ds(blk, KB) for blk in neigh[:nloc]]

    def key_block(ref, cref, kh, j, windows):
        return ref[kh, 0, windows[j], :] if j < nloc else cref[kh, 0, (j - nloc) * KB:(j - nloc + 1) * KB, :]

    def scores(nblk, s_ref):
        nb, rows, windows = block_rows(nblk)
        if local:
            bias_prev = tri_prev + jnp.where(nb == 0, NEG_BIG, 0.0)
            bias_next = tri_next + jnp.where(nb == nq - 1, NEG_BIG, 0.0)
        for kh in range(ATT_KV_HEADS):
            q2 = jnp.concatenate([q_ref[0, rows, (2 * kh + s) * LANES:(2 * kh + s + 1) * LANES] for s in range(2)],
                                 axis=0)
            for j in range(nloc + ncb):
                sc = lax.dot_general(q2, key_block(k_ref, kc_ref, kh, j, windows), nt, preferred_element_type=F32)
                for e in range(2):
                    c = sc[:, e * BLK:(e + 1) * BLK]
                    if local and j == 0:
                        c = c + bias_prev
                    if local and j == 2:
                        c = c + bias_next
                    s_ref[kh, :, (2 * j + e) * BLK:(2 * j + e + 1) * BLK] = c

    def softmax_pv(nblk, s_ref):
        nb, rows, windows = block_rows(nblk)
        lane1 = lax.broadcasted_iota(jnp.int32, (BLK, LANES), 1)
        for kh in range(ATT_KV_HEADS):
            inv = [[None, None], [None, None]]
            for s in range(2):
                srows = slice(s * BLK, (s + 1) * BLK)
                for e in range(2):
                    cols = [slice((2 * j + e) * BLK, (2 * j + e + 1) * BLK) for j in range(nloc + ncb)]
                    sink = sink_ref[kh * 4 + 2 * s + e] * LOG2E
                    mx = s_ref[kh, srows, cols[0]]
                    for cl in cols[1:]:
                        mx = jnp.maximum(mx, s_ref[kh, srows, cl])
                    m = jnp.maximum(jnp.max(mx, axis=-1, keepdims=True), sink)
                    tot = None
                    for cl in cols:
                        p = jnp.exp2(s_ref[kh, srows, cl] - m)
                        tot = p if tot is None else tot + p
                        p_ref[kh, srows, cl] = p.astype(BF16)
                    den = jnp.sum(tot, axis=-1, keepdims=True) + jnp.exp2(sink - m)
                    inv[s][e] = 1.0 / den
            o2 = None
            for j in range(nloc + ncb):
                part = jnp.dot(p_ref[kh, :, j * KB:(j + 1) * KB], key_block(v_ref, vc_ref, kh, j, windows),
                               preferred_element_type=F32)
                o2 = part if o2 is None else o2 + part
            for s in range(2):
                scale = jnp.where(lane1 < ATT_HD, inv[s][0], inv[s][1])
                o_ref[0, rows, (2 * kh + s) * LANES:(2 * kh + s + 1) * LANES] = \
                    (o2[s * BLK:(s + 1) * BLK] * scale).astype(BF16)

    if not local:
        for nblk in range(nq):
            scores(nblk, sa_ref)
            softmax_pv(nblk, sa_ref)
        return

    scores(0, sa_ref)

    def body(i, carry):
        nblk = 2 * i
        scores(nblk + 1, sb_ref)
        softmax_pv(nblk, sa_ref)
        scores(nblk + 2, sa_ref)
        softmax_pv(nblk + 1, sb_ref)
        return carry
    lax.fori_loop(0, nq // 2, body, 0)


def _attention(sink, q, k, v, kc, vc):
    b, l, qw = q.shape
    local = k is not None
    nq = l // ATT_BLOCK
    ncb = kc.shape[2] // (2 * ATT_BLOCK)
    assert not local or nq % 2 == 0
    tok = pl.BlockSpec((1, l, qw), lambda i: (i, 0, 0))
    kv = lambda a: pl.BlockSpec((ATT_KV_HEADS, 1) + a.shape[2:], lambda i: (0, i, 0, 0))
    args = [sink, q] + ([k, v] if local else []) + [kc, vc]
    in_specs = [pl.BlockSpec(memory_space=pltpu.SMEM), tok] + [kv(a) for a in args[2:]]
    nkb = (3 if local else 0) + ncb
    tile = (ATT_KV_HEADS, 2 * ATT_BLOCK, nkb * 2 * ATT_BLOCK)
    scratch = [pltpu.VMEM(tile, BF16)] + [pltpu.VMEM(tile, F32)] * (2 if local else 1)
    return pl.pallas_call(
        functools.partial(_attn_kernel, local=local, nq=nq, ncb=ncb),
        grid=(b,), in_specs=in_specs, out_specs=tok,
        out_shape=jax.ShapeDtypeStruct((b, l, qw), BF16),
        scratch_shapes=scratch,
        compiler_params=_cparams(1),
        name="attn_window" if local else "attn_ctx",
    )(*args)


def _merge_mlp_kernel(x_ref, yr_ref, yp_ref, ya_ref, gt_ref, mod_ref, nw_ref,
                      wr_ref, wp_ref, wa_ref, wo_ref, w1_ref, w2_ref, o_ref, *, d, hid_chunk):
    mod = mod_ref[0]
    g1, sh2, sc2, g2 = mod[:, 2 * d:3 * d], mod[:, 3 * d:4 * d], mod[:, 4 * d:5 * d], mod[:, 5 * d:6 * d]
    y = None
    for i, (y_ref, w_ref) in enumerate(((yr_ref, wr_ref), (yp_ref, wp_ref), (ya_ref, wa_ref))):
        proj = jnp.dot(y_ref[0], w_ref[...], preferred_element_type=F32)
        term = _sigmoid(gt_ref[0, :, i * d:(i + 1) * d].astype(F32)) * proj
        y = term if y is None else y + term
    mix = jnp.dot(y.astype(BF16), wo_ref[...], preferred_element_type=F32)
    x1 = x_ref[0] + g1 * mix
    ms = jnp.mean(x1 * x1, axis=-1, keepdims=True)
    h = ((x1 * lax.rsqrt(ms + EPS) * nw_ref[...]) * (1.0 + sc2) + sh2).astype(BF16)
    acc = None
    hidden = w1_ref.shape[1]
    for c in range(hidden // hid_chunk):
        a = jnp.dot(h, w1_ref[:, c * hid_chunk:(c + 1) * hid_chunk], preferred_element_type=F32)
        a = jnp.maximum(a, 0.0)
        part = jnp.dot((a * a).astype(BF16), w2_ref[c * hid_chunk:(c + 1) * hid_chunk, :], preferred_element_type=F32)
        acc = part if acc is None else acc + part
    o_ref[0] = x1 + g2 * acc


def _merge_mlp(x3, y_ret, y_pool, y_att, gates, mod3, mod_row, nw, weights, *, tm):
    bx, l, d = x3.shape
    row = lambda b, i: (b, i, 0)
    bw = y_ret.shape[2]
    in_specs = [pl.BlockSpec((1, tm, d), row)] + [pl.BlockSpec((1, tm, bw), row)] * 3 + [
        pl.BlockSpec((1, tm, N_BRANCH * d), row),
        pl.BlockSpec((1, 1, mod3.shape[2]), (lambda b, i: (b, 0, 0)) if mod_row is None
                     else (lambda b, i: (mod_row, 0, 0))),
        pl.BlockSpec((1, d), lambda b, i: (0, 0))] + [_resident(w.shape) for w in weights]
    return pl.pallas_call(
        functools.partial(_merge_mlp_kernel, d=d, hid_chunk=1024),
        grid=(bx, l // tm), in_specs=in_specs, out_specs=pl.BlockSpec((1, tm, d), row),
        out_shape=jax.ShapeDtypeStruct((bx, l, d), F32),
        compiler_params=_cparams(2),
        name="merge_mlp",
    )(x3, y_ret, y_pool, y_att, gates, mod3, nw, *weights)


def _rope_tables(s):
    pos = jnp.arange(s, dtype=F32)
    half = RET_DK // 2
    ang = pos[:, None] * (RET_ROPE_BASE ** (-jnp.arange(half, dtype=F32) / half))[None, :]
    cos_r = jnp.tile(jnp.concatenate([jnp.cos(ang), jnp.cos(ang)], axis=1), (1, LANES // RET_DK))
    sin_r = jnp.tile(jnp.concatenate([-jnp.sin(ang), jnp.sin(ang)], axis=1), (1, LANES // RET_DK))
    q = ATT_HD // 4
    freq = ROPE_BASE ** (-jnp.arange(q, dtype=F32) / q)
    ar = (jnp.arange(s) // GRID_W).astype(F32)[:, None] * freq[None, :]
    ac = (jnp.arange(s) % GRID_W).astype(F32)[:, None] * freq[None, :]
    cos_a = jnp.concatenate([jnp.cos(ar), jnp.cos(ar), jnp.cos(ac), jnp.cos(ac)], axis=1)
    sin_a = jnp.concatenate([-jnp.sin(ar), jnp.sin(ar), -jnp.sin(ac), jnp.sin(ac)], axis=1)
    return cos_r, sin_r, jnp.tile(cos_a, (1, LANES // ATT_HD)), jnp.tile(sin_a, (1, LANES // ATT_HD))


def kernel(x, c, ctx, c_ctx, norm1_w, norm2_w, ada_w, ada_b, w_in, ret_decay, pool_w, pool_scale, q_norm_w, k_norm_w, attn_sink, w_ret_out, w_pool_out, w_attn_out, w_out, w_mlp1, w_mlp2):
    b, s, d = x.shape
    lc = ctx.shape[1]
    depth = w_in.shape[0]
    tm = 512 if s % 512 == 0 else 256
    tmc = 512 if (b * lc) % 512 == 0 else 256

    rows = -(-(b + 1) // 8) * 8
    cc = jnp.zeros((rows, d), F32).at[:b].set(c).at[b].set(c_ctx)
    mod = _modulation(cc, ada_w, ada_b)

    tables = _rope_tables(s)
    xc = ctx.reshape(1, b * lc, d)

    for l in range(depth):
        last = l == depth - 1
        mod3 = mod[l].reshape(rows, 1, 6 * d)
        log_gamma = jnp.log1p(-jnp.exp2(-ret_decay[l].astype(F32)))
        w_l = w_in[l].astype(BF16)
        nw1, nw2 = norm1_w[l].reshape(1, d), norm2_w[l].reshape(1, d)
        qnw = jnp.tile(q_norm_w[l], LANES // ATT_HD).reshape(1, LANES)
        knw = jnp.tile(k_norm_w[l], LANES // ATT_HD).reshape(1, LANES)
        pw = pool_w[l].astype(BF16)
        psc = pool_scale[l].reshape(1, -1)
        weights = [w_ret_out[l].astype(BF16), w_pool_out[l].astype(BF16), w_attn_out[l].astype(BF16),
                   w_out[l].astype(BF16), w_mlp1[l].astype(BF16), w_mlp2[l].astype(BF16)]
        per_batch = lambda p: p.reshape(b, lc, p.shape[-1])
        kv_per_batch = lambda p: p.reshape(ATT_KV_HEADS, b, 2 * lc, LANES)

        cw = w_l[:, :1024] if last else w_l
        cparts = _inproj(xc, mod3, b, nw1, cw, qnw, knw, None, tm=tmc, ctx_only=last)
        rk_c, rv_c = per_batch(cparts[0]), per_batch(cparts[1])
        ak_c, av_c = kv_per_batch(cparts[2]), kv_per_batch(cparts[3])

        rk, rv, ak, av, rq, rg, aq, pu, gt = _inproj(x, mod3, None, nw1, w_l, qnw, knw, tables,
                                                     tm=tm, ctx_only=False)
        if last:
            _, y_ret = _retention(log_gamma, (rk_c, rv_c), (rq, rk, rv, rg), ctx_out=False)
        else:
            rq_c, rg_c, aq_c, pu_c, gt_c = [per_batch(p) for p in cparts[4:]]
            y_ret_c, y_ret = _retention(log_gamma, (rq_c, rk_c, rv_c, rg_c), (rq, rk, rv, rg), ctx_out=True)
        y_att = _attention(attn_sink[l], aq, ak, av, ak_c, av_c)
        y_pool = _pool(pu, pw, psc)
        x = _merge_mlp(x, y_ret, y_pool, y_att, gt, mod3, None, nw2, weights, tm=tm)

        if not last:
            y_att_c = _attention(attn_sink[l], aq_c, None, None, ak_c, av_c)
            y_pool_c = _pool(pu_c, pw, psc)
            flat = lambda a: a.reshape(1, b * lc, a.shape[-1])
            xc = _merge_mlp(xc, flat(y_ret_c), flat(y_pool_c), flat(y_att_c), flat(gt_c), mod3, b, nw2,
                            weights, tm=tmc)
    return x
```

```python
import functools

import jax
import jax.numpy as jnp
from jax import lax
from jax.experimental import pallas as pl
from jax.experimental.pallas import tpu as pltpu

F32 = jnp.float32
BF16 = jnp.bfloat16

EPS = 1e-6
LOG2E = 1.4426950408889634
LANES = 128
RET_HEADS, RET_DK, RET_DV, RET_CHUNK = 4, 64, 128, 128
RET_SLABS = RET_HEADS // 2
RET_ROPE_BASE = 10000.0
POOL_GROUPS, POOL_GW = 4, 128
POOL_WINDOWS = (2, 4, 8, 16)
ATT_HEADS, ATT_KV_HEADS, ATT_HD = 8, 2, 64
ATT_WINDOW = ATT_BLOCK = 128
ROPE_BASE = 10000.0
GRID_W = 64
N_BRANCH = 3
NEG_BIG = -1e30
VMEM_LIMIT = 56 * 1024 * 1024


def _cparams(n_axes):
    return pltpu.CompilerParams(dimension_semantics=("arbitrary",) * n_axes,
                                vmem_limit_bytes=VMEM_LIMIT)


def _resident(shape):
    zeros = (0,) * len(shape)
    return pl.BlockSpec(shape, lambda *_: zeros, pipeline_mode=pl.Buffered(1))


def _sigmoid(x):
    return 1.0 / (1.0 + jnp.exp(-x))


def _mod_kernel(cc_ref, w_ref, b_ref, o_ref):
    cc = cc_ref[...]
    s = (cc * _sigmoid(cc)).astype(BF16)
    o_ref[0] = jnp.dot(s, w_ref[0].astype(BF16), preferred_element_type=F32) + b_ref[0]


def _modulation(cc, ada_w, ada_b):
    depth, d, n = ada_w.shape
    r = cc.shape[0]
    tn = 1024
    return pl.pallas_call(
        _mod_kernel,
        grid=(depth, n // tn),
        in_specs=[pl.BlockSpec((r, d), lambda l, j: (0, 0)),
                  pl.BlockSpec((1, d, tn), lambda l, j: (l, 0, j)),
                  pl.BlockSpec((1, 1, tn), lambda l, j: (l, 0, j))],
        out_specs=pl.BlockSpec((1, r, tn), lambda l, j: (l, 0, j)),
        out_shape=jax.ShapeDtypeStruct((depth, r, n), F32),
        compiler_params=_cparams(2),
        name="adaln_mod",
    )(cc, ada_w, ada_b.reshape(depth, 1, n))


def _swap_halves(x, half):
    lane = lax.broadcasted_iota(jnp.int32, x.shape, 1)
    first = (lane % (2 * half)) < half
    return jnp.where(first, pltpu.roll(x, LANES - half, 1), pltpu.roll(x, half, 1))


def _rope(x, cos, sin, half):
    return x * cos + _swap_halves(x, half) * sin


def _inproj_kernel(*refs, d, ctx_only, use_rope):
    x_ref, mod_ref, nw_ref, w_ref, qnw_ref, knw_ref = refs[:6]
    pos = 6
    if use_rope:
        cr_ref, sr_ref, ca_ref, sa_ref = refs[6:10]
        pos = 10
    outs = refs[pos:]
    tm = x_ref.shape[1]

    x = x_ref[0]
    mod = mod_ref[0]
    shift, scale = mod[:, 0:d], mod[:, d:2 * d]
    ms = jnp.mean(x * x, axis=-1, keepdims=True)
    u = ((x * lax.rsqrt(ms + EPS) * nw_ref[...]) * (1.0 + scale) + shift).astype(BF16)

    def proj(a, b):
        return jnp.dot(u, w_ref[:, a:b], preferred_element_type=F32)

    def slab(a, s):
        return a[:, s * LANES:(s + 1) * LANES]

    lane = lax.broadcasted_iota(jnp.int32, (tm, LANES), 1)
    low = lane < ATT_HD

    def qk_norm(a, w):
        sq = a * a
        lo = jnp.sum(jnp.where(low, sq, 0.0), axis=-1, keepdims=True)
        hi = jnp.sum(jnp.where(low, 0.0, sq), axis=-1, keepdims=True)
        msq = jnp.where(low, lo, hi) * (1.0 / ATT_HD)
        return a * lax.rsqrt(msq + EPS) * w

    def ret_rope(a):
        return _rope(a, cr_ref[...], sr_ref[...], RET_DK // 2) if use_rope else a

    def att_rope(a):
        return _rope(a, ca_ref[...], sa_ref[...], ATT_HD // 4) if use_rope else a

    def store_kv(a, out_ref):
        sw = pltpu.roll(a, ATT_HD, 1)
        zero = jnp.zeros_like(a)
        lo_hi = ((jnp.where(low, a, zero), jnp.where(low, zero, sw)),
                 (jnp.where(low, sw, zero), jnp.where(low, zero, a)))
        for kh in range(ATT_KV_HEADS):
            for j in range(tm // ATT_BLOCK):
                for e in range(2):
                    out_ref[kh, 0, (2 * j + e) * ATT_BLOCK:(2 * j + e + 1) * ATT_BLOCK, :] = \
                        lo_hi[kh][e][j * ATT_BLOCK:(j + 1) * ATT_BLOCK].astype(BF16)

    rk_ref, rv_ref, ak_ref, av_ref = outs[:4]
    a = proj(0, 256)
    for s in range(RET_SLABS):
        rk_ref[0, :, s * LANES:(s + 1) * LANES] = (ret_rope(slab(a, s)) * RET_DK ** -0.5).astype(BF16)
    rv_ref[0] = proj(256, 768).astype(BF16)
    a = proj(768, 1024)
    store_kv(att_rope(qk_norm(slab(a, 0), knw_ref[...])), ak_ref)
    vs = slab(a, 1)
    zq = jnp.zeros((ATT_HD, ATT_BLOCK), F32)
    for j in range(tm // ATT_BLOCK):
        vt = vs[j * ATT_BLOCK:(j + 1) * ATT_BLOCK].T
        for kh in range(ATT_KV_HEADS):
            top = vt[kh * ATT_HD:(kh + 1) * ATT_HD]
            tile = jnp.concatenate([jnp.concatenate([top, zq], axis=1), jnp.concatenate([zq, top], axis=1)], axis=0)
            av_ref[kh, 0, j] = tile.astype(BF16)
    if ctx_only:
        return
    rq_ref, rg_ref, aq_ref, pu_ref, gt_ref = outs[4:]
    a = proj(1024, 1280)
    for s in range(RET_SLABS):
        rq_ref[0, :, s * LANES:(s + 1) * LANES] = ret_rope(slab(a, s)).astype(BF16)
    rg_ref[0] = proj(1280, 1792).astype(BF16)
    a = proj(1792, 2304)
    for s in range(ATT_HEADS // 2):
        qs = att_rope(qk_norm(slab(a, s), qnw_ref[...])) * (ATT_HD ** -0.5 * LOG2E)
        for j in range(tm // ATT_BLOCK):
            aq_ref[0, j, s // 2, :, (s % 2) * ATT_BLOCK:(s % 2 + 1) * ATT_BLOCK] = \
                qs[j * ATT_BLOCK:(j + 1) * ATT_BLOCK].T.astype(BF16)
    pu_ref[0] = proj(2304, 2816).astype(BF16)
    for s in range(6):
        gt_ref[0, :, s * 512:(s + 1) * 512] = proj(2816 + s * 512, 2816 + (s + 1) * 512).astype(BF16)


def _inproj(x3, mod3, mod_row, nw, w, qnw, knw, tables, *, tm, ctx_only):
    bx, l, d = x3.shape
    ncol = w.shape[1]
    use_rope = tables is not None
    grid = (bx, l // tm)
    row = lambda b, i: (b, i, 0)
    in_specs = [pl.BlockSpec((1, tm, d), row),
                pl.BlockSpec((1, 1, mod3.shape[2]), (lambda b, i: (b, 0, 0)) if mod_row is None
                             else (lambda b, i: (mod_row, 0, 0))),
                pl.BlockSpec((1, d), lambda b, i: (0, 0)),
                _resident((d, ncol)),
                pl.BlockSpec((1, LANES), lambda b, i: (0, 0)),
                pl.BlockSpec((1, LANES), lambda b, i: (0, 0))]
    args = [x3, mod3, nw, w, qnw, knw]
    if use_rope:
        in_specs += [pl.BlockSpec((tm, LANES), lambda b, i: (i, 0))] * 4
        args += list(tables)
    kv_spec = pl.BlockSpec((ATT_KV_HEADS, 1, 2 * tm, LANES), lambda b, i: (0, b, i, 0))
    kv_shape = jax.ShapeDtypeStruct((ATT_KV_HEADS, bx, 2 * l, LANES), BF16)
    tok = lambda wd: (pl.BlockSpec((1, tm, wd), row), jax.ShapeDtypeStruct((bx, l, wd), BF16))
    nblk, tblk = l // ATT_BLOCK, tm // ATT_BLOCK
    vt_spec = pl.BlockSpec((ATT_KV_HEADS, 1, tblk, LANES, 2 * ATT_BLOCK), lambda b, i: (0, b, i, 0, 0))
    vt_shape = jax.ShapeDtypeStruct((ATT_KV_HEADS, bx, nblk, LANES, 2 * ATT_BLOCK), BF16)
    qt_spec = pl.BlockSpec((1, tblk, ATT_KV_HEADS, LANES, 2 * ATT_BLOCK), lambda b, i: (b, i, 0, 0, 0))
    qt_shape = jax.ShapeDtypeStruct((bx, nblk, ATT_KV_HEADS, LANES, 2 * ATT_BLOCK), BF16)
    outs = [tok(256), tok(512), (kv_spec, kv_shape), (vt_spec, vt_shape)]
    if not ctx_only:
        outs += [tok(256), tok(512), (qt_spec, qt_shape), tok(512), tok(3072)]
    return pl.pallas_call(
        functools.partial(_inproj_kernel, d=d, ctx_only=ctx_only, use_rope=use_rope),
        grid=grid, in_specs=in_specs, out_specs=[o[0] for o in outs], out_shape=[o[1] for o in outs],
        compiler_params=_cparams(2),
        name="inproj_ctx" if not use_rope else "inproj",
    )(*args)


def _ret_kernel(*refs, nc, n, ctx_out):
    lg_ref = refs[0]
    if ctx_out:
        qc_ref, kc_ref, vc_ref, gc_ref, q_ref, k_ref, v_ref, g_ref, yc_ref, y_ref = refs[1:11]
        scr = refs[11:]
    else:
        kc_ref, vc_ref, q_ref, k_ref, v_ref, g_ref, y_ref = refs[1:8]
        qc_ref = gc_ref = yc_ref = None
        scr = refs[8:]
    dmat_ref, zf_ref, zb_ref, xf_ref, xb_ref, gf_ref, gb_ref, rf_ref, rb_ref, st_ref, uf_ref = scr
    C = RET_CHUNK
    W2 = 2 * RET_DV

    ri = lax.broadcasted_iota(jnp.int32, (C, LANES), 0)
    ci = lax.broadcasted_iota(jnp.int32, (C, LANES), 1)
    low = ci < RET_DK
    r2 = lax.broadcasted_iota(jnp.int32, (C, W2), 0)
    c2 = lax.broadcasted_iota(jnp.int32, (C, W2), 1)
    even_col = c2 < RET_DV
    diag = (r2 < RET_DK) == even_col
    r2f = r2.astype(F32)
    rel = (r2 - (c2 % RET_DV)).astype(F32)
    for s in range(RET_SLABS):
        lgf2 = jnp.where(even_col, lg_ref[0, 2 * s], lg_ref[0, 2 * s + 1])
        lgb2 = jnp.where(even_col, lg_ref[1, 2 * s], lg_ref[1, 2 * s + 1])
        dmat_ref[s] = jnp.where(rel >= 0, jnp.exp(jnp.maximum(rel, 0.0) * lgf2), 0.0) \
            + jnp.where(rel <= 0, jnp.exp(jnp.maximum(-rel, 0.0) * lgb2), 0.0)
        zf_ref[s] = jnp.exp((C - 1 - r2f) * lgf2)
        zb_ref[s] = jnp.exp(r2f * lgb2)
        gf_ref[s] = jnp.exp(jnp.full((C, W2), C, F32) * lgf2)
        gb_ref[s] = jnp.exp(jnp.full((C, W2), C, F32) * lgb2)
        lgf1 = jnp.where(low, lg_ref[0, 2 * s], lg_ref[0, 2 * s + 1])
        lgb1 = jnp.where(low, lg_ref[1, 2 * s], lg_ref[1, 2 * s + 1])
        rif = ri.astype(F32)
        xf_ref[s] = jnp.exp((rif + 1.0) * lgf1)
        xb_ref[s] = jnp.exp((C - rif) * lgb1)
        rf_ref[s] = jnp.zeros((C, W2), F32)
        rb_ref[s] = jnp.zeros((C, W2), F32)

    def bwd_chunk(k_blk, v_blk, t):
        for s in range(RET_SLABS):
            st_ref[t, s, C:2 * C, :] = rb_ref[s].astype(BF16)
            vf = v_blk[:, s * W2:(s + 1) * W2].astype(F32)
            vz = jnp.concatenate([(vf * zf_ref[s]).astype(BF16), (vf * zb_ref[s]).astype(BF16)], axis=1)
            inc = lax.dot_general(k_blk[:, s * LANES:(s + 1) * LANES], vz, (((0,), (0,)), ((), ())),
                                  preferred_element_type=F32)
            uf_ref[t, s] = jnp.where(diag, inc[:, 0:W2], 0.0)
            rb_ref[s] = gb_ref[s] * rb_ref[s] + jnp.where(diag, inc[:, W2:2 * W2], 0.0)

    def fwd_chunk(q_blk, k_blk, v_blk, g_blk, t, store):
        for s in range(RET_SLABS):
            qs = q_blk[:, s * LANES:(s + 1) * LANES]
            ks = k_blk[:, s * LANES:(s + 1) * LANES]
            v2 = v_blk[:, s * W2:(s + 1) * W2]
            zk = jnp.zeros_like(ks)
            k2 = jnp.concatenate([jnp.where(low, ks, zk), jnp.where(low, zk, ks)], axis=0)
            sc = lax.dot_general(qs, k2, (((1,), (1,)), ((), ())), preferred_element_type=F32) * dmat_ref[s]
            zv = jnp.zeros_like(v2)
            vv = jnp.concatenate([jnp.where(even_col, v2, zv), jnp.where(even_col, zv, v2)], axis=0)
            y2 = jnp.dot(sc.astype(BF16), vv, preferred_element_type=F32)
            qf = qs.astype(F32)
            qx = jnp.concatenate([(qf * xf_ref[s]).astype(BF16), (qf * xb_ref[s]).astype(BF16)], axis=1)
            st_ref[t, s, 0:C, :] = rf_ref[s].astype(BF16)
            y2 = y2 + jnp.dot(qx, st_ref[t, s], preferred_element_type=F32)
            if store is not None:
                for e in range(2):
                    h = 2 * s + e
                    y = y2[:, e * RET_DV:(e + 1) * RET_DV]
                    gh = g_blk[:, h * RET_DV:(h + 1) * RET_DV].astype(F32)
                    yn = y * lax.rsqrt(jnp.mean(y * y, axis=-1, keepdims=True) + EPS)
                    store(h, (gh * _sigmoid(gh) * yn).astype(BF16))
            rf_ref[s] = gf_ref[s] * rf_ref[s] + uf_ref[t, s]

    for t in reversed(range(nc)):
        bwd_chunk(kc_ref[0, t * C:(t + 1) * C, :], vc_ref[0, t * C:(t + 1) * C, :], t)

    def bwd_body(i, carry):
        j = n - 1 - i
        r0 = pl.multiple_of(j * C, C)
        bwd_chunk(k_ref[0, pl.ds(r0, C), :], v_ref[0, pl.ds(r0, C), :], nc + j)
        return carry
    lax.fori_loop(0, n, bwd_body, 0)

    for t in range(nc):
        sl = slice(t * C, (t + 1) * C)
        if ctx_out:
            def store_c(h, val, sl=sl):
                yc_ref[0, sl, h * RET_DV:(h + 1) * RET_DV] = val
            fwd_chunk(qc_ref[0, sl, :], kc_ref[0, sl, :], vc_ref[0, sl, :], gc_ref[0, sl, :], t, store_c)
        else:
            for s in range(RET_SLABS):
                rf_ref[s] = gf_ref[s] * rf_ref[s] + uf_ref[t, s]

    def fwd_body(j, carry):
        r0 = pl.multiple_of(j * C, C)
        rows = pl.ds(r0, C)

        def store(h, val):
            y_ref[0, rows, h * RET_DV:(h + 1) * RET_DV] = val
        fwd_chunk(q_ref[0, rows, :], k_ref[0, rows, :], v_ref[0, rows, :], g_ref[0, rows, :], nc + j, store)
        return carry
    lax.fori_loop(0, n, fwd_body, 0, unroll=2)


def _retention(log_gamma, ctx_parts, lat_parts, *, ctx_out):
    b, s, _ = lat_parts[0].shape
    lc = ctx_parts[-1].shape[1]
    nc, n = lc // RET_CHUNK, s // RET_CHUNK
    full = lambda a: pl.BlockSpec((1,) + a.shape[1:], lambda i: (i, 0, 0))
    in_specs = [pl.BlockSpec(memory_space=pltpu.SMEM)] + [full(a) for a in (*ctx_parts, *lat_parts)]
    vw = RET_HEADS * RET_DV
    out_shape = [jax.ShapeDtypeStruct((b, s, vw), BF16)]
    if ctx_out:
        out_shape = [jax.ShapeDtypeStruct((b, lc, vw), BF16)] + out_shape
    out_specs = [full(o) for o in out_shape]
    wide = (RET_SLABS, RET_CHUNK, 2 * RET_DV)
    narrow = (RET_SLABS, RET_CHUNK, LANES)
    scratch = [pltpu.VMEM(wide, F32),
               pltpu.VMEM(wide, F32), pltpu.VMEM(wide, F32),
               pltpu.VMEM(narrow, F32), pltpu.VMEM(narrow, F32),
               pltpu.VMEM(wide, F32), pltpu.VMEM(wide, F32),
               pltpu.VMEM(wide, F32), pltpu.VMEM(wide, F32),
               pltpu.VMEM((nc + n, RET_SLABS, 2 * RET_CHUNK, 2 * RET_DV), BF16),
               pltpu.VMEM((nc + n, RET_SLABS, RET_CHUNK, 2 * RET_DV), F32)]
    res = pl.pallas_call(
        functools.partial(_ret_kernel, nc=nc, n=n, ctx_out=ctx_out),
        grid=(b,), in_specs=in_specs, out_specs=out_specs, out_shape=out_shape,
        scratch_shapes=scratch, compiler_params=_cparams(1),
        name="retention",
    )(log_gamma, *ctx_parts, *lat_parts)
    return res if ctx_out else (None, res[0])


def _pool_kernel(u_ref, w_ref, sc_ref, o_ref, pa_ref, pb_ref, *, l):
    pad = 16
    t = lax.broadcasted_iota(jnp.int32, (l, LANES), 0)
    zeros_edge = jnp.zeros((pad, LANES), F32)
    ext = l + 16
    for g, w in enumerate(POOL_WINDOWS):
        ug = u_ref[0, :, g * POOL_GW:(g + 1) * POOL_GW].astype(F32)
        for buf in (pa_ref, pb_ref):
            buf[0:pad, :] = zeros_edge
            buf[pad + l:pad + l + pad, :] = zeros_edge
        pa_ref[pad:pad + l, :] = ug
        pb_ref[8:8 + ext, :] = pa_ref[7:7 + ext, :] + pa_ref[8:8 + ext, :]
        src, dst = pb_ref, pa_ref
        step = 1
        while 2 * step < w:
            dst[8:8 + ext, :] = src[8 - step:8 - step + ext, :] + src[8 + step:8 + step + ext, :]
            src, dst = dst, src
            step *= 2
        hw = w // 2
        cnt = (jnp.minimum(t + hw, l) - jnp.maximum(t - hw, 0)).astype(F32)
        pooled = src[pad:pad + l, :] / cnt
        mixed = jnp.dot((pooled - ug).astype(BF16), w_ref[g], preferred_element_type=F32)
        o_ref[0, :, g * POOL_GW:(g + 1) * POOL_GW] = (mixed * sc_ref[:, g * POOL_GW:(g + 1) * POOL_GW]).astype(BF16)


def _pool(u, w_grp, scale):
    b, l, pw = u.shape
    return pl.pallas_call(
        functools.partial(_pool_kernel, l=l),
        grid=(b,),
        in_specs=[pl.BlockSpec((1, l, pw), lambda i: (i, 0, 0)),
                  pl.BlockSpec(w_grp.shape, lambda i: (0, 0, 0)),
                  pl.BlockSpec((1, pw), lambda i: (0, 0))],
        out_specs=pl.BlockSpec((1, l, pw), lambda i: (i, 0, 0)),
        out_shape=jax.ShapeDtypeStruct((b, l, pw), BF16),
        scratch_shapes=[pltpu.VMEM((l + 32, LANES), F32), pltpu.VMEM((l + 32, LANES), F32)],
        compiler_params=_cparams(1),
        name="pool",
    )(u, w_grp, scale)


def _attn_kernel(*refs, local, nq, ncb):
    sink_ref = refs[0]
    if local:
        q_ref, k_ref, v_ref, kc_ref, vc_ref, o_ref, pa_ref, pb_ref, sa_ref, sb_ref = refs[1:]
    else:
        q_ref, kc_ref, vc_ref, o_ref, pa_ref, sa_ref = refs[1:]
        k_ref = v_ref = None
    BLK = ATT_BLOCK
    KB = 2 * BLK
    nloc = 3 if local else 0
    kj = lax.broadcasted_iota(jnp.int32, (BLK, 2 * BLK), 0)
    qi = lax.broadcasted_iota(jnp.int32, (BLK, 2 * BLK), 1) % BLK
    tri_prev = jnp.where(kj >= qi, 0.0, NEG_BIG)
    tri_next = jnp.where(kj <= qi, 0.0, NEG_BIG)
    head_row = lax.broadcasted_iota(jnp.int32, (LANES, BLK), 0) < ATT_HD

    def block_rows(nblk):
        if isinstance(nblk, int):
            nb = min(nblk, nq - 1)
            ds = lambda blk, size: pl.ds(blk * size, size)
            neigh = [max(nb - 1, 0), nb, min(nb + 1, nq - 1)]
        else:
            nb = jnp.minimum(nblk, nq - 1)
            ds = lambda blk, size: pl.ds(pl.multiple_of(blk * size, size), size)
            neigh = [jnp.maximum(nb - 1, 0), nb, jnp.minimum(nb + 1, nq - 1)]
        return nb, ds(nb, BLK), neigh[:nloc], [ds(blk, KB) for blk in neigh[:nloc]]

    def scores(nblk, s_ref):
        nb, _, _, windows = block_rows(nblk)
        if local:
            bias_prev = tri_prev + jnp.where(nb == 0, NEG_BIG, 0.0)
            bias_next = tri_next + jnp.where(nb == nq - 1, NEG_BIG, 0.0)
        for kh in range(ATT_KV_HEADS):
            qt = q_ref[0, nb, kh]
            for j in range(nloc + ncb):
                kb = k_ref[kh, 0, windows[j], :] if j < nloc else kc_ref[kh, 0, (j - nloc) * KB:(j - nloc + 1) * KB, :]
                sc = jnp.dot(kb, qt, preferred_element_type=F32)
                for e in range(2):
                    c = sc[e * BLK:(e + 1) * BLK]
                    if local and j == 0:
                        c = c + bias_prev
                    if local and j == 2:
                        c = c + bias_next
                    s_ref[kh, (2 * j + e) * BLK:(2 * j + e + 1) * BLK, :] = c

    def softmax_pv(nblk, s_ref, p_ref):
        nb, rows, neigh, _ = block_rows(nblk)
        for kh in range(ATT_KV_HEADS):
            inv = [[None, None], [None, None]]
            for s in range(2):
                scols = slice(s * BLK, (s + 1) * BLK)
                for e in range(2):
                    krows = [slice((2 * j + e) * BLK, (2 * j + e + 1) * BLK) for j in range(nloc + ncb)]
                    sink = sink_ref[kh * 4 + 2 * s + e] * LOG2E
                    mx = s_ref[kh, krows[0], scols]
                    for kr in krows[1:]:
                        mx = jnp.maximum(mx, s_ref[kh, kr, scols])
                    m = jnp.maximum(jnp.max(mx, axis=0, keepdims=True), sink)
                    tot = None
                    for kr in krows:
                        p = jnp.exp2(s_ref[kh, kr, scols] - m)
                        tot = p if tot is None else tot + p
                        p_ref[kh, kr, scols] = p.astype(BF16)
                    den = jnp.sum(tot, axis=0, keepdims=True) + jnp.exp2(sink - m)
                    inv[s][e] = 1.0 / den
            ot = None
            for j in range(nloc + ncb):
                vt = v_ref[kh, 0, neigh[j]] if j < nloc else vc_ref[kh, 0, j - nloc]
                part = jnp.dot(vt, p_ref[kh, j * KB:(j + 1) * KB, :], preferred_element_type=F32)
                ot = part if ot is None else ot + part
            for s in range(2):
                scale = jnp.where(head_row, inv[s][0], inv[s][1])
                o_ref[0, rows, (2 * kh + s) * LANES:(2 * kh + s + 1) * LANES] = \
                    (ot[:, s * BLK:(s + 1) * BLK] * scale).T.astype(BF16)

    if not local:
        for nblk in range(nq):
            scores(nblk, sa_ref)
            softmax_pv(nblk, sa_ref, pa_ref)
        return

    scores(0, sa_ref)

    def body(i, carry):
        nblk = 2 * i
        scores(nblk + 1, sb_ref)
        softmax_pv(nblk, sa_ref, pa_ref)
        scores(nblk + 2, sa_ref)
        softmax_pv(nblk + 1, sb_ref, pb_ref)
        return carry
    lax.fori_loop(0, nq // 2, body, 0)


def _attention(sink, q, k, v, kc, vc):
    b, nq = q.shape[:2]
    l = nq * ATT_BLOCK
    qw = ATT_HEADS * ATT_HD
    local = k is not None
    ncb = vc.shape[2]
    assert not local or nq % 2 == 0
    tok = pl.BlockSpec((1, l, qw), lambda i: (i, 0, 0))
    lead = lambda a: pl.BlockSpec((1,) + a.shape[1:], lambda i: (i,) + (0,) * (a.ndim - 1))
    second = lambda a: pl.BlockSpec((a.shape[0], 1) + a.shape[2:], lambda i: (0, i) + (0,) * (a.ndim - 2))
    kvs = ([k, v] if local else []) + [kc, vc]
    args = [sink, q] + kvs
    in_specs = [pl.BlockSpec(memory_space=pltpu.SMEM), lead(q)] + [second(a) for a in kvs]
    nkb = (3 if local else 0) + ncb
    tile = (ATT_KV_HEADS, nkb * 2 * ATT_BLOCK, 2 * ATT_BLOCK)
    scratch = [pltpu.VMEM(tile, BF16)] * (2 if local else 1) + [pltpu.VMEM(tile, F32)] * (2 if local else 1)
    return pl.pallas_call(
        functools.partial(_attn_kernel, local=local, nq=nq, ncb=ncb),
        grid=(b,), in_specs=in_specs, out_specs=tok,
        out_shape=jax.ShapeDtypeStruct((b, l, qw), BF16),
        scratch_shapes=scratch,
        compiler_params=_cparams(1),
        name="attn_window" if local else "attn_ctx",
    )(*args)


def _merge_mlp_kernel(x_ref, yr_ref, yp_ref, ya_ref, gt_ref, mod_ref, nw_ref,
                      wr_ref, wp_ref, wa_ref, wo_ref, w1_ref, w2_ref, o_ref, *, d, hid_chunk):
    mod = mod_ref[0]
    g1, sh2, sc2, g2 = mod[:, 2 * d:3 * d], mod[:, 3 * d:4 * d], mod[:, 4 * d:5 * d], mod[:, 5 * d:6 * d]
    y = None
    for i, (y_ref, w_ref) in enumerate(((yr_ref, wr_ref), (yp_ref, wp_ref), (ya_ref, wa_ref))):
        proj = jnp.dot(y_ref[0], w_ref[...], preferred_element_type=F32)
        term = _sigmoid(gt_ref[0, :, i * d:(i + 1) * d].astype(F32)) * proj
        y = term if y is None else y + term
    mix = jnp.dot(y.astype(BF16), wo_ref[...], preferred_element_type=F32)
    x1 = x_ref[0] + g1 * mix
    ms = jnp.mean(x1 * x1, axis=-1, keepdims=True)
    h = ((x1 * lax.rsqrt(ms + EPS) * nw_ref[...]) * (1.0 + sc2) + sh2).astype(BF16)
    acc = None
    hidden = w1_ref.shape[1]
    for c in range(hidden // hid_chunk):
        a = jnp.dot(h, w1_ref[:, c * hid_chunk:(c + 1) * hid_chunk], preferred_element_type=F32)
        a = jnp.maximum(a, 0.0)
        part = jnp.dot((a * a).astype(BF16), w2_ref[c * hid_chunk:(c + 1) * hid_chunk, :], preferred_element_type=F32)
        acc = part if acc is None else acc + part
    o_ref[0] = x1 + g2 * acc


def _merge_mlp(x3, y_ret, y_pool, y_att, gates, mod3, mod_row, nw, weights, *, tm):
    bx, l, d = x3.shape
    row = lambda b, i: (b, i, 0)
    bw = y_ret.shape[2]
    in_specs = [pl.BlockSpec((1, tm, d), row)] + [pl.BlockSpec((1, tm, bw), row)] * 3 + [
        pl.BlockSpec((1, tm, N_BRANCH * d), row),
        pl.BlockSpec((1, 1, mod3.shape[2]), (lambda b, i: (b, 0, 0)) if mod_row is None
                     else (lambda b, i: (mod_row, 0, 0))),
        pl.BlockSpec((1, d), lambda b, i: (0, 0))] + [_resident(w.shape) for w in weights]
    return pl.pallas_call(
        functools.partial(_merge_mlp_kernel, d=d, hid_chunk=1024),
        grid=(bx, l // tm), in_specs=in_specs, out_specs=pl.BlockSpec((1, tm, d), row),
        out_shape=jax.ShapeDtypeStruct((bx, l, d), F32),
        compiler_params=_cparams(2),
        name="merge_mlp",
    )(x3, y_ret, y_pool, y_att, gates, mod3, nw, *weights)


def _rope_tables(s):
    pos = jnp.arange(s, dtype=F32)
    half = RET_DK // 2
    ang = pos[:, None] * (RET_ROPE_BASE ** (-jnp.arange(half, dtype=F32) / half))[None, :]
    cos_r = jnp.tile(jnp.concatenate([jnp.cos(ang), jnp.cos(ang)], axis=1), (1, LANES // RET_DK))
    sin_r = jnp.tile(jnp.concatenate([-jnp.sin(ang), jnp.sin(ang)], axis=1), (1, LANES // RET_DK))
    q = ATT_HD // 4
    freq = ROPE_BASE ** (-jnp.arange(q, dtype=F32) / q)
    ar = (jnp.arange(s) // GRID_W).astype(F32)[:, None] * freq[None, :]
    ac = (jnp.arange(s) % GRID_W).astype(F32)[:, None] * freq[None, :]
    cos_a = jnp.concatenate([jnp.cos(ar), jnp.cos(ar), jnp.cos(ac), jnp.cos(ac)], axis=1)
    sin_a = jnp.concatenate([-jnp.sin(ar), jnp.sin(ar), -jnp.sin(ac), jnp.sin(ac)], axis=1)
    return cos_r, sin_r, jnp.tile(cos_a, (1, LANES // ATT_HD)), jnp.tile(sin_a, (1, LANES // ATT_HD))


def kernel(x, c, ctx, c_ctx, norm1_w, norm2_w, ada_w, ada_b, w_in, ret_decay, pool_w, pool_scale, q_norm_w, k_norm_w, attn_sink, w_ret_out, w_pool_out, w_attn_out, w_out, w_mlp1, w_mlp2):
    b, s, d = x.shape
    lc = ctx.shape[1]
    depth = w_in.shape[0]
    tm = 512 if s % 512 == 0 else 256
    tmc = 512 if (b * lc) % 512 == 0 else 256

    rows = -(-(b + 1) // 8) * 8
    cc = jnp.zeros((rows, d), F32).at[:b].set(c).at[b].set(c_ctx)
    mod = _modulation(cc, ada_w, ada_b)

    tables = _rope_tables(s)
    xc = ctx.reshape(1, b * lc, d)

    for l in range(depth):
        last = l == depth - 1
        mod3 = mod[l].reshape(rows, 1, 6 * d)
        log_gamma = jnp.log1p(-jnp.exp2(-ret_decay[l].astype(F32)))
        w_l = w_in[l].astype(BF16)
        nw1, nw2 = norm1_w[l].reshape(1, d), norm2_w[l].reshape(1, d)
        qnw = jnp.tile(q_norm_w[l], LANES // ATT_HD).reshape(1, LANES)
        knw = jnp.tile(k_norm_w[l], LANES // ATT_HD).reshape(1, LANES)
        pw = pool_w[l].astype(BF16)
        psc = pool_scale[l].reshape(1, -1)
        weights = [w_ret_out[l].astype(BF16), w_pool_out[l].astype(BF16), w_attn_out[l].astype(BF16),
                   w_out[l].astype(BF16), w_mlp1[l].astype(BF16), w_mlp2[l].astype(BF16)]
        per_batch = lambda p: p.reshape(b, lc, p.shape[-1])
        ncb = lc // ATT_BLOCK

        cw = w_l[:, :1024] if last else w_l
        cparts = _inproj(xc, mod3, b, nw1, cw, qnw, knw, None, tm=tmc, ctx_only=last)
        rk_c, rv_c = per_batch(cparts[0]), per_batch(cparts[1])
        ak_c = cparts[2].reshape(ATT_KV_HEADS, b, 2 * lc, LANES)
        av_c = cparts[3].reshape(ATT_KV_HEADS, b, ncb, LANES, 2 * ATT_BLOCK)

        rk, rv, ak, av, rq, rg, aq, pu, gt = _inproj(x, mod3, None, nw1, w_l, qnw, knw, tables,
                                                     tm=tm, ctx_only=False)
        if last:
            _, y_ret = _retention(log_gamma, (rk_c, rv_c), (rq, rk, rv, rg), ctx_out=False)
        else:
            rq_c, rg_c, pu_c, gt_c = [per_batch(cparts[i]) for i in (4, 5, 7, 8)]
            aq_c = cparts[6].reshape(b, ncb, ATT_KV_HEADS, LANES, 2 * ATT_BLOCK)
            y_ret_c, y_ret = _retention(log_gamma, (rq_c, rk_c, rv_c, rg_c), (rq, rk, rv, rg), ctx_out=True)
        y_att = _attention(attn_sink[l], aq, ak, av, ak_c, av_c)
        y_pool = _pool(pu, pw, psc)
        x = _merge_mlp(x, y_ret, y_pool, y_att, gt, mod3, None, nw2, weights, tm=tm)

        if not last:
            y_att_c = _attention(attn_sink[l], aq_c, None, None, ak_c, av_c)
            y_pool_c = _pool(pu_c, pw, psc)
            flat = lambda a: a.reshape(1, b * lc, a.shape[-1])
            xc = _merge_mlp(xc, flat(y_ret_c), flat(y_pool_c), flat(y_att_c), flat(gt_c), mod3, b, nw2,
                            weights, tm=tmc)
    return x
```

```python
import functools

import jax
import jax.numpy as jnp
from jax import lax
from jax.experimental import pallas as pl
from jax.experimental.pallas import tpu as pltpu

F32 = jnp.float32
BF16 = jnp.bfloat16

EPS = 1e-6
LOG2E = 1.4426950408889634
LANES = 128
RET_HEADS, RET_DK, RET_DV, RET_CHUNK = 4, 64, 128, 128
RET_SLABS = RET_HEADS // 2
RET_ROPE_BASE = 10000.0
POOL_GROUPS, POOL_GW = 4, 128
POOL_WINDOWS = (2, 4, 8, 16)
ATT_HEADS, ATT_KV_HEADS, ATT_HD = 8, 2, 64
ATT_WINDOW = ATT_BLOCK = 128
ROPE_BASE = 10000.0
GRID_W = 64
N_BRANCH = 3
CTX_SIDE_COLS = 1024
NEG_BIG = -1e30
VMEM_LIMIT = 56 * 1024 * 1024


def _cparams(n_axes):
    return pltpu.CompilerParams(dimension_semantics=("arbitrary",) * n_axes,
                                vmem_limit_bytes=VMEM_LIMIT)


def _layer_block(stacked, layer, block=None, resident=False):
    block = tuple(stacked.shape[1:]) if block is None else block
    idx = (layer,) + (0,) * len(block)
    return pl.BlockSpec((None,) + block, lambda *_: idx,
                        pipeline_mode=pl.Buffered(1) if resident else None)


def _sigmoid(x):
    return 1.0 / (1.0 + jnp.exp(-x))


def _mod_kernel(cc_ref, w_ref, b_ref, o_ref):
    cc = cc_ref[...]
    s = (cc * _sigmoid(cc)).astype(BF16)
    o_ref[0] = jnp.dot(s, w_ref[0].astype(BF16), preferred_element_type=F32) + b_ref[0]


def _modulation(cc, ada_w, ada_b):
    depth, d, n = ada_w.shape
    r = cc.shape[0]
    tn = 1024
    return pl.pallas_call(
        _mod_kernel,
        grid=(depth, n // tn),
        in_specs=[pl.BlockSpec((r, d), lambda l, j: (0, 0)),
                  pl.BlockSpec((1, d, tn), lambda l, j: (l, 0, j)),
                  pl.BlockSpec((1, 1, tn), lambda l, j: (l, 0, j))],
        out_specs=pl.BlockSpec((1, r, tn), lambda l, j: (l, 0, j)),
        out_shape=jax.ShapeDtypeStruct((depth, r, n), F32),
        compiler_params=_cparams(2),
        name="adaln_mod",
    )(cc, ada_w, ada_b.reshape(depth, 1, n))


def _swap_halves(x, half):
    lane = lax.broadcasted_iota(jnp.int32, x.shape, 1)
    first = (lane % (2 * half)) < half
    return jnp.where(first, pltpu.roll(x, LANES - half, 1), pltpu.roll(x, half, 1))


def _rope(x, cos, sin, half):
    return x * cos + _swap_halves(x, half) * sin


def _inproj_kernel(*refs, d, ctx_only, use_rope):
    x_ref, mod_ref, nw_ref, w_ref, qnw_ref, knw_ref = refs[:6]
    pos = 6
    if use_rope:
        cr_ref, sr_ref, ca_ref, sa_ref = refs[6:10]
        pos = 10
    outs = refs[pos:]
    tm = x_ref.shape[1]

    x = x_ref[0]
    mod = mod_ref[0]
    shift, scale = mod[:, 0:d], mod[:, d:2 * d]
    ms = jnp.mean(x * x, axis=-1, keepdims=True)
    u = ((x * lax.rsqrt(ms + EPS) * nw_ref[...]) * (1.0 + scale) + shift).astype(BF16)

    def proj(a, b):
        return jnp.dot(u, w_ref[:, a:b], preferred_element_type=F32)

    def slab(a, s):
        return a[:, s * LANES:(s + 1) * LANES]

    lane = lax.broadcasted_iota(jnp.int32, (tm, LANES), 1)
    low = lane < ATT_HD

    def qk_norm(a, w):
        sq = a * a
        lo = jnp.sum(jnp.where(low, sq, 0.0), axis=-1, keepdims=True)
        hi = jnp.sum(jnp.where(low, 0.0, sq), axis=-1, keepdims=True)
        msq = jnp.where(low, lo, hi) * (1.0 / ATT_HD)
        return a * lax.rsqrt(msq + EPS) * w

    def ret_rope(a):
        return _rope(a, cr_ref[...], sr_ref[...], RET_DK // 2) if use_rope else a

    def att_rope(a):
        return _rope(a, ca_ref[...], sa_ref[...], ATT_HD // 4) if use_rope else a

    def store_kv(a, out_ref):
        sw = pltpu.roll(a, ATT_HD, 1)
        zero = jnp.zeros_like(a)
        lo_hi = ((jnp.where(low, a, zero), jnp.where(low, zero, sw)),
                 (jnp.where(low, sw, zero), jnp.where(low, zero, a)))
        for kh in range(ATT_KV_HEADS):
            for j in range(tm // ATT_BLOCK):
                for e in range(2):
                    out_ref[kh, 0, (2 * j + e) * ATT_BLOCK:(2 * j + e + 1) * ATT_BLOCK, :] = \
                        lo_hi[kh][e][j * ATT_BLOCK:(j + 1) * ATT_BLOCK].astype(BF16)

    rk_ref, rv_ref, ak_ref, av_ref = outs[:4]
    a = proj(0, 256)
    for s in range(RET_SLABS):
        rk_ref[0, :, s * LANES:(s + 1) * LANES] = (ret_rope(slab(a, s)) * RET_DK ** -0.5).astype(BF16)
    rv_ref[0] = proj(256, 768).astype(BF16)
    a = proj(768, 1024)
    store_kv(att_rope(qk_norm(slab(a, 0), knw_ref[...])), ak_ref)
    vs = slab(a, 1)
    zq = jnp.zeros((ATT_HD, ATT_BLOCK), F32)
    for j in range(tm // ATT_BLOCK):
        vt = vs[j * ATT_BLOCK:(j + 1) * ATT_BLOCK].T
        for kh in range(ATT_KV_HEADS):
            top = vt[kh * ATT_HD:(kh + 1) * ATT_HD]
            tile = jnp.concatenate([jnp.concatenate([top, zq], axis=1), jnp.concatenate([zq, top], axis=1)], axis=0)
            av_ref[kh, 0, j] = tile.astype(BF16)
    if ctx_only:
        return
    rq_ref, rg_ref, aq_ref, pu_ref, gt_ref = outs[4:]
    a = proj(1024, 1280)
    for s in range(RET_SLABS):
        rq_ref[0, :, s * LANES:(s + 1) * LANES] = ret_rope(slab(a, s)).astype(BF16)
    rg_ref[0] = proj(1280, 1792).astype(BF16)
    a = proj(1792, 2304)
    for s in range(ATT_HEADS // 2):
        qs = att_rope(qk_norm(slab(a, s), qnw_ref[...])) * (ATT_HD ** -0.5 * LOG2E)
        for j in range(tm // ATT_BLOCK):
            aq_ref[0, j, s // 2, :, (s % 2) * ATT_BLOCK:(s % 2 + 1) * ATT_BLOCK] = \
                qs[j * ATT_BLOCK:(j + 1) * ATT_BLOCK].T.astype(BF16)
    pu_ref[0] = proj(2304, 2816).astype(BF16)
    for s in range(6):
        gt_ref[0, :, s * 512:(s + 1) * 512] = proj(2816 + s * 512, 2816 + (s + 1) * 512).astype(BF16)


def _inproj(x3, mod3, mod_row, layer, nw, w, qnw, knw, tables, *, tm, ctx_only):
    bx, l, d = x3.shape
    ncol = CTX_SIDE_COLS if ctx_only else w.shape[2]
    use_rope = tables is not None
    grid = (bx, l // tm)
    row = lambda b, i: (b, i, 0)
    in_specs = [pl.BlockSpec((1, tm, d), row),
                pl.BlockSpec((1, 1, mod3.shape[2]), lambda b, i: (mod_row(b), 0, 0)),
                _layer_block(nw, layer),
                _layer_block(w, layer, (d, ncol), resident=True),
                _layer_block(qnw, layer),
                _layer_block(knw, layer)]
    args = [x3, mod3, nw, w, qnw, knw]
    if use_rope:
        in_specs += [pl.BlockSpec((tm, LANES), lambda b, i: (i, 0))] * 4
        args += list(tables)
    kv_spec = pl.BlockSpec((ATT_KV_HEADS, 1, 2 * tm, LANES), lambda b, i: (0, b, i, 0))
    kv_shape = jax.ShapeDtypeStruct((ATT_KV_HEADS, bx, 2 * l, LANES), BF16)
    tok = lambda wd: (pl.BlockSpec((1, tm, wd), row), jax.ShapeDtypeStruct((bx, l, wd), BF16))
    nblk, tblk = l // ATT_BLOCK, tm // ATT_BLOCK
    vt_spec = pl.BlockSpec((ATT_KV_HEADS, 1, tblk, LANES, 2 * ATT_BLOCK), lambda b, i: (0, b, i, 0, 0))
    vt_shape = jax.ShapeDtypeStruct((ATT_KV_HEADS, bx, nblk, LANES, 2 * ATT_BLOCK), BF16)
    qt_spec = pl.BlockSpec((1, tblk, ATT_KV_HEADS, LANES, 2 * ATT_BLOCK), lambda b, i: (b, i, 0, 0, 0))
    qt_shape = jax.ShapeDtypeStruct((bx, nblk, ATT_KV_HEADS, LANES, 2 * ATT_BLOCK), BF16)
    outs = [tok(256), tok(512), (kv_spec, kv_shape), (vt_spec, vt_shape)]
    if not ctx_only:
        outs += [tok(256), tok(512), (qt_spec, qt_shape), tok(512), tok(3072)]
    return pl.pallas_call(
        functools.partial(_inproj_kernel, d=d, ctx_only=ctx_only, use_rope=use_rope),
        grid=grid, in_specs=in_specs, out_specs=[o[0] for o in outs], out_shape=[o[1] for o in outs],
        compiler_params=_cparams(2),
        name="inproj_ctx" if not use_rope else "inproj",
    )(*args)


def _ret_kernel(*refs, nc, n, ctx_out):
    lg_ref = refs[0]
    if ctx_out:
        qc_ref, kc_ref, vc_ref, gc_ref, q_ref, k_ref, v_ref, g_ref, yc_ref, y_ref = refs[1:11]
        scr = refs[11:]
    else:
        kc_ref, vc_ref, q_ref, k_ref, v_ref, g_ref, y_ref = refs[1:8]
        qc_ref = gc_ref = yc_ref = None
        scr = refs[8:]
    dmat_ref, zf_ref, zb_ref, xf_ref, xb_ref, gf_ref, gb_ref, rf_ref, rb_ref, st_ref, uf_ref = scr
    C = RET_CHUNK
    W2 = 2 * RET_DV

    ri = lax.broadcasted_iota(jnp.int32, (C, LANES), 0)
    ci = lax.broadcasted_iota(jnp.int32, (C, LANES), 1)
    low = ci < RET_DK
    r2 = lax.broadcasted_iota(jnp.int32, (C, W2), 0)
    c2 = lax.broadcasted_iota(jnp.int32, (C, W2), 1)
    even_col = c2 < RET_DV
    diag = (r2 < RET_DK) == even_col
    r2f = r2.astype(F32)
    rel = (r2 - (c2 % RET_DV)).astype(F32)
    for s in range(RET_SLABS):
        lgf2 = jnp.where(even_col, lg_ref[0, 2 * s], lg_ref[0, 2 * s + 1])
        lgb2 = jnp.where(even_col, lg_ref[1, 2 * s], lg_ref[1, 2 * s + 1])
        dmat_ref[s] = jnp.where(rel >= 0, jnp.exp(jnp.maximum(rel, 0.0) * lgf2), 0.0) \
            + jnp.where(rel <= 0, jnp.exp(jnp.maximum(-rel, 0.0) * lgb2), 0.0)
        zf_ref[s] = jnp.exp((C - 1 - r2f) * lgf2)
        zb_ref[s] = jnp.exp(r2f * lgb2)
        gf_ref[s] = jnp.exp(jnp.full((C, W2), C, F32) * lgf2)
        gb_ref[s] = jnp.exp(jnp.full((C, W2), C, F32) * lgb2)
        lgf1 = jnp.where(low, lg_ref[0, 2 * s], lg_ref[0, 2 * s + 1])
        lgb1 = jnp.where(low, lg_ref[1, 2 * s], lg_ref[1, 2 * s + 1])
        rif = ri.astype(F32)
        xf_ref[s] = jnp.exp((rif + 1.0) * lgf1)
        xb_ref[s] = jnp.exp((C - rif) * lgb1)
        rf_ref[s] = jnp.zeros((C, W2), F32)
        rb_ref[s] = jnp.zeros((C, W2), F32)

    def bwd_chunk(k_blk, v_blk, t):
        for s in range(RET_SLABS):
            st_ref[t, s, C:2 * C, :] = rb_ref[s].astype(BF16)
            vf = v_blk[:, s * W2:(s + 1) * W2].astype(F32)
            vz = jnp.concatenate([(vf * zf_ref[s]).astype(BF16), (vf * zb_ref[s]).astype(BF16)], axis=1)
            inc = lax.dot_general(k_blk[:, s * LANES:(s + 1) * LANES], vz, (((0,), (0,)), ((), ())),
                                  preferred_element_type=F32)
            uf_ref[t, s] = jnp.where(diag, inc[:, 0:W2], 0.0)
            rb_ref[s] = gb_ref[s] * rb_ref[s] + jnp.where(diag, inc[:, W2:2 * W2], 0.0)

    def fwd_chunk(q_blk, k_blk, v_blk, g_blk, t, store):
        for s in range(RET_SLABS):
            qs = q_blk[:, s * LANES:(s + 1) * LANES]
            ks = k_blk[:, s * LANES:(s + 1) * LANES]
            v2 = v_blk[:, s * W2:(s + 1) * W2]
            zk = jnp.zeros_like(ks)
            k2 = jnp.concatenate([jnp.where(low, ks, zk), jnp.where(low, zk, ks)], axis=0)
            sc = lax.dot_general(qs, k2, (((1,), (1,)), ((), ())), preferred_element_type=F32) * dmat_ref[s]
            zv = jnp.zeros_like(v2)
            vv = jnp.concatenate([jnp.where(even_col, v2, zv), jnp.where(even_col, zv, v2)], axis=0)
            y2 = jnp.dot(sc.astype(BF16), vv, preferred_element_type=F32)
            qf = qs.astype(F32)
            qx = jnp.concatenate([(qf * xf_ref[s]).astype(BF16), (qf * xb_ref[s]).astype(BF16)], axis=1)
            st_ref[t, s, 0:C, :] = rf_ref[s].astype(BF16)
            y2 = y2 + jnp.dot(qx, st_ref[t, s], preferred_element_type=F32)
            if store is not None:
                for e in range(2):
                    h = 2 * s + e
                    y = y2[:, e * RET_DV:(e + 1) * RET_DV]
                    gh = g_blk[:, h * RET_DV:(h + 1) * RET_DV].astype(F32)
                    yn = y * lax.rsqrt(jnp.mean(y * y, axis=-1, keepdims=True) + EPS)
                    store(h, (gh * _sigmoid(gh) * yn).astype(BF16))
            rf_ref[s] = gf_ref[s] * rf_ref[s] + uf_ref[t, s]

    for t in reversed(range(nc)):
        bwd_chunk(kc_ref[0, t * C:(t + 1) * C, :], vc_ref[0, t * C:(t + 1) * C, :], t)

    def bwd_body(i, carry):
        j = n - 1 - i
        r0 = pl.multiple_of(j * C, C)
        bwd_chunk(k_ref[0, pl.ds(r0, C), :], v_ref[0, pl.ds(r0, C), :], nc + j)
        return carry
    lax.fori_loop(0, n, bwd_body, 0)

    for t in range(nc):
        sl = slice(t * C, (t + 1) * C)
        if ctx_out:
            def store_c(h, val, sl=sl):
                yc_ref[0, sl, h * RET_DV:(h + 1) * RET_DV] = val
            fwd_chunk(qc_ref[0, sl, :], kc_ref[0, sl, :], vc_ref[0, sl, :], gc_ref[0, sl, :], t, store_c)
        else:
            for s in range(RET_SLABS):
                rf_ref[s] = gf_ref[s] * rf_ref[s] + uf_ref[t, s]

    def fwd_body(j, carry):
        r0 = pl.multiple_of(j * C, C)
        rows = pl.ds(r0, C)

        def store(h, val):
            y_ref[0, rows, h * RET_DV:(h + 1) * RET_DV] = val
        fwd_chunk(q_ref[0, rows, :], k_ref[0, rows, :], v_ref[0, rows, :], g_ref[0, rows, :], nc + j, store)
        return carry
    lax.fori_loop(0, n, fwd_body, 0, unroll=2)


def _retention(log_gamma, ctx_parts, lat_parts, *, ctx_out):
    b, s, _ = lat_parts[0].shape
    lc = ctx_parts[-1].shape[1]
    nc, n = lc // RET_CHUNK, s // RET_CHUNK
    full = lambda a: pl.BlockSpec((1,) + a.shape[1:], lambda i: (i, 0, 0))
    in_specs = [pl.BlockSpec(memory_space=pltpu.SMEM)] + [full(a) for a in (*ctx_parts, *lat_parts)]
    vw = RET_HEADS * RET_DV
    out_shape = [jax.ShapeDtypeStruct((b, s, vw), BF16)]
    if ctx_out:
        out_shape = [jax.ShapeDtypeStruct((b, lc, vw), BF16)] + out_shape
    out_specs = [full(o) for o in out_shape]
    wide = (RET_SLABS, RET_CHUNK, 2 * RET_DV)
    narrow = (RET_SLABS, RET_CHUNK, LANES)
    scratch = [pltpu.VMEM(wide, F32),
               pltpu.VMEM(wide, F32), pltpu.VMEM(wide, F32),
               pltpu.VMEM(narrow, F32), pltpu.VMEM(narrow, F32),
               pltpu.VMEM(wide, F32), pltpu.VMEM(wide, F32),
               pltpu.VMEM(wide, F32), pltpu.VMEM(wide, F32),
               pltpu.VMEM((nc + n, RET_SLABS, 2 * RET_CHUNK, 2 * RET_DV), BF16),
               pltpu.VMEM((nc + n, RET_SLABS, RET_CHUNK, 2 * RET_DV), F32)]
    res = pl.pallas_call(
        functools.partial(_ret_kernel, nc=nc, n=n, ctx_out=ctx_out),
        grid=(b,), in_specs=in_specs, out_specs=out_specs, out_shape=out_shape,
        scratch_shapes=scratch, compiler_params=_cparams(1),
        name="retention",
    )(log_gamma, *ctx_parts, *lat_parts)
    return res if ctx_out else (None, res[0])


def _pool_kernel(u_ref, w_ref, sc_ref, o_ref, pa_ref, pb_ref, *, l):
    pad = 16
    t = lax.broadcasted_iota(jnp.int32, (l, LANES), 0)
    zeros_edge = jnp.zeros((pad, LANES), F32)
    ext = l + 16
    for g, w in enumerate(POOL_WINDOWS):
        ug = u_ref[0, :, g * POOL_GW:(g + 1) * POOL_GW].astype(F32)
        for buf in (pa_ref, pb_ref):
            buf[0:pad, :] = zeros_edge
            buf[pad + l:pad + l + pad, :] = zeros_edge
        pa_ref[pad:pad + l, :] = ug
        pb_ref[8:8 + ext, :] = pa_ref[7:7 + ext, :] + pa_ref[8:8 + ext, :]
        src, dst = pb_ref, pa_ref
        step = 1
        while 2 * step < w:
            dst[8:8 + ext, :] = src[8 - step:8 - step + ext, :] + src[8 + step:8 + step + ext, :]
            src, dst = dst, src
            step *= 2
        hw = w // 2
        cnt = (jnp.minimum(t + hw, l) - jnp.maximum(t - hw, 0)).astype(F32)
        pooled = src[pad:pad + l, :] / cnt
        mixed = jnp.dot((pooled - ug).astype(BF16), w_ref[g], preferred_element_type=F32)
        o_ref[0, :, g * POOL_GW:(g + 1) * POOL_GW] = (mixed * sc_ref[:, g * POOL_GW:(g + 1) * POOL_GW]).astype(BF16)


def _pool(u, layer, w_grp, scale):
    b, l, pw = u.shape
    return pl.pallas_call(
        functools.partial(_pool_kernel, l=l),
        grid=(b,),
        in_specs=[pl.BlockSpec((1, l, pw), lambda i: (i, 0, 0)),
                  _layer_block(w_grp, layer), _layer_block(scale, layer)],
        out_specs=pl.BlockSpec((1, l, pw), lambda i: (i, 0, 0)),
        out_shape=jax.ShapeDtypeStruct((b, l, pw), BF16),
        scratch_shapes=[pltpu.VMEM((l + 32, LANES), F32), pltpu.VMEM((l + 32, LANES), F32)],
        compiler_params=_cparams(1),
        name="pool",
    )(u, w_grp, scale)


def _attn_kernel(*refs, local, nq, ncb):
    sink_ref = refs[0]
    if local:
        q_ref, k_ref, v_ref, kc_ref, vc_ref, o_ref = refs[1:7]
        slot_a, slot_b = refs[7:10], refs[10:13]
    else:
        q_ref, kc_ref, vc_ref, o_ref = refs[1:5]
        slot_a = refs[5:8]
        k_ref = v_ref = None
    BLK = ATT_BLOCK
    KB = 2 * BLK
    nloc = 3 if local else 0
    kj = lax.broadcasted_iota(jnp.int32, (BLK, 2 * BLK), 0)
    qi = lax.broadcasted_iota(jnp.int32, (BLK, 2 * BLK), 1) % BLK
    tri_prev = jnp.where(kj >= qi, 0.0, NEG_BIG)
    tri_next = jnp.where(kj <= qi, 0.0, NEG_BIG)
    head_row = lax.broadcasted_iota(jnp.int32, (LANES, BLK), 0) < ATT_HD
    nkeys = (nloc + ncb) * KB
    orow = lax.broadcasted_iota(jnp.int32, (16, nkeys), 0)
    ocol = lax.broadcasted_iota(jnp.int32, (16, nkeys), 1)
    ones_rows = jnp.where(orow == (ocol % KB) // BLK, 1.0, 0.0).astype(BF16)

    def block_rows(nblk):
        if isinstance(nblk, int):
            nb = min(nblk, nq - 1)
            ds = lambda blk, size: pl.ds(blk * size, size)
            neigh = [max(nb - 1, 0), nb, min(nb + 1, nq - 1)]
        else:
            nb = jnp.minimum(nblk, nq - 1)
            ds = lambda blk, size: pl.ds(pl.multiple_of(blk * size, size), size)
            neigh = [jnp.maximum(nb - 1, 0), nb, jnp.minimum(nb + 1, nq - 1)]
        return nb, ds(nb, BLK), neigh[:nloc], [ds(blk, KB) for blk in neigh[:nloc]]

    def scores(nblk, slot):
        s_ref, m_ref, _ = slot
        nb, _, _, windows = block_rows(nblk)
        if local:
            bias_prev = tri_prev + jnp.where(nb == 0, NEG_BIG, 0.0)
            bias_next = tri_next + jnp.where(nb == nq - 1, NEG_BIG, 0.0)
        for kh in range(ATT_KV_HEADS):
            qt = q_ref[0, nb, kh]
            kcat = jnp.concatenate([k_ref[kh, 0, windows[j], :] for j in range(nloc)] + [kc_ref[kh, 0]], axis=0)
            sc = jnp.dot(kcat, qt, preferred_element_type=F32)
            mx = [None, None]
            for j in range(nloc + ncb):
                for e in range(2):
                    krow = slice((2 * j + e) * BLK, (2 * j + e + 1) * BLK)
                    c = sc[krow]
                    if local and j == 0:
                        c = c + bias_prev
                    if local and j == 2:
                        c = c + bias_next
                    s_ref[kh, krow, :] = c
                    cm = jnp.max(c.reshape(BLK // 8, 8, 2 * BLK), axis=0)
                    mx[e] = cm if mx[e] is None else jnp.maximum(mx[e], cm)
            for e in range(2):
                m_ref[kh, e] = mx[e]

    def softmax_pv(nblk, slot):
        s_ref, m_ref, p_ref = slot
        nb, rows, neigh, _ = block_rows(nblk)
        for kh in range(ATT_KV_HEADS):
            sink_term = [[None, None], [None, None]]
            for s in range(2):
                scols = slice(s * BLK, (s + 1) * BLK)
                for e in range(2):
                    sink = sink_ref[kh * 4 + 2 * s + e] * LOG2E
                    m = jnp.maximum(jnp.max(m_ref[kh, e, :, scols], axis=0, keepdims=True), sink)
                    for j in range(nloc + ncb):
                        kr = slice((2 * j + e) * BLK, (2 * j + e + 1) * BLK)
                        p_ref[kh, kr, scols] = jnp.exp2(s_ref[kh, kr, scols] - m).astype(BF16)
                    sink_term[s][e] = jnp.exp2(sink - m)
            vcat = jnp.concatenate([v_ref[kh, 0, neigh[j]] for j in range(nloc)]
                                   + [vc_ref[kh, 0, j] for j in range(ncb)], axis=1)
            ot = jnp.dot(jnp.concatenate([vcat, ones_rows], axis=0), p_ref[kh],
                         preferred_element_type=F32)
            for s in range(2):
                scols = slice(s * BLK, (s + 1) * BLK)
                inv = [1.0 / (ot[LANES + e:LANES + e + 1, scols] + sink_term[s][e]) for e in range(2)]
                scale = jnp.where(head_row, inv[0], inv[1])
                o_ref[0, rows, (2 * kh + s) * LANES:(2 * kh + s + 1) * LANES] = \
                    (ot[0:LANES, scols] * scale).T.astype(BF16)

    if not local:
        for nblk in range(nq):
            scores(nblk, slot_a)
            softmax_pv(nblk, slot_a)
        return

    scores(0, slot_a)

    def body(i, carry):
        nblk = 2 * i
        scores(nblk + 1, slot_b)
        softmax_pv(nblk, slot_a)
        scores(nblk + 2, slot_a)
        softmax_pv(nblk + 1, slot_b)
        return carry
    lax.fori_loop(0, nq // 2, body, 0)


def _attention(sink, q, k, v, kc, vc):
    b, nq = q.shape[:2]
    l = nq * ATT_BLOCK
    qw = ATT_HEADS * ATT_HD
    local = k is not None
    ncb = vc.shape[2]
    assert not local or nq % 2 == 0
    tok = pl.BlockSpec((1, l, qw), lambda i: (i, 0, 0))
    lead = lambda a: pl.BlockSpec((1,) + a.shape[1:], lambda i: (i,) + (0,) * (a.ndim - 1))
    second = lambda a: pl.BlockSpec((a.shape[0], 1) + a.shape[2:], lambda i: (0, i) + (0,) * (a.ndim - 2))
    kvs = ([k, v] if local else []) + [kc, vc]
    args = [sink, q] + kvs
    in_specs = [pl.BlockSpec(memory_space=pltpu.SMEM), lead(q)] + [second(a) for a in kvs]
    nkb = (3 if local else 0) + ncb
    tile = (ATT_KV_HEADS, nkb * 2 * ATT_BLOCK, 2 * ATT_BLOCK)
    slot = [pltpu.VMEM(tile, F32),
            pltpu.VMEM((ATT_KV_HEADS, 2, 8, 2 * ATT_BLOCK), F32),
            pltpu.VMEM(tile, BF16)]
    scratch = slot * (2 if local else 1)
    return pl.pallas_call(
        functools.partial(_attn_kernel, local=local, nq=nq, ncb=ncb),
        grid=(b,), in_specs=in_specs, out_specs=tok,
        out_shape=jax.ShapeDtypeStruct((b, l, qw), BF16),
        scratch_shapes=scratch,
        compiler_params=_cparams(1),
        name="attn_window" if local else "attn_ctx",
    )(*args)


def _merge_mlp_kernel(x_ref, yr_ref, yp_ref, ya_ref, gt_ref, mod_ref, nw_ref,
                      wr_ref, wp_ref, wa_ref, wo_ref, w1_ref, w2_ref, o_ref, *, d, hid_chunk):
    mod = mod_ref[0]
    g1, sh2, sc2, g2 = mod[:, 2 * d:3 * d], mod[:, 3 * d:4 * d], mod[:, 4 * d:5 * d], mod[:, 5 * d:6 * d]
    y = None
    for i, (y_ref, w_ref) in enumerate(((yr_ref, wr_ref), (yp_ref, wp_ref), (ya_ref, wa_ref))):
        proj = jnp.dot(y_ref[0], w_ref[...], preferred_element_type=F32)
        term = _sigmoid(gt_ref[0, :, i * d:(i + 1) * d].astype(F32)) * proj
        y = term if y is None else y + term
    mix = jnp.dot(y.astype(BF16), wo_ref[...], preferred_element_type=F32)
    x1 = x_ref[0] + g1 * mix
    ms = jnp.mean(x1 * x1, axis=-1, keepdims=True)
    h = ((x1 * lax.rsqrt(ms + EPS) * nw_ref[...]) * (1.0 + sc2) + sh2).astype(BF16)
    acc = None
    hidden = w1_ref.shape[1]
    for c in range(hidden // hid_chunk):
        a = jnp.dot(h, w1_ref[:, c * hid_chunk:(c + 1) * hid_chunk], preferred_element_type=F32)
        a = jnp.maximum(a, 0.0)
        part = jnp.dot((a * a).astype(BF16), w2_ref[c * hid_chunk:(c + 1) * hid_chunk, :], preferred_element_type=F32)
        acc = part if acc is None else acc + part
    o_ref[0] = x1 + g2 * acc


def _merge_mlp(x3, y_ret, y_pool, y_att, gates, mod3, mod_row, layer, nw, weights, *, tm):
    bx, l, d = x3.shape
    row = lambda b, i: (b, i, 0)
    bw = y_ret.shape[2]
    in_specs = [pl.BlockSpec((1, tm, d), row)] + [pl.BlockSpec((1, tm, bw), row)] * 3 + [
        pl.BlockSpec((1, tm, N_BRANCH * d), row),
        pl.BlockSpec((1, 1, mod3.shape[2]), lambda b, i: (mod_row(b), 0, 0)),
        _layer_block(nw, layer)] + [_layer_block(w, layer, resident=True) for w in weights]
    return pl.pallas_call(
        functools.partial(_merge_mlp_kernel, d=d, hid_chunk=1024),
        grid=(bx, l // tm), in_specs=in_specs, out_specs=pl.BlockSpec((1, tm, d), row),
        out_shape=jax.ShapeDtypeStruct((bx, l, d), F32),
        compiler_params=_cparams(2),
        name="merge_mlp",
    )(x3, y_ret, y_pool, y_att, gates, mod3, nw, *weights)


def _rope_tables(s):
    pos = jnp.arange(s, dtype=F32)
    half = RET_DK // 2
    ang = pos[:, None] * (RET_ROPE_BASE ** (-jnp.arange(half, dtype=F32) / half))[None, :]
    cos_r = jnp.tile(jnp.concatenate([jnp.cos(ang), jnp.cos(ang)], axis=1), (1, LANES // RET_DK))
    sin_r = jnp.tile(jnp.concatenate([-jnp.sin(ang), jnp.sin(ang)], axis=1), (1, LANES // RET_DK))
    q = ATT_HD // 4
    freq = ROPE_BASE ** (-jnp.arange(q, dtype=F32) / q)
    ar = (jnp.arange(s) // GRID_W).astype(F32)[:, None] * freq[None, :]
    ac = (jnp.arange(s) % GRID_W).astype(F32)[:, None] * freq[None, :]
    cos_a = jnp.concatenate([jnp.cos(ar), jnp.cos(ar), jnp.cos(ac), jnp.cos(ac)], axis=1)
    sin_a = jnp.concatenate([-jnp.sin(ar), jnp.sin(ar), -jnp.sin(ac), jnp.sin(ac)], axis=1)
    return cos_r, sin_r, jnp.tile(cos_a, (1, LANES // ATT_HD)), jnp.tile(sin_a, (1, LANES // ATT_HD))


def kernel(x, c, ctx, c_ctx, norm1_w, norm2_w, ada_w, ada_b, w_in, ret_decay, pool_w, pool_scale, q_norm_w, k_norm_w, attn_sink, w_ret_out, w_pool_out, w_attn_out, w_out, w_mlp1, w_mlp2):
    b, s, d = x.shape
    lc = ctx.shape[1]
    depth = w_in.shape[0]
    tm = 512 if s % 512 == 0 else 256
    tmc = 512 if (b * lc) % 512 == 0 else 256

    rows = -(-(b + 1) // 8) * 8
    cc = jnp.zeros((rows, d), F32).at[:b].set(c).at[b].set(c_ctx)
    mod3 = _modulation(cc, ada_w, ada_b).reshape(depth * rows, 1, 6 * d)

    tables = _rope_tables(s)
    xc = ctx.reshape(1, b * lc, d)

    log_gamma = jnp.log1p(-jnp.exp2(-ret_decay.astype(F32)))
    w_in_b = w_in.astype(BF16)
    nw1, nw2 = norm1_w.reshape(depth, 1, d), norm2_w.reshape(depth, 1, d)
    qnw = jnp.tile(q_norm_w, (1, LANES // ATT_HD)).reshape(depth, 1, LANES)
    knw = jnp.tile(k_norm_w, (1, LANES // ATT_HD)).reshape(depth, 1, LANES)
    pw = pool_w.astype(BF16)
    psc = pool_scale.reshape(depth, 1, -1)
    weights = [w.astype(BF16) for w in (w_ret_out, w_pool_out, w_attn_out, w_out, w_mlp1, w_mlp2)]
    per_batch = lambda p: p.reshape(b, lc, p.shape[-1])
    ncb = lc // ATT_BLOCK

    for l in range(depth):
        last = l == depth - 1
        lat_row = lambda bi, l=l: l * rows + bi
        ctx_row = lambda bi, l=l: l * rows + b

        cparts = _inproj(xc, mod3, ctx_row, l, nw1, w_in_b, qnw, knw, None, tm=tmc, ctx_only=last)
        rk_c, rv_c = per_batch(cparts[0]), per_batch(cparts[1])
        ak_c = cparts[2].reshape(ATT_KV_HEADS, b, 2 * lc, LANES)
        av_c = cparts[3].reshape(ATT_KV_HEADS, b, ncb, LANES, 2 * ATT_BLOCK)

        rk, rv, ak, av, rq, rg, aq, pu, gt = _inproj(x, mod3, lat_row, l, nw1, w_in_b, qnw, knw, tables,
                                                     tm=tm, ctx_only=False)
        if last:
            _, y_ret = _retention(log_gamma[l], (rk_c, rv_c), (rq, rk, rv, rg), ctx_out=False)
        else:
            rq_c, rg_c, pu_c, gt_c = [per_batch(cparts[i]) for i in (4, 5, 7, 8)]
            aq_c = cparts[6].reshape(b, ncb, ATT_KV_HEADS, LANES, 2 * ATT_BLOCK)
            y_ret_c, y_ret = _retention(log_gamma[l], (rq_c, rk_c, rv_c, rg_c), (rq, rk, rv, rg), ctx_out=True)
        y_att = _attention(attn_sink[l], aq, ak, av, ak_c, av_c)
        y_pool = _pool(pu, l, pw, psc)
        x = _merge_mlp(x, y_ret, y_pool, y_att, gt, mod3, lat_row, l, nw2, weights, tm=tm)

        if not last:
            y_att_c = _attention(attn_sink[l], aq_c, None, None, ak_c, av_c)
            y_pool_c = _pool(pu_c, l, pw, psc)
            flat = lambda a: a.reshape(1, b * lc, a.shape[-1])
            xc = _merge_mlp(xc, flat(y_ret_c), flat(y_pool_c), flat(y_att_c), flat(gt_c), mod3, ctx_row, l, nw2,
                            weights, tm=tmc)
    return x
```

```python
import functools

import jax
import jax.numpy as jnp
from jax import lax
from jax.experimental import pallas as pl
from jax.experimental.pallas import tpu as pltpu

F32 = jnp.float32
BF16 = jnp.bfloat16

EPS = 1e-6
LOG2E = 1.4426950408889634
LANES = 128
RET_HEADS, RET_DK, RET_DV, RET_CHUNK = 4, 64, 128, 128
RET_SLABS = RET_HEADS // 2
RET_ROPE_BASE = 10000.0
POOL_GROUPS, POOL_GW = 4, 128
POOL_WINDOWS = (2, 4, 8, 16)
ATT_HEADS, ATT_KV_HEADS, ATT_HD = 8, 2, 64
ATT_WINDOW = ATT_BLOCK = 128
ROPE_BASE = 10000.0
GRID_W = 64
N_BRANCH = 3
CTX_SIDE_COLS = 1024
NEG_BIG = -1e30
VMEM_LIMIT = 56 * 1024 * 1024


def _cparams(n_axes):
    return pltpu.CompilerParams(dimension_semantics=("arbitrary",) * n_axes,
                                vmem_limit_bytes=VMEM_LIMIT)


def _layer_block(stacked, layer, block=None, resident=False):
    block = tuple(stacked.shape[1:]) if block is None else block
    idx = (layer,) + (0,) * len(block)
    return pl.BlockSpec((None,) + block, lambda *_: idx,
                        pipeline_mode=pl.Buffered(1) if resident else None)


def _sigmoid(x):
    return 1.0 / (1.0 + jnp.exp(-x))


def _mod_kernel(cc_ref, w_ref, b_ref, o_ref):
    cc = cc_ref[...]
    s = (cc * _sigmoid(cc)).astype(BF16)
    o_ref[0] = jnp.dot(s, w_ref[0].astype(BF16), preferred_element_type=F32) + b_ref[0]


def _modulation(cc, ada_w, ada_b):
    depth, d, n = ada_w.shape
    r = cc.shape[0]
    tn = 1024
    return pl.pallas_call(
        _mod_kernel,
        grid=(depth, n // tn),
        in_specs=[pl.BlockSpec((r, d), lambda l, j: (0, 0)),
                  pl.BlockSpec((1, d, tn), lambda l, j: (l, 0, j)),
                  pl.BlockSpec((1, 1, tn), lambda l, j: (l, 0, j))],
        out_specs=pl.BlockSpec((1, r, tn), lambda l, j: (l, 0, j)),
        out_shape=jax.ShapeDtypeStruct((depth, r, n), F32),
        compiler_params=_cparams(2),
        name="adaln_mod",
    )(cc, ada_w, ada_b.reshape(depth, 1, n))


def _swap_halves(x, half):
    lane = lax.broadcasted_iota(jnp.int32, x.shape, 1)
    first = (lane % (2 * half)) < half
    return jnp.where(first, pltpu.roll(x, LANES - half, 1), pltpu.roll(x, half, 1))


def _rope(x, cos, sin, half):
    return x * cos + _swap_halves(x, half) * sin


def _inproj_kernel(*refs, d, ctx_only, use_rope):
    x_ref, mod_ref, nw_ref, w_ref, qnw_ref, knw_ref = refs[:6]
    pos = 6
    if use_rope:
        cr_ref, sr_ref, ca_ref, sa_ref = refs[6:10]
        pos = 10
    outs = refs[pos:]
    tm = x_ref.shape[1]

    x = x_ref[0]
    mod = mod_ref[0]
    shift, scale = mod[:, 0:d], mod[:, d:2 * d]
    ms = jnp.mean(x * x, axis=-1, keepdims=True)
    u = ((x * lax.rsqrt(ms + EPS) * nw_ref[...]) * (1.0 + scale) + shift).astype(BF16)

    def proj(a, b):
        return jnp.dot(u, w_ref[:, a:b], preferred_element_type=F32)

    def slab(a, s):
        return a[:, s * LANES:(s + 1) * LANES]

    lane = lax.broadcasted_iota(jnp.int32, (tm, LANES), 1)
    low = lane < ATT_HD

    def qk_norm(a, w):
        sq = a * a
        lo = jnp.sum(jnp.where(low, sq, 0.0), axis=-1, keepdims=True)
        hi = jnp.sum(jnp.where(low, 0.0, sq), axis=-1, keepdims=True)
        msq = jnp.where(low, lo, hi) * (1.0 / ATT_HD)
        return a * lax.rsqrt(msq + EPS) * w

    def ret_rope(a):
        return _rope(a, cr_ref[...], sr_ref[...], RET_DK // 2) if use_rope else a

    def att_rope(a):
        return _rope(a, ca_ref[...], sa_ref[...], ATT_HD // 4) if use_rope else a

    def store_kv(a, out_ref):
        sw = pltpu.roll(a, ATT_HD, 1)
        zero = jnp.zeros_like(a)
        lo_hi = ((jnp.where(low, a, zero), jnp.where(low, zero, sw)),
                 (jnp.where(low, sw, zero), jnp.where(low, zero, a)))
        for kh in range(ATT_KV_HEADS):
            for j in range(tm // ATT_BLOCK):
                for e in range(2):
                    out_ref[kh, 0, (2 * j + e) * ATT_BLOCK:(2 * j + e + 1) * ATT_BLOCK, :] = \
                        lo_hi[kh][e][j * ATT_BLOCK:(j + 1) * ATT_BLOCK].astype(BF16)

    rk_ref, rv_ref, ak_ref, av_ref = outs[:4]
    a = proj(0, 256)
    for s in range(RET_SLABS):
        rk_ref[0, :, s * LANES:(s + 1) * LANES] = (ret_rope(slab(a, s)) * RET_DK ** -0.5).astype(BF16)
    rv_ref[0] = proj(256, 768).astype(BF16)
    a = proj(768, 1024)
    store_kv(att_rope(qk_norm(slab(a, 0), knw_ref[...])), ak_ref)
    vs = slab(a, 1)
    zq = jnp.zeros((ATT_HD, ATT_BLOCK), F32)
    for j in range(tm // ATT_BLOCK):
        vt = vs[j * ATT_BLOCK:(j + 1) * ATT_BLOCK].T
        for kh in range(ATT_KV_HEADS):
            top = vt[kh * ATT_HD:(kh + 1) * ATT_HD]
            tile = jnp.concatenate([jnp.concatenate([top, zq], axis=1), jnp.concatenate([zq, top], axis=1)], axis=0)
            av_ref[kh, 0, j] = tile.astype(BF16)
    if ctx_only:
        return
    rq_ref, rg_ref, aq_ref, pu_ref, gt_ref = outs[4:]
    a = proj(1024, 1280)
    for s in range(RET_SLABS):
        rq_ref[0, :, s * LANES:(s + 1) * LANES] = ret_rope(slab(a, s)).astype(BF16)
    rg_ref[0] = proj(1280, 1792).astype(BF16)
    a = proj(1792, 2304)
    for s in range(ATT_HEADS // 2):
        qs = att_rope(qk_norm(slab(a, s), qnw_ref[...])) * (ATT_HD ** -0.5 * LOG2E)
        for j in range(tm // ATT_BLOCK):
            aq_ref[0, j, s // 2, :, (s % 2) * ATT_BLOCK:(s % 2 + 1) * ATT_BLOCK] = \
                qs[j * ATT_BLOCK:(j + 1) * ATT_BLOCK].T.astype(BF16)
    pu_ref[0] = proj(2304, 2816).astype(BF16)
    for s in range(6):
        gt_ref[0, :, s * 512:(s + 1) * 512] = proj(2816 + s * 512, 2816 + (s + 1) * 512).astype(BF16)


def _inproj(x3, mod3, mod_row, layer, nw, w, qnw, knw, tables, *, tm, ctx_only):
    bx, l, d = x3.shape
    ncol = CTX_SIDE_COLS if ctx_only else w.shape[2]
    use_rope = tables is not None
    grid = (bx, l // tm)
    row = lambda b, i: (b, i, 0)
    in_specs = [pl.BlockSpec((1, tm, d), row),
                pl.BlockSpec((1, 1, mod3.shape[2]), lambda b, i: (mod_row(b), 0, 0)),
                _layer_block(nw, layer),
                _layer_block(w, layer, (d, ncol), resident=True),
                _layer_block(qnw, layer),
                _layer_block(knw, layer)]
    args = [x3, mod3, nw, w, qnw, knw]
    if use_rope:
        in_specs += [pl.BlockSpec((tm, LANES), lambda b, i: (i, 0))] * 4
        args += list(tables)
    kv_spec = pl.BlockSpec((ATT_KV_HEADS, 1, 2 * tm, LANES), lambda b, i: (0, b, i, 0))
    kv_shape = jax.ShapeDtypeStruct((ATT_KV_HEADS, bx, 2 * l, LANES), BF16)
    tok = lambda wd: (pl.BlockSpec((1, tm, wd), row), jax.ShapeDtypeStruct((bx, l, wd), BF16))
    nblk, tblk = l // ATT_BLOCK, tm // ATT_BLOCK
    vt_spec = pl.BlockSpec((ATT_KV_HEADS, 1, tblk, LANES, 2 * ATT_BLOCK), lambda b, i: (0, b, i, 0, 0))
    vt_shape = jax.ShapeDtypeStruct((ATT_KV_HEADS, bx, nblk, LANES, 2 * ATT_BLOCK), BF16)
    qt_spec = pl.BlockSpec((1, tblk, ATT_KV_HEADS, LANES, 2 * ATT_BLOCK), lambda b, i: (b, i, 0, 0, 0))
    qt_shape = jax.ShapeDtypeStruct((bx, nblk, ATT_KV_HEADS, LANES, 2 * ATT_BLOCK), BF16)
    outs = [tok(256), tok(512), (kv_spec, kv_shape), (vt_spec, vt_shape)]
    if not ctx_only:
        outs += [tok(256), tok(512), (qt_spec, qt_shape), tok(512), tok(3072)]
    return pl.pallas_call(
        functools.partial(_inproj_kernel, d=d, ctx_only=ctx_only, use_rope=use_rope),
        grid=grid, in_specs=in_specs, out_specs=[o[0] for o in outs], out_shape=[o[1] for o in outs],
        compiler_params=_cparams(2),
        name="inproj_ctx" if not use_rope else "inproj",
    )(*args)


def _ret_kernel(*refs, nc, n, ctx_out):
    lg_ref = refs[0]
    if ctx_out:
        qc_ref, kc_ref, vc_ref, gc_ref, q_ref, k_ref, v_ref, g_ref, yc_ref, y_ref = refs[1:11]
        scr = refs[11:]
    else:
        kc_ref, vc_ref, q_ref, k_ref, v_ref, g_ref, y_ref = refs[1:8]
        qc_ref = gc_ref = yc_ref = None
        scr = refs[8:]
    dmat_ref, zf_ref, zb_ref, xf_ref, xb_ref, gf_ref, gb_ref, rf_ref, rb_ref, st_ref, uf_ref = scr
    C = RET_CHUNK
    W2 = 2 * RET_DV

    ri = lax.broadcasted_iota(jnp.int32, (C, LANES), 0)
    ci = lax.broadcasted_iota(jnp.int32, (C, LANES), 1)
    low = ci < RET_DK
    r2 = lax.broadcasted_iota(jnp.int32, (C, W2), 0)
    c2 = lax.broadcasted_iota(jnp.int32, (C, W2), 1)
    even_col = c2 < RET_DV
    diag = (r2 < RET_DK) == even_col
    r2f = r2.astype(F32)
    rel = (r2 - (c2 % RET_DV)).astype(F32)
    for s in range(RET_SLABS):
        lgf2 = jnp.where(even_col, lg_ref[0, 2 * s], lg_ref[0, 2 * s + 1])
        lgb2 = jnp.where(even_col, lg_ref[1, 2 * s], lg_ref[1, 2 * s + 1])
        dmat_ref[s] = jnp.where(rel >= 0, jnp.exp(jnp.maximum(rel, 0.0) * lgf2), 0.0) \
            + jnp.where(rel <= 0, jnp.exp(jnp.maximum(-rel, 0.0) * lgb2), 0.0)
        zf_ref[s] = jnp.exp((C - 1 - r2f) * lgf2)
        zb_ref[s] = jnp.exp(r2f * lgb2)
        gf_ref[s] = jnp.exp(jnp.full((C, W2), C, F32) * lgf2)
        gb_ref[s] = jnp.exp(jnp.full((C, W2), C, F32) * lgb2)
        lgf1 = jnp.where(low, lg_ref[0, 2 * s], lg_ref[0, 2 * s + 1])
        lgb1 = jnp.where(low, lg_ref[1, 2 * s], lg_ref[1, 2 * s + 1])
        rif = ri.astype(F32)
        xf_ref[s] = jnp.exp((rif + 1.0) * lgf1)
        xb_ref[s] = jnp.exp((C - rif) * lgb1)
        rf_ref[s] = jnp.zeros((C, W2), F32)
        rb_ref[s] = jnp.zeros((C, W2), F32)

    def bwd_chunk(k_blk, v_blk, t):
        for s in range(RET_SLABS):
            st_ref[t, s, C:2 * C, :] = rb_ref[s].astype(BF16)
            vf = v_blk[:, s * W2:(s + 1) * W2].astype(F32)
            vz = jnp.concatenate([(vf * zf_ref[s]).astype(BF16), (vf * zb_ref[s]).astype(BF16)], axis=1)
            inc = lax.dot_general(k_blk[:, s * LANES:(s + 1) * LANES], vz, (((0,), (0,)), ((), ())),
                                  preferred_element_type=F32)
            uf_ref[t, s] = jnp.where(diag, inc[:, 0:W2], 0.0)
            rb_ref[s] = gb_ref[s] * rb_ref[s] + jnp.where(diag, inc[:, W2:2 * W2], 0.0)

    def fwd_chunk(q_blk, k_blk, v_blk, g_blk, t, store):
        for s in range(RET_SLABS):
            qs = q_blk[:, s * LANES:(s + 1) * LANES]
            ks = k_blk[:, s * LANES:(s + 1) * LANES]
            v2 = v_blk[:, s * W2:(s + 1) * W2]
            zk = jnp.zeros_like(ks)
            k2 = jnp.concatenate([jnp.where(low, ks, zk), jnp.where(low, zk, ks)], axis=0)
            sc = lax.dot_general(qs, k2, (((1,), (1,)), ((), ())), preferred_element_type=F32) * dmat_ref[s]
            zv = jnp.zeros_like(v2)
            vv = jnp.concatenate([jnp.where(even_col, v2, zv), jnp.where(even_col, zv, v2)], axis=0)
            y2 = jnp.dot(sc.astype(BF16), vv, preferred_element_type=F32)
            qf = qs.astype(F32)
            qx = jnp.concatenate([(qf * xf_ref[s]).astype(BF16), (qf * xb_ref[s]).astype(BF16)], axis=1)
            st_ref[t, s, 0:C, :] = rf_ref[s].astype(BF16)
            y2 = y2 + jnp.dot(qx, st_ref[t, s], preferred_element_type=F32)
            if store is not None:
                for e in range(2):
                    h = 2 * s + e
                    y = y2[:, e * RET_DV:(e + 1) * RET_DV]
                    gh = g_blk[:, h * RET_DV:(h + 1) * RET_DV].astype(F32)
                    yn = y * lax.rsqrt(jnp.mean(y * y, axis=-1, keepdims=True) + EPS)
                    store(h, (gh * _sigmoid(gh) * yn).astype(BF16))
            rf_ref[s] = gf_ref[s] * rf_ref[s] + uf_ref[t, s]

    chunk = lambda t: slice(t * C, (t + 1) * C)

    for t in reversed(range(nc)):
        bwd_chunk(kc_ref[0, chunk(t), :], vc_ref[0, chunk(t), :], t)
    for j in reversed(range(n)):
        bwd_chunk(k_ref[0, chunk(j), :], v_ref[0, chunk(j), :], nc + j)

    def store_to(out_ref, sl):
        def store(h, val):
            out_ref[0, sl, h * RET_DV:(h + 1) * RET_DV] = val
        return store

    for t in range(nc):
        sl = chunk(t)
        if ctx_out:
            fwd_chunk(qc_ref[0, sl, :], kc_ref[0, sl, :], vc_ref[0, sl, :], gc_ref[0, sl, :], t, store_to(yc_ref, sl))
        else:
            for s in range(RET_SLABS):
                rf_ref[s] = gf_ref[s] * rf_ref[s] + uf_ref[t, s]
    for j in range(n):
        sl = chunk(j)
        fwd_chunk(q_ref[0, sl, :], k_ref[0, sl, :], v_ref[0, sl, :], g_ref[0, sl, :], nc + j, store_to(y_ref, sl))


def _retention(log_gamma, ctx_parts, lat_parts, *, ctx_out):
    b, s, _ = lat_parts[0].shape
    lc = ctx_parts[-1].shape[1]
    nc, n = lc // RET_CHUNK, s // RET_CHUNK
    full = lambda a: pl.BlockSpec((1,) + a.shape[1:], lambda i: (i, 0, 0))
    in_specs = [pl.BlockSpec(memory_space=pltpu.SMEM)] + [full(a) for a in (*ctx_parts, *lat_parts)]
    vw = RET_HEADS * RET_DV
    out_shape = [jax.ShapeDtypeStruct((b, s, vw), BF16)]
    if ctx_out:
        out_shape = [jax.ShapeDtypeStruct((b, lc, vw), BF16)] + out_shape
    out_specs = [full(o) for o in out_shape]
    wide = (RET_SLABS, RET_CHUNK, 2 * RET_DV)
    narrow = (RET_SLABS, RET_CHUNK, LANES)
    scratch = [pltpu.VMEM(wide, F32),
               pltpu.VMEM(wide, F32), pltpu.VMEM(wide, F32),
               pltpu.VMEM(narrow, F32), pltpu.VMEM(narrow, F32),
               pltpu.VMEM(wide, F32), pltpu.VMEM(wide, F32),
               pltpu.VMEM(wide, F32), pltpu.VMEM(wide, F32),
               pltpu.VMEM((nc + n, RET_SLABS, 2 * RET_CHUNK, 2 * RET_DV), BF16),
               pltpu.VMEM((nc + n, RET_SLABS, RET_CHUNK, 2 * RET_DV), F32)]
    res = pl.pallas_call(
        functools.partial(_ret_kernel, nc=nc, n=n, ctx_out=ctx_out),
        grid=(b,), in_specs=in_specs, out_specs=out_specs, out_shape=out_shape,
        scratch_shapes=scratch, compiler_params=_cparams(1),
        name="retention",
    )(log_gamma, *ctx_parts, *lat_parts)
    return res if ctx_out else (None, res[0])


def _pool_kernel(u_ref, w_ref, sc_ref, o_ref, pa_ref, pb_ref, *, l):
    pad = 16
    t = lax.broadcasted_iota(jnp.int32, (l, LANES), 0)
    zeros_edge = jnp.zeros((pad, LANES), F32)
    ext = l + 16
    for g, w in enumerate(POOL_WINDOWS):
        ug = u_ref[0, :, g * POOL_GW:(g + 1) * POOL_GW].astype(F32)
        for buf in (pa_ref, pb_ref):
            buf[0:pad, :] = zeros_edge
            buf[pad + l:pad + l + pad, :] = zeros_edge
        pa_ref[pad:pad + l, :] = ug
        pb_ref[8:8 + ext, :] = pa_ref[7:7 + ext, :] + pa_ref[8:8 + ext, :]
        src, dst = pb_ref, pa_ref
        step = 1
        while 2 * step < w:
            dst[8:8 + ext, :] = src[8 - step:8 - step + ext, :] + src[8 + step:8 + step + ext, :]
            src, dst = dst, src
            step *= 2
        hw = w // 2
        cnt = (jnp.minimum(t + hw, l) - jnp.maximum(t - hw, 0)).astype(F32)
        pooled = src[pad:pad + l, :] / cnt
        mixed = jnp.dot((pooled - ug).astype(BF16), w_ref[g], preferred_element_type=F32)
        o_ref[0, :, g * POOL_GW:(g + 1) * POOL_GW] = (mixed * sc_ref[:, g * POOL_GW:(g + 1) * POOL_GW]).astype(BF16)


def _pool(u, layer, w_grp, scale):
    b, l, pw = u.shape
    return pl.pallas_call(
        functools.partial(_pool_kernel, l=l),
        grid=(b,),
        in_specs=[pl.BlockSpec((1, l, pw), lambda i: (i, 0, 0)),
                  _layer_block(w_grp, layer), _layer_block(scale, layer)],
        out_specs=pl.BlockSpec((1, l, pw), lambda i: (i, 0, 0)),
        out_shape=jax.ShapeDtypeStruct((b, l, pw), BF16),
        scratch_shapes=[pltpu.VMEM((l + 32, LANES), F32), pltpu.VMEM((l + 32, LANES), F32)],
        compiler_params=_cparams(1),
        name="pool",
    )(u, w_grp, scale)


def _attn_kernel(*refs, local, nq, ncb):
    sink_ref = refs[0]
    if local:
        q_ref, k_ref, v_ref, kc_ref, vc_ref, o_ref, pall_ref = refs[1:8]
        slot_a, slot_b = refs[8:10], refs[10:12]
    else:
        q_ref, kc_ref, vc_ref, o_ref, pall_ref = refs[1:6]
        slot_a = refs[6:8]
        k_ref = v_ref = None
    BLK = ATT_BLOCK
    KB = 2 * BLK
    nloc = 3 if local else 0
    kj = lax.broadcasted_iota(jnp.int32, (BLK, 2 * BLK), 0)
    qi = lax.broadcasted_iota(jnp.int32, (BLK, 2 * BLK), 1) % BLK
    tri_prev = jnp.where(kj >= qi, 0.0, NEG_BIG)
    tri_next = jnp.where(kj <= qi, 0.0, NEG_BIG)
    head_row = lax.broadcasted_iota(jnp.int32, (LANES, BLK), 0) < ATT_HD
    nkeys = (nloc + ncb) * KB
    orow = lax.broadcasted_iota(jnp.int32, (16, nkeys), 0)
    ocol = lax.broadcasted_iota(jnp.int32, (16, nkeys), 1)
    ones_rows = jnp.where(orow == (ocol % KB) // BLK, 1.0, 0.0).astype(BF16)

    def block_rows(nblk):
        nb = nblk
        ds = lambda blk, size: pl.ds(blk * size, size)
        neigh = [max(nb - 1, 0), nb, min(nb + 1, nq - 1)]
        return nb, ds(nb, BLK), neigh[:nloc], [ds(blk, KB) for blk in neigh[:nloc]]

    def scores(nblk, slot):
        s_ref, m_ref = slot
        nb, _, _, windows = block_rows(nblk)
        if local:
            bias_prev = tri_prev + (NEG_BIG if nb == 0 else 0.0)
            bias_next = tri_next + (NEG_BIG if nb == nq - 1 else 0.0)
        for kh in range(ATT_KV_HEADS):
            qt = q_ref[0, nb, kh]
            kcat = jnp.concatenate([k_ref[kh, 0, windows[j], :] for j in range(nloc)] + [kc_ref[kh, 0]], axis=0)
            sc = jnp.dot(kcat, qt, preferred_element_type=F32)
            mx = [None, None]
            for j in range(nloc + ncb):
                for e in range(2):
                    krow = slice((2 * j + e) * BLK, (2 * j + e + 1) * BLK)
                    c = sc[krow]
                    if local and j == 0:
                        c = c + bias_prev
                    if local and j == 2:
                        c = c + bias_next
                    s_ref[kh, krow, :] = c
                    cm = jnp.max(c.reshape(BLK // 8, 8, 2 * BLK), axis=0)
                    mx[e] = cm if mx[e] is None else jnp.maximum(mx[e], cm)
            for e in range(2):
                m_ref[kh, e] = mx[e]

    def softmax_pv(nblk, slot):
        s_ref, m_ref = slot
        p_ref = pall_ref.at[nblk]
        nb, rows, neigh, _ = block_rows(nblk)
        for kh in range(ATT_KV_HEADS):
            sink_term = [[None, None], [None, None]]
            for s in range(2):
                scols = slice(s * BLK, (s + 1) * BLK)
                for e in range(2):
                    sink = sink_ref[kh * 4 + 2 * s + e] * LOG2E
                    m = jnp.maximum(jnp.max(m_ref[kh, e, :, scols], axis=0, keepdims=True), sink)
                    for j in range(nloc + ncb):
                        kr = slice((2 * j + e) * BLK, (2 * j + e + 1) * BLK)
                        p_ref[kh, kr, scols] = jnp.exp2(s_ref[kh, kr, scols] - m).astype(BF16)
                    sink_term[s][e] = jnp.exp2(sink - m)
            vcat = jnp.concatenate([v_ref[kh, 0, neigh[j]] for j in range(nloc)]
                                   + [vc_ref[kh, 0, j] for j in range(ncb)], axis=1)
            ot = jnp.dot(jnp.concatenate([vcat, ones_rows], axis=0), p_ref[kh],
                         preferred_element_type=F32)
            for s in range(2):
                scols = slice(s * BLK, (s + 1) * BLK)
                inv = [1.0 / (ot[LANES + e:LANES + e + 1, scols] + sink_term[s][e]) for e in range(2)]
                scale = jnp.where(head_row, inv[0], inv[1])
                o_ref[0, rows, (2 * kh + s) * LANES:(2 * kh + s + 1) * LANES] = \
                    (ot[0:LANES, scols] * scale).T.astype(BF16)

    if not local:
        for nblk in range(nq):
            scores(nblk, slot_a)
            softmax_pv(nblk, slot_a)
        return

    scores(0, slot_a)
    for nblk in range(nq):
        cur, nxt = (slot_a, slot_b) if nblk % 2 == 0 else (slot_b, slot_a)
        if nblk + 1 < nq:
            scores(nblk + 1, nxt)
        softmax_pv(nblk, cur)


def _attention(sink, q, k, v, kc, vc):
    b, nq = q.shape[:2]
    l = nq * ATT_BLOCK
    qw = ATT_HEADS * ATT_HD
    local = k is not None
    ncb = vc.shape[2]
    tok = pl.BlockSpec((1, l, qw), lambda i: (i, 0, 0))
    lead = lambda a: pl.BlockSpec((1,) + a.shape[1:], lambda i: (i,) + (0,) * (a.ndim - 1))
    second = lambda a: pl.BlockSpec((a.shape[0], 1) + a.shape[2:], lambda i: (0, i) + (0,) * (a.ndim - 2))
    kvs = ([k, v] if local else []) + [kc, vc]
    args = [sink, q] + kvs
    in_specs = [pl.BlockSpec(memory_space=pltpu.SMEM), lead(q)] + [second(a) for a in kvs]
    nkb = (3 if local else 0) + ncb
    tile = (ATT_KV_HEADS, nkb * 2 * ATT_BLOCK, 2 * ATT_BLOCK)
    slot = [pltpu.VMEM(tile, F32), pltpu.VMEM((ATT_KV_HEADS, 2, 8, 2 * ATT_BLOCK), F32)]
    scratch = [pltpu.VMEM((nq,) + tile, BF16)] + slot * (2 if local else 1)
    return pl.pallas_call(
        functools.partial(_attn_kernel, local=local, nq=nq, ncb=ncb),
        grid=(b,), in_specs=in_specs, out_specs=tok,
        out_shape=jax.ShapeDtypeStruct((b, l, qw), BF16),
        scratch_shapes=scratch,
        compiler_params=_cparams(1),
        name="attn_window" if local else "attn_ctx",
    )(*args)


def _merge_mlp_kernel(x_ref, yr_ref, yp_ref, ya_ref, gt_ref, mod_ref, nw_ref,
                      wr_ref, wp_ref, wa_ref, wo_ref, w1_ref, w2_ref, o_ref, *, d, hid_chunk):
    mod = mod_ref[0]
    g1, sh2, sc2, g2 = mod[:, 2 * d:3 * d], mod[:, 3 * d:4 * d], mod[:, 4 * d:5 * d], mod[:, 5 * d:6 * d]
    y = None
    for i, (y_ref, w_ref) in enumerate(((yr_ref, wr_ref), (yp_ref, wp_ref), (ya_ref, wa_ref))):
        proj = jnp.dot(y_ref[0], w_ref[...], preferred_element_type=F32)
        term = _sigmoid(gt_ref[0, :, i * d:(i + 1) * d].astype(F32)) * proj
        y = term if y is None else y + term
    mix = jnp.dot(y.astype(BF16), wo_ref[...], preferred_element_type=F32)
    x1 = x_ref[0] + g1 * mix
    ms = jnp.mean(x1 * x1, axis=-1, keepdims=True)
    h = ((x1 * lax.rsqrt(ms + EPS) * nw_ref[...]) * (1.0 + sc2) + sh2).astype(BF16)
    acc = None
    hidden = w1_ref.shape[1]
    for c in range(hidden // hid_chunk):
        a = jnp.dot(h, w1_ref[:, c * hid_chunk:(c + 1) * hid_chunk], preferred_element_type=F32)
        a = jnp.maximum(a, 0.0)
        part = jnp.dot((a * a).astype(BF16), w2_ref[c * hid_chunk:(c + 1) * hid_chunk, :], preferred_element_type=F32)
        acc = part if acc is None else acc + part
    o_ref[0] = x1 + g2 * acc


def _merge_mlp(x3, y_ret, y_pool, y_att, gates, mod3, mod_row, layer, nw, weights, *, tm):
    bx, l, d = x3.shape
    row = lambda b, i: (b, i, 0)
    bw = y_ret.shape[2]
    in_specs = [pl.BlockSpec((1, tm, d), row)] + [pl.BlockSpec((1, tm, bw), row)] * 3 + [
        pl.BlockSpec((1, tm, N_BRANCH * d), row),
        pl.BlockSpec((1, 1, mod3.shape[2]), lambda b, i: (mod_row(b), 0, 0)),
        _layer_block(nw, layer)] + [_layer_block(w, layer, resident=True) for w in weights]
    return pl.pallas_call(
        functools.partial(_merge_mlp_kernel, d=d, hid_chunk=1024),
        grid=(bx, l // tm), in_specs=in_specs, out_specs=pl.BlockSpec((1, tm, d), row),
        out_shape=jax.ShapeDtypeStruct((bx, l, d), F32),
        compiler_params=_cparams(2),
        name="merge_mlp",
    )(x3, y_ret, y_pool, y_att, gates, mod3, nw, *weights)


def _rope_tables(s):
    pos = jnp.arange(s, dtype=F32)
    half = RET_DK // 2
    ang = pos[:, None] * (RET_ROPE_BASE ** (-jnp.arange(half, dtype=F32) / half))[None, :]
    cos_r = jnp.tile(jnp.concatenate([jnp.cos(ang), jnp.cos(ang)], axis=1), (1, LANES // RET_DK))
    sin_r = jnp.tile(jnp.concatenate([-jnp.sin(ang), jnp.sin(ang)], axis=1), (1, LANES // RET_DK))
    q = ATT_HD // 4
    freq = ROPE_BASE ** (-jnp.arange(q, dtype=F32) / q)
    ar = (jnp.arange(s) // GRID_W).astype(F32)[:, None] * freq[None, :]
    ac = (jnp.arange(s) % GRID_W).astype(F32)[:, None] * freq[None, :]
    cos_a = jnp.concatenate([jnp.cos(ar), jnp.cos(ar), jnp.cos(ac), jnp.cos(ac)], axis=1)
    sin_a = jnp.concatenate([-jnp.sin(ar), jnp.sin(ar), -jnp.sin(ac), jnp.sin(ac)], axis=1)
    return cos_r, sin_r, jnp.tile(cos_a, (1, LANES // ATT_HD)), jnp.tile(sin_a, (1, LANES // ATT_HD))


def kernel(x, c, ctx, c_ctx, norm1_w, norm2_w, ada_w, ada_b, w_in, ret_decay, pool_w, pool_scale, q_norm_w, k_norm_w, attn_sink, w_ret_out, w_pool_out, w_attn_out, w_out, w_mlp1, w_mlp2):
    b, s, d = x.shape
    lc = ctx.shape[1]
    depth = w_in.shape[0]
    tm = 512 if s % 512 == 0 else 256
    tmc = 512 if (b * lc) % 512 == 0 else 256

    rows = -(-(b + 1) // 8) * 8
    cc = jnp.zeros((rows, d), F32).at[:b].set(c).at[b].set(c_ctx)
    mod3 = _modulation(cc, ada_w, ada_b).reshape(depth * rows, 1, 6 * d)

    tables = _rope_tables(s)
    xc = ctx.reshape(1, b * lc, d)

    log_gamma = jnp.log1p(-jnp.exp2(-ret_decay.astype(F32)))
    w_in_b = w_in.astype(BF16)
    nw1, nw2 = norm1_w.reshape(depth, 1, d), norm2_w.reshape(depth, 1, d)
    qnw = jnp.tile(q_norm_w, (1, LANES // ATT_HD)).reshape(depth, 1, LANES)
    knw = jnp.tile(k_norm_w, (1, LANES // ATT_HD)).reshape(depth, 1, LANES)
    pw = pool_w.astype(BF16)
    psc = pool_scale.reshape(depth, 1, -1)
    weights = [w.astype(BF16) for w in (w_ret_out, w_pool_out, w_attn_out, w_out, w_mlp1, w_mlp2)]
    per_batch = lambda p: p.reshape(b, lc, p.shape[-1])
    ncb = lc // ATT_BLOCK

    for l in range(depth):
        last = l == depth - 1
        lat_row = lambda bi, l=l: l * rows + bi
        ctx_row = lambda bi, l=l: l * rows + b

        cparts = _inproj(xc, mod3, ctx_row, l, nw1, w_in_b, qnw, knw, None, tm=tmc, ctx_only=last)
        rk_c, rv_c = per_batch(cparts[0]), per_batch(cparts[1])
        ak_c = cparts[2].reshape(ATT_KV_HEADS, b, 2 * lc, LANES)
        av_c = cparts[3].reshape(ATT_KV_HEADS, b, ncb, LANES, 2 * ATT_BLOCK)

        rk, rv, ak, av, rq, rg, aq, pu, gt = _inproj(x, mod3, lat_row, l, nw1, w_in_b, qnw, knw, tables,
                                                     tm=tm, ctx_only=False)
        if last:
            _, y_ret = _retention(log_gamma[l], (rk_c, rv_c), (rq, rk, rv, rg), ctx_out=False)
        else:
            rq_c, rg_c, pu_c, gt_c = [per_batch(cparts[i]) for i in (4, 5, 7, 8)]
            aq_c = cparts[6].reshape(b, ncb, ATT_KV_HEADS, LANES, 2 * ATT_BLOCK)
            y_ret_c, y_ret = _retention(log_gamma[l], (rq_c, rk_c, rv_c, rg_c), (rq, rk, rv, rg), ctx_out=True)
        y_att = _attention(attn_sink[l], aq, ak, av, ak_c, av_c)
        y_pool = _pool(pu, l, pw, psc)
        x = _merge_mlp(x, y_ret, y_pool, y_att, gt, mod3, lat_row, l, nw2, weights, tm=tm)

        if not last:
            y_att_c = _attention(attn_sink[l], aq_c, None, None, ak_c, av_c)
            y_pool_c = _pool(pu_c, l, pw, psc)
            flat = lambda a: a.reshape(1, b * lc, a.shape[-1])
            xc = _merge_mlp(xc, flat(y_ret_c), flat(y_pool_c), flat(y_att_c), flat(gt_c), mod3, ctx_row, l, nw2,
                            weights, tm=tmc)
    return x
```

```python
import functools

import jax
import jax.numpy as jnp
from jax import lax
from jax.experimental import pallas as pl
from jax.experimental.pallas import tpu as pltpu

F32 = jnp.float32
BF16 = jnp.bfloat16

EPS = 1e-6
LOG2E = 1.4426950408889634
LANES = 128
RET_HEADS, RET_DK, RET_DV, RET_CHUNK = 4, 64, 128, 128
RET_SLABS = RET_HEADS // 2
RET_ROPE_BASE = 10000.0
POOL_GROUPS, POOL_GW = 4, 128
POOL_WINDOWS = (2, 4, 8, 16)
ATT_HEADS, ATT_KV_HEADS, ATT_HD = 8, 2, 64
ATT_WINDOW = ATT_BLOCK = 128
ROPE_BASE = 10000.0
GRID_W = 64
N_BRANCH = 3
CTX_SIDE_COLS = 1024
INPROJ_SUBTILE = 512
NEG_BIG = -1e30
VMEM_LIMIT = 62 * 1024 * 1024


def _cparams(n_axes):
    return pltpu.CompilerParams(dimension_semantics=("arbitrary",) * n_axes,
                                vmem_limit_bytes=VMEM_LIMIT)


def _layer_block(stacked, layer, block=None, resident=False):
    if layer is None:
        block = tuple(stacked.shape) if block is None else block
        shape, idx = block, (0,) * len(block)
    else:
        block = tuple(stacked.shape[1:]) if block is None else block
        shape, idx = (None,) + block, (layer,) + (0,) * len(block)
    return pl.BlockSpec(shape, lambda *_: idx, pipeline_mode=pl.Buffered(1) if resident else None)


def _sigmoid(x):
    return 1.0 / (1.0 + jnp.exp(-x))


def _mod_kernel(cc_ref, w_ref, b_ref, o_ref):
    cc = cc_ref[...]
    s = (cc * _sigmoid(cc)).astype(BF16)
    o_ref[0] = jnp.dot(s, w_ref[0].astype(BF16), preferred_element_type=F32) + b_ref[0]


def _modulation(cc, ada_w, ada_b):
    depth, d, n = ada_w.shape
    r = cc.shape[0]
    tn = 1024
    return pl.pallas_call(
        _mod_kernel,
        grid=(depth, n // tn),
        in_specs=[pl.BlockSpec((r, d), lambda l, j: (0, 0)),
                  pl.BlockSpec((1, d, tn), lambda l, j: (l, 0, j)),
                  pl.BlockSpec((1, 1, tn), lambda l, j: (l, 0, j))],
        out_specs=pl.BlockSpec((1, r, tn), lambda l, j: (l, 0, j)),
        out_shape=jax.ShapeDtypeStruct((depth, r, n), F32),
        compiler_params=_cparams(2),
        name="adaln_mod",
    )(cc, ada_w, ada_b.reshape(depth, 1, n))


def _swap_halves(x, half):
    lane = lax.broadcasted_iota(jnp.int32, x.shape, 1)
    first = (lane % (2 * half)) < half
    return jnp.where(first, pltpu.roll(x, LANES - half, 1), pltpu.roll(x, half, 1))


def _rope(x, cos, sin, half):
    return x * cos + _swap_halves(x, half) * sin


def _inproj_kernel(*refs, d, ctx_only, use_rope, n_cast):
    x_ref, mod_ref, nw_ref, w_ref, qnw_ref, knw_ref = refs[:6]
    pos = 6
    if use_rope:
        cr_ref, sr_ref, ca_ref, sa_ref = refs[6:10]
        pos = 10
    cast_in, refs_out = refs[pos:pos + n_cast], refs[pos + n_cast:]
    outs, cast_out = refs_out[:len(refs_out) - n_cast], refs_out[len(refs_out) - n_cast:]
    tm = x_ref.shape[1]

    for src, dst in zip(cast_in, cast_out):
        dst[...] = src[...].astype(BF16)

    mod = mod_ref[0]
    shift, scale = mod[:, 0:d], mod[:, d:2 * d]
    sub = min(tm, INPROJ_SUBTILE)
    lane = lax.broadcasted_iota(jnp.int32, (sub, LANES), 1)
    low = lane < ATT_HD

    def slab(a, s):
        return a[:, s * LANES:(s + 1) * LANES]

    def qk_norm(a, w):
        sq = a * a
        lo = jnp.sum(jnp.where(low, sq, 0.0), axis=-1, keepdims=True)
        hi = jnp.sum(jnp.where(low, 0.0, sq), axis=-1, keepdims=True)
        msq = jnp.where(low, lo, hi) * (1.0 / ATT_HD)
        return a * lax.rsqrt(msq + EPS) * w

    def subtile(r0):
        rows = slice(r0, r0 + sub)
        blk0 = r0 // ATT_BLOCK
        x = x_ref[0, rows]
        ms = jnp.mean(x * x, axis=-1, keepdims=True)
        u = ((x * lax.rsqrt(ms + EPS) * nw_ref[...]) * (1.0 + scale) + shift).astype(BF16)

        def proj(a, b):
            return jnp.dot(u, w_ref[:, a:b], preferred_element_type=F32)

        def ret_rope(a):
            return _rope(a, cr_ref[rows], sr_ref[rows], RET_DK // 2) if use_rope else a

        def att_rope(a):
            return _rope(a, ca_ref[rows], sa_ref[rows], ATT_HD // 4) if use_rope else a

        def store_kv(a, out_ref):
            sw = pltpu.roll(a, ATT_HD, 1)
            zero = jnp.zeros_like(a)
            lo_hi = ((jnp.where(low, a, zero), jnp.where(low, zero, sw)),
                     (jnp.where(low, sw, zero), jnp.where(low, zero, a)))
            for kh in range(ATT_KV_HEADS):
                for j in range(sub // ATT_BLOCK):
                    for e in range(2):
                        k0 = (2 * (blk0 + j) + e) * ATT_BLOCK
                        out_ref[kh, 0, k0:k0 + ATT_BLOCK, :] = \
                            lo_hi[kh][e][j * ATT_BLOCK:(j + 1) * ATT_BLOCK].astype(BF16)

        rk_ref, rv_ref, ak_ref, av_ref = outs[:4]
        a = proj(0, 256)
        for s in range(RET_SLABS):
            rk_ref[0, rows, s * LANES:(s + 1) * LANES] = (ret_rope(slab(a, s)) * RET_DK ** -0.5).astype(BF16)
        rv_ref[0, rows] = proj(256, 768).astype(BF16)
        a = proj(768, 1024)
        store_kv(att_rope(qk_norm(slab(a, 0), knw_ref[...])), ak_ref)
        vs = slab(a, 1)
        zq = jnp.zeros((ATT_HD, ATT_BLOCK), F32)
        for j in range(sub // ATT_BLOCK):
            vt = vs[j * ATT_BLOCK:(j + 1) * ATT_BLOCK].T
            for kh in range(ATT_KV_HEADS):
                top = vt[kh * ATT_HD:(kh + 1) * ATT_HD]
                tile = jnp.concatenate([jnp.concatenate([top, zq], axis=1), jnp.concatenate([zq, top], axis=1)],
                                       axis=0)
                av_ref[kh, 0, blk0 + j] = tile.astype(BF16)
        if ctx_only:
            return
        rq_ref, rg_ref, aq_ref, pu_ref, gt_ref = outs[4:]
        a = proj(1024, 1280)
        for s in range(RET_SLABS):
            rq_ref[0, rows, s * LANES:(s + 1) * LANES] = ret_rope(slab(a, s)).astype(BF16)
        g = proj(1280, 1792)
        rg_ref[0, rows] = (g * _sigmoid(g)).astype(BF16)
        a = proj(1792, 2304)
        for s in range(ATT_HEADS // 2):
            qs = att_rope(qk_norm(slab(a, s), qnw_ref[...])) * (ATT_HD ** -0.5 * LOG2E)
            for j in range(sub // ATT_BLOCK):
                aq_ref[0, blk0 + j, s // 2, :, (s % 2) * ATT_BLOCK:(s % 2 + 1) * ATT_BLOCK] = \
                    qs[j * ATT_BLOCK:(j + 1) * ATT_BLOCK].T.astype(BF16)
        pu_ref[0, rows] = proj(2304, 2816).astype(BF16)
        for s in range(6):
            gt_ref[0, rows, s * 512:(s + 1) * 512] = proj(2816 + s * 512, 2816 + (s + 1) * 512).astype(BF16)

    for r0 in range(0, tm, sub):
        subtile(r0)


def _inproj(x3, mod3, mod_row, layer, nw, w, qnw, knw, tables, *, tm, ctx_only, cast=()):
    bx, l, d = x3.shape
    ncol = CTX_SIDE_COLS if ctx_only else w.shape[2]
    use_rope = tables is not None
    grid = (bx, l // tm)
    row = lambda b, i: (b, i, 0)
    in_specs = [pl.BlockSpec((1, tm, d), row),
                pl.BlockSpec((1, 1, mod3.shape[2]), lambda b, i: (mod_row(b), 0, 0)),
                _layer_block(nw, layer),
                _layer_block(w, layer, (d, ncol), resident=True),
                _layer_block(qnw, layer),
                _layer_block(knw, layer)]
    args = [x3, mod3, nw, w, qnw, knw]
    if use_rope:
        in_specs += [pl.BlockSpec((tm, LANES), lambda b, i: (i, 0))] * 4
        args += list(tables)
    kv_spec = pl.BlockSpec((ATT_KV_HEADS, 1, 2 * tm, LANES), lambda b, i: (0, b, i, 0))
    kv_shape = jax.ShapeDtypeStruct((ATT_KV_HEADS, bx, 2 * l, LANES), BF16)
    tok = lambda wd: (pl.BlockSpec((1, tm, wd), row), jax.ShapeDtypeStruct((bx, l, wd), BF16))
    nblk, tblk = l // ATT_BLOCK, tm // ATT_BLOCK
    vt_spec = pl.BlockSpec((ATT_KV_HEADS, 1, tblk, LANES, 2 * ATT_BLOCK), lambda b, i: (0, b, i, 0, 0))
    vt_shape = jax.ShapeDtypeStruct((ATT_KV_HEADS, bx, nblk, LANES, 2 * ATT_BLOCK), BF16)
    qt_spec = pl.BlockSpec((1, tblk, ATT_KV_HEADS, LANES, 2 * ATT_BLOCK), lambda b, i: (b, i, 0, 0, 0))
    qt_shape = jax.ShapeDtypeStruct((bx, nblk, ATT_KV_HEADS, LANES, 2 * ATT_BLOCK), BF16)
    outs = [tok(256), tok(512), (kv_spec, kv_shape), (vt_spec, vt_shape)]
    if not ctx_only:
        outs += [tok(256), tok(512), (qt_spec, qt_shape), tok(512), tok(3072)]
    steps = bx * (l // tm)
    for wf in cast:
        k, n = wf.shape[1:]
        rps = max(16, k // steps)
        per = steps // (k // rps)
        in_specs.append(pl.BlockSpec((None, rps, n), lambda b, i, per=per: (layer, (b * (l // tm) + i) // per, 0)))
        outs.append((pl.BlockSpec((rps, n), lambda b, i, per=per: ((b * (l // tm) + i) // per, 0)),
                     jax.ShapeDtypeStruct((k, n), BF16)))
        args.append(wf)
    return pl.pallas_call(
        functools.partial(_inproj_kernel, d=d, ctx_only=ctx_only, use_rope=use_rope, n_cast=len(cast)),
        grid=grid, in_specs=in_specs, out_specs=[o[0] for o in outs], out_shape=[o[1] for o in outs],
        compiler_params=_cparams(2),
        name="inproj_ctx" if not use_rope else "inproj",
    )(*args)


def _ret_kernel(*refs, nc, n, ctx_out):
    lg_ref = refs[0]
    if ctx_out:
        qc_ref, kc_ref, vc_ref, gc_ref, q_ref, k_ref, v_ref, g_ref = refs[1:9]
        puc_ref, pu_ref, pw_ref, psc_ref, yc_ref, y_ref, ypc_ref, yp_ref = refs[9:17]
        scr = refs[17:]
    else:
        kc_ref, vc_ref, q_ref, k_ref, v_ref, g_ref = refs[1:7]
        pu_ref, pw_ref, psc_ref, y_ref, yp_ref = refs[7:12]
        qc_ref = gc_ref = yc_ref = puc_ref = ypc_ref = None
        scr = refs[12:]
    dmat_ref, zf_ref, zb_ref, xf_ref, xb_ref, gf_ref, gb_ref, rf_ref, rb_ref, st_ref, uf_ref, pa_ref, pb_ref = scr

    if ctx_out:
        _pool_tokens(puc_ref, pw_ref, psc_ref, ypc_ref, pa_ref, pb_ref)
    _pool_tokens(pu_ref, pw_ref, psc_ref, yp_ref, pa_ref, pb_ref)
    C = RET_CHUNK
    W2 = 2 * RET_DV

    ri = lax.broadcasted_iota(jnp.int32, (C, LANES), 0)
    ci = lax.broadcasted_iota(jnp.int32, (C, LANES), 1)
    low = ci < RET_DK
    r2 = lax.broadcasted_iota(jnp.int32, (C, W2), 0)
    c2 = lax.broadcasted_iota(jnp.int32, (C, W2), 1)
    even_col = c2 < RET_DV
    diag = (r2 < RET_DK) == even_col
    r2f = r2.astype(F32)
    rel = (r2 - (c2 % RET_DV)).astype(F32)
    for s in range(RET_SLABS):
        lgf2 = jnp.where(even_col, lg_ref[0, 2 * s], lg_ref[0, 2 * s + 1])
        lgb2 = jnp.where(even_col, lg_ref[1, 2 * s], lg_ref[1, 2 * s + 1])
        dmat_ref[s] = jnp.where(rel >= 0, jnp.exp(jnp.maximum(rel, 0.0) * lgf2), 0.0) \
            + jnp.where(rel <= 0, jnp.exp(jnp.maximum(-rel, 0.0) * lgb2), 0.0)
        zf_ref[s] = jnp.exp((C - 1 - r2f) * lgf2)
        zb_ref[s] = jnp.exp(r2f * lgb2)
        gf_ref[s] = jnp.exp(jnp.full((C, W2), C, F32) * lgf2)
        gb_ref[s] = jnp.exp(jnp.full((C, W2), C, F32) * lgb2)
        lgf1 = jnp.where(low, lg_ref[0, 2 * s], lg_ref[0, 2 * s + 1])
        lgb1 = jnp.where(low, lg_ref[1, 2 * s], lg_ref[1, 2 * s + 1])
        rif = ri.astype(F32)
        xf_ref[s] = jnp.exp((rif + 1.0) * lgf1)
        xb_ref[s] = jnp.exp((C - rif) * lgb1)
        rf_ref[s] = jnp.zeros((C, W2), F32)
        rb_ref[s] = jnp.zeros((C, W2), F32)

    def bwd_chunk(k_blk, v_blk, t):
        for s in range(RET_SLABS):
            st_ref[t, s, C:2 * C, :] = rb_ref[s].astype(BF16)
            vf = v_blk[:, s * W2:(s + 1) * W2].astype(F32)
            vz = jnp.concatenate([(vf * zf_ref[s]).astype(BF16), (vf * zb_ref[s]).astype(BF16)], axis=1)
            inc = lax.dot_general(k_blk[:, s * LANES:(s + 1) * LANES], vz, (((0,), (0,)), ((), ())),
                                  preferred_element_type=F32)
            uf_ref[t, s] = jnp.where(diag, inc[:, 0:W2], 0.0)
            rb_ref[s] = gb_ref[s] * rb_ref[s] + jnp.where(diag, inc[:, W2:2 * W2], 0.0)

    def fwd_chunk(q_blk, k_blk, v_blk, g_blk, t, store):
        for s in range(RET_SLABS):
            qs = q_blk[:, s * LANES:(s + 1) * LANES]
            ks = k_blk[:, s * LANES:(s + 1) * LANES]
            v2 = v_blk[:, s * W2:(s + 1) * W2]
            zk = jnp.zeros_like(ks)
            k2 = jnp.concatenate([jnp.where(low, ks, zk), jnp.where(low, zk, ks)], axis=0)
            sc = lax.dot_general(qs, k2, (((1,), (1,)), ((), ())), preferred_element_type=F32) * dmat_ref[s]
            zv = jnp.zeros_like(v2)
            vv = jnp.concatenate([jnp.where(even_col, v2, zv), jnp.where(even_col, zv, v2)], axis=0)
            y2 = jnp.dot(sc.astype(BF16), vv, preferred_element_type=F32)
            qf = qs.astype(F32)
            qx = jnp.concatenate([(qf * xf_ref[s]).astype(BF16), (qf * xb_ref[s]).astype(BF16)], axis=1)
            st_ref[t, s, 0:C, :] = rf_ref[s].astype(BF16)
            y2 = y2 + jnp.dot(qx, st_ref[t, s], preferred_element_type=F32)
            if store is not None:
                for e in range(2):
                    h = 2 * s + e
                    y = y2[:, e * RET_DV:(e + 1) * RET_DV]
                    gh = g_blk[:, h * RET_DV:(h + 1) * RET_DV].astype(F32)
                    yn = y * lax.rsqrt(jnp.mean(y * y, axis=-1, keepdims=True) + EPS)
                    store(h, (gh * yn).astype(BF16))
            rf_ref[s] = gf_ref[s] * rf_ref[s] + uf_ref[t, s]

    chunk = lambda t: slice(t * C, (t + 1) * C)

    for t in reversed(range(nc)):
        bwd_chunk(kc_ref[0, chunk(t), :], vc_ref[0, chunk(t), :], t)
    for j in reversed(range(n)):
        bwd_chunk(k_ref[0, chunk(j), :], v_ref[0, chunk(j), :], nc + j)

    def store_to(out_ref, sl):
        def store(h, val):
            out_ref[0, sl, h * RET_DV:(h + 1) * RET_DV] = val
        return store

    for t in range(nc):
        sl = chunk(t)
        if ctx_out:
            fwd_chunk(qc_ref[0, sl, :], kc_ref[0, sl, :], vc_ref[0, sl, :], gc_ref[0, sl, :], t, store_to(yc_ref, sl))
        else:
            for s in range(RET_SLABS):
                rf_ref[s] = gf_ref[s] * rf_ref[s] + uf_ref[t, s]
    for j in range(n):
        sl = chunk(j)
        fwd_chunk(q_ref[0, sl, :], k_ref[0, sl, :], v_ref[0, sl, :], g_ref[0, sl, :], nc + j, store_to(y_ref, sl))


def _retention_pool(log_gamma, ctx_parts, lat_parts, pool_parts, layer, pool_w, pool_scale, *, ctx_out):
    b, s, _ = lat_parts[0].shape
    lc = ctx_parts[-1].shape[1]
    nc, n = lc // RET_CHUNK, s // RET_CHUNK
    full = lambda a: pl.BlockSpec((1,) + a.shape[1:], lambda i: (i, 0, 0))
    in_specs = [pl.BlockSpec(memory_space=pltpu.SMEM)] + [full(a) for a in (*ctx_parts, *lat_parts, *pool_parts)] \
        + [_layer_block(pool_w, layer), _layer_block(pool_scale, layer)]
    vw = RET_HEADS * RET_DV
    lens = ([lc] if ctx_out else []) + [s]
    out_shape = [jax.ShapeDtypeStruct((b, ln, vw), BF16) for ln in lens] \
        + [jax.ShapeDtypeStruct((b, ln, POOL_GROUPS * POOL_GW), BF16) for ln in lens]
    out_specs = [full(o) for o in out_shape]
    wide = (RET_SLABS, RET_CHUNK, 2 * RET_DV)
    narrow = (RET_SLABS, RET_CHUNK, LANES)
    scratch = [pltpu.VMEM(wide, F32),
               pltpu.VMEM(wide, F32), pltpu.VMEM(wide, F32),
               pltpu.VMEM(narrow, F32), pltpu.VMEM(narrow, F32),
               pltpu.VMEM(wide, F32), pltpu.VMEM(wide, F32),
               pltpu.VMEM(wide, F32), pltpu.VMEM(wide, F32),
               pltpu.VMEM((nc + n, RET_SLABS, 2 * RET_CHUNK, 2 * RET_DV), BF16),
               pltpu.VMEM((nc + n, RET_SLABS, RET_CHUNK, 2 * RET_DV), F32),
               pltpu.VMEM((s + 32, LANES), F32), pltpu.VMEM((s + 32, LANES), F32)]
    res = pl.pallas_call(
        functools.partial(_ret_kernel, nc=nc, n=n, ctx_out=ctx_out),
        grid=(b,), in_specs=in_specs, out_specs=out_specs, out_shape=out_shape,
        scratch_shapes=scratch, compiler_params=_cparams(1),
        name="retention_pool",
    )(log_gamma, *ctx_parts, *lat_parts, *pool_parts, pool_w, pool_scale)
    return res if ctx_out else (None, res[0], None, res[1])


def _pool_tokens(u_ref, w_ref, sc_ref, o_ref, pa_ref, pb_ref):
    l = u_ref.shape[1]
    pad = 16
    t = lax.broadcasted_iota(jnp.int32, (l, LANES), 0)
    zeros_edge = jnp.zeros((pad, LANES), F32)
    ext = l + 16
    for g, w in enumerate(POOL_WINDOWS):
        ug = u_ref[0, :, g * POOL_GW:(g + 1) * POOL_GW].astype(F32)
        for buf in (pa_ref, pb_ref):
            buf[0:pad, :] = zeros_edge
            buf[pad + l:pad + l + pad, :] = zeros_edge
        pa_ref[pad:pad + l, :] = ug
        pb_ref[8:8 + ext, :] = pa_ref[7:7 + ext, :] + pa_ref[8:8 + ext, :]
        src, dst = pb_ref, pa_ref
        step = 1
        while 2 * step < w:
            dst[8:8 + ext, :] = src[8 - step:8 - step + ext, :] + src[8 + step:8 + step + ext, :]
            src, dst = dst, src
            step *= 2
        hw = w // 2
        cnt = (jnp.minimum(t + hw, l) - jnp.maximum(t - hw, 0)).astype(F32)
        pooled = src[pad:pad + l, :] / cnt
        mixed = jnp.dot((pooled - ug).astype(BF16), w_ref[g], preferred_element_type=F32)
        o_ref[0, :, g * POOL_GW:(g + 1) * POOL_GW] = (mixed * sc_ref[:, g * POOL_GW:(g + 1) * POOL_GW]).astype(BF16)


def _attn_kernel(*refs, local, nq, ncb):
    sink_ref = refs[0]
    if local:
        q_ref, k_ref, v_ref, kc_ref, vc_ref, o_ref, pall_ref = refs[1:8]
        slot_a, slot_b = refs[8:10], refs[10:12]
    else:
        q_ref, kc_ref, vc_ref, o_ref, pall_ref = refs[1:6]
        slot_a = refs[6:8]
        k_ref = v_ref = None
    BLK = ATT_BLOCK
    KB = 2 * BLK
    nloc = 3 if local else 0
    kj = lax.broadcasted_iota(jnp.int32, (BLK, 2 * BLK), 0)
    qi = lax.broadcasted_iota(jnp.int32, (BLK, 2 * BLK), 1) % BLK
    tri_prev = jnp.where(kj >= qi, 0.0, NEG_BIG)
    tri_next = jnp.where(kj <= qi, 0.0, NEG_BIG)
    head_row = lax.broadcasted_iota(jnp.int32, (LANES, BLK), 0) < ATT_HD
    nkeys = (nloc + ncb) * KB
    orow = lax.broadcasted_iota(jnp.int32, (16, nkeys), 0)
    ocol = lax.broadcasted_iota(jnp.int32, (16, nkeys), 1)
    ones_rows = jnp.where(orow == (ocol % KB) // BLK, 1.0, 0.0).astype(BF16)

    def block_rows(nblk):
        nb = nblk
        ds = lambda blk, size: pl.ds(blk * size, size)
        neigh = [max(nb - 1, 0), nb, min(nb + 1, nq - 1)]
        return nb, ds(nb, BLK), neigh[:nloc], [ds(blk, KB) for blk in neigh[:nloc]]

    def scores(nblk, slot):
        s_ref, m_ref = slot
        nb, _, _, windows = block_rows(nblk)
        if local:
            bias_prev = tri_prev + (NEG_BIG if nb == 0 else 0.0)
            bias_next = tri_next + (NEG_BIG if nb == nq - 1 else 0.0)
        for kh in range(ATT_KV_HEADS):
            qt = q_ref[0, nb, kh]
            kcat = jnp.concatenate([k_ref[kh, 0, windows[j], :] for j in range(nloc)] + [kc_ref[kh, 0]], axis=0)
            sc = jnp.dot(kcat, qt, preferred_element_type=F32)
            mx = [None, None]
            for j in range(nloc + ncb):
                for e in range(2):
                    krow = slice((2 * j + e) * BLK, (2 * j + e + 1) * BLK)
                    c = sc[krow]
                    if local and j == 0:
                        c = c + bias_prev
                    if local and j == 2:
                        c = c + bias_next
                    s_ref[kh, krow, :] = c
                    cm = jnp.max(c.reshape(BLK // 8, 8, 2 * BLK), axis=0)
                    mx[e] = cm if mx[e] is None else jnp.maximum(mx[e], cm)
            for e in range(2):
                m_ref[kh, e] = mx[e]

    def softmax_pv(nblk, slot):
        s_ref, m_ref = slot
        p_ref = pall_ref.at[nblk]
        nb, rows, neigh, _ = block_rows(nblk)
        for kh in range(ATT_KV_HEADS):
            sink_term = [[None, None], [None, None]]
            for s in range(2):
                scols = slice(s * BLK, (s + 1) * BLK)
                for e in range(2):
                    sink = sink_ref[kh * 4 + 2 * s + e] * LOG2E
                    m = jnp.maximum(jnp.max(m_ref[kh, e, :, scols], axis=0, keepdims=True), sink)
                    for j in range(nloc + ncb):
                        kr = slice((2 * j + e) * BLK, (2 * j + e + 1) * BLK)
                        p_ref[kh, kr, scols] = jnp.exp2(s_ref[kh, kr, scols] - m).astype(BF16)
                    sink_term[s][e] = jnp.exp2(sink - m)
            vcat = jnp.concatenate([v_ref[kh, 0, neigh[j]] for j in range(nloc)]
                                   + [vc_ref[kh, 0, j] for j in range(ncb)], axis=1)
            ot = jnp.dot(jnp.concatenate([vcat, ones_rows], axis=0), p_ref[kh],
                         preferred_element_type=F32)
            for s in range(2):
                scols = slice(s * BLK, (s + 1) * BLK)
                inv = [1.0 / (ot[LANES + e:LANES + e + 1, scols] + sink_term[s][e]) for e in range(2)]
                scale = jnp.where(head_row, inv[0], inv[1])
                o_ref[0, rows, (2 * kh + s) * LANES:(2 * kh + s + 1) * LANES] = \
                    (ot[0:LANES, scols] * scale).T.astype(BF16)

    if not local:
        for nblk in range(nq):
            scores(nblk, slot_a)
            softmax_pv(nblk, slot_a)
        return

    scores(0, slot_a)
    for nblk in range(nq):
        cur, nxt = (slot_a, slot_b) if nblk % 2 == 0 else (slot_b, slot_a)
        if nblk + 1 < nq:
            scores(nblk + 1, nxt)
        softmax_pv(nblk, cur)


def _attention(sink, q, k, v, kc, vc):
    b, nq = q.shape[:2]
    l = nq * ATT_BLOCK
    qw = ATT_HEADS * ATT_HD
    local = k is not None
    ncb = vc.shape[2]
    tok = pl.BlockSpec((1, l, qw), lambda i: (i, 0, 0))
    lead = lambda a: pl.BlockSpec((1,) + a.shape[1:], lambda i: (i,) + (0,) * (a.ndim - 1))
    second = lambda a: pl.BlockSpec((a.shape[0], 1) + a.shape[2:], lambda i: (0, i) + (0,) * (a.ndim - 2))
    kvs = ([k, v] if local else []) + [kc, vc]
    args = [sink, q] + kvs
    in_specs = [pl.BlockSpec(memory_space=pltpu.SMEM), lead(q)] + [second(a) for a in kvs]
    nkb = (3 if local else 0) + ncb
    tile = (ATT_KV_HEADS, nkb * 2 * ATT_BLOCK, 2 * ATT_BLOCK)
    slot = [pltpu.VMEM(tile, F32), pltpu.VMEM((ATT_KV_HEADS, 2, 8, 2 * ATT_BLOCK), F32)]
    scratch = [pltpu.VMEM((nq,) + tile, BF16)] + slot * (2 if local else 1)
    return pl.pallas_call(
        functools.partial(_attn_kernel, local=local, nq=nq, ncb=ncb),
        grid=(b,), in_specs=in_specs, out_specs=tok,
        out_shape=jax.ShapeDtypeStruct((b, l, qw), BF16),
        scratch_shapes=scratch,
        compiler_params=_cparams(1),
        name="attn_window" if local else "attn_ctx",
    )(*args)


def _merge_mlp_kernel(x_ref, yr_ref, yp_ref, ya_ref, gt_ref, mod_ref, nw_ref,
                      wr_ref, wp_ref, wa_ref, wo_ref, w1_ref, w2_ref, o_ref, *, d, hid_chunk):
    mod = mod_ref[0]
    g1, sh2, sc2, g2 = mod[:, 2 * d:3 * d], mod[:, 3 * d:4 * d], mod[:, 4 * d:5 * d], mod[:, 5 * d:6 * d]
    y = None
    for i, (y_ref, w_ref) in enumerate(((yr_ref, wr_ref), (yp_ref, wp_ref), (ya_ref, wa_ref))):
        proj = jnp.dot(y_ref[0], w_ref[...], preferred_element_type=F32)
        term = _sigmoid(gt_ref[0, :, i * d:(i + 1) * d].astype(F32)) * proj
        y = term if y is None else y + term
    mix = jnp.dot(y.astype(BF16), wo_ref[...], preferred_element_type=F32)
    x1 = x_ref[0] + g1 * mix
    ms = jnp.mean(x1 * x1, axis=-1, keepdims=True)
    h = ((x1 * lax.rsqrt(ms + EPS) * nw_ref[...]) * (1.0 + sc2) + sh2).astype(BF16)
    acc = None
    hidden = w1_ref.shape[1]
    for c in range(hidden // hid_chunk):
        a = jnp.dot(h, w1_ref[:, c * hid_chunk:(c + 1) * hid_chunk], preferred_element_type=F32)
        a = jnp.maximum(a, 0.0)
        part = jnp.dot((a * a).astype(BF16), w2_ref[c * hid_chunk:(c + 1) * hid_chunk, :], preferred_element_type=F32)
        acc = part if acc is None else acc + part
    o_ref[0] = x1 + g2 * acc


def _merge_mlp(x3, y_ret, y_pool, y_att, gates, mod3, mod_row, layer, nw, weights, *, tm):
    bx, l, d = x3.shape
    row = lambda b, i: (b, i, 0)
    bw = y_ret.shape[2]
    in_specs = [pl.BlockSpec((1, tm, d), row)] + [pl.BlockSpec((1, tm, bw), row)] * 3 + [
        pl.BlockSpec((1, tm, N_BRANCH * d), row),
        pl.BlockSpec((1, 1, mod3.shape[2]), lambda b, i: (mod_row(b), 0, 0)),
        _layer_block(nw, layer)] + [_layer_block(w, None, resident=True) for w in weights]
    return pl.pallas_call(
        functools.partial(_merge_mlp_kernel, d=d, hid_chunk=1024),
        grid=(bx, l // tm), in_specs=in_specs, out_specs=pl.BlockSpec((1, tm, d), row),
        out_shape=jax.ShapeDtypeStruct((bx, l, d), F32),
        compiler_params=_cparams(2),
        name="merge_mlp",
    )(x3, y_ret, y_pool, y_att, gates, mod3, nw, *weights)


def _rope_tables(s):
    pos = jnp.arange(s, dtype=F32)
    half = RET_DK // 2
    ang = pos[:, None] * (RET_ROPE_BASE ** (-jnp.arange(half, dtype=F32) / half))[None, :]
    cos_r = jnp.tile(jnp.concatenate([jnp.cos(ang), jnp.cos(ang)], axis=1), (1, LANES // RET_DK))
    sin_r = jnp.tile(jnp.concatenate([-jnp.sin(ang), jnp.sin(ang)], axis=1), (1, LANES // RET_DK))
    q = ATT_HD // 4
    freq = ROPE_BASE ** (-jnp.arange(q, dtype=F32) / q)
    ar = (jnp.arange(s) // GRID_W).astype(F32)[:, None] * freq[None, :]
    ac = (jnp.arange(s) % GRID_W).astype(F32)[:, None] * freq[None, :]
    cos_a = jnp.concatenate([jnp.cos(ar), jnp.cos(ar), jnp.cos(ac), jnp.cos(ac)], axis=1)
    sin_a = jnp.concatenate([-jnp.sin(ar), jnp.sin(ar), -jnp.sin(ac), jnp.sin(ac)], axis=1)
    return cos_r, sin_r, jnp.tile(cos_a, (1, LANES // ATT_HD)), jnp.tile(sin_a, (1, LANES // ATT_HD))


def kernel(x, c, ctx, c_ctx, norm1_w, norm2_w, ada_w, ada_b, w_in, ret_decay, pool_w, pool_scale, q_norm_w, k_norm_w, attn_sink, w_ret_out, w_pool_out, w_attn_out, w_out, w_mlp1, w_mlp2):
    b, s, d = x.shape
    lc = ctx.shape[1]
    depth = w_in.shape[0]
    tm = 512 if s % 512 == 0 else 256
    tmc = 512 if (b * lc) % 512 == 0 else 256
    tm_in = 2 * INPROJ_SUBTILE if s % (2 * INPROJ_SUBTILE) == 0 else tm

    rows = -(-(b + 1) // 8) * 8
    cc = jnp.zeros((rows, d), F32).at[:b].set(c).at[b].set(c_ctx)
    mod3 = _modulation(cc, ada_w, ada_b).reshape(depth * rows, 1, 6 * d)

    tables = _rope_tables(s)
    xc = ctx.reshape(1, b * lc, d)

    log_gamma = jnp.log1p(-jnp.exp2(-ret_decay.astype(F32)))
    w_in_b = w_in.astype(BF16)
    nw1, nw2 = norm1_w.reshape(depth, 1, d), norm2_w.reshape(depth, 1, d)
    qnw = jnp.tile(q_norm_w, (1, LANES // ATT_HD)).reshape(depth, 1, LANES)
    knw = jnp.tile(k_norm_w, (1, LANES // ATT_HD)).reshape(depth, 1, LANES)
    pw = pool_w.astype(BF16)
    psc = pool_scale.reshape(depth, 1, -1)
    merge_w_f32 = (w_ret_out, w_pool_out, w_attn_out, w_out, w_mlp1, w_mlp2)
    per_batch = lambda p: p.reshape(b, lc, p.shape[-1])
    ncb = lc // ATT_BLOCK

    for l in range(depth):
        last = l == depth - 1
        lat_row = lambda bi, l=l: l * rows + bi
        ctx_row = lambda bi, l=l: l * rows + b

        cparts = _inproj(xc, mod3, ctx_row, l, nw1, w_in_b, qnw, knw, None, tm=tmc, ctx_only=last)
        rk_c, rv_c = per_batch(cparts[0]), per_batch(cparts[1])
        ak_c = cparts[2].reshape(ATT_KV_HEADS, b, 2 * lc, LANES)
        av_c = cparts[3].reshape(ATT_KV_HEADS, b, ncb, LANES, 2 * ATT_BLOCK)

        rk, rv, ak, av, rq, rg, aq, pu, gt, *weights = _inproj(x, mod3, lat_row, l, nw1, w_in_b, qnw, knw, tables,
                                                               tm=tm_in, ctx_only=False, cast=merge_w_f32)
        if last:
            _, y_ret, _, y_pool = _retention_pool(log_gamma[l], (rk_c, rv_c), (rq, rk, rv, rg), (pu,),
                                                  l, pw, psc, ctx_out=False)
        else:
            rq_c, rg_c, pu_c, gt_c = [per_batch(cparts[i]) for i in (4, 5, 7, 8)]
            aq_c = cparts[6].reshape(b, ncb, ATT_KV_HEADS, LANES, 2 * ATT_BLOCK)
            y_ret_c, y_ret, y_pool_c, y_pool = _retention_pool(log_gamma[l], (rq_c, rk_c, rv_c, rg_c),
                                                               (rq, rk, rv, rg), (pu_c, pu), l, pw, psc, ctx_out=True)
        y_att = _attention(attn_sink[l], aq, ak, av, ak_c, av_c)
        x = _merge_mlp(x, y_ret, y_pool, y_att, gt, mod3, lat_row, l, nw2, weights, tm=tm)

        if not last:
            y_att_c = _attention(attn_sink[l], aq_c, None, None, ak_c, av_c)
            flat = lambda a: a.reshape(1, b * lc, a.shape[-1])
            xc = _merge_mlp(xc, flat(y_ret_c), flat(y_pool_c), flat(y_att_c), flat(gt_c), mod3, ctx_row, l, nw2,
                            weights, tm=tmc)
    return x
```

```python
import functools

import jax
import jax.numpy as jnp
from jax import lax
from jax.experimental import pallas as pl
from jax.experimental.pallas import tpu as pltpu

F32 = jnp.float32
BF16 = jnp.bfloat16

EPS = 1e-6
LOG2E = 1.4426950408889634
LANES = 128
RET_HEADS, RET_DK, RET_DV, RET_CHUNK = 4, 64, 128, 128
RET_SLABS = RET_HEADS // 2
RET_ROPE_BASE = 10000.0
POOL_GROUPS, POOL_GW = 4, 128
POOL_WINDOWS = (2, 4, 8, 16)
ATT_HEADS, ATT_KV_HEADS, ATT_HD = 8, 2, 64
ATT_WINDOW = ATT_BLOCK = 128
ROPE_BASE = 10000.0
GRID_W = 64
N_BRANCH = 3
CTX_SIDE_COLS = 1024
INPROJ_SUBTILE = 512
NEG_BIG = -1e30
VMEM_LIMIT = 62 * 1024 * 1024


def _cparams(n_axes):
    return pltpu.CompilerParams(dimension_semantics=("arbitrary",) * n_axes,
                                vmem_limit_bytes=VMEM_LIMIT)


def _layer_block(stacked, layer, block=None, resident=False):
    if layer is None:
        block = tuple(stacked.shape) if block is None else block
        shape, idx = block, (0,) * len(block)
    else:
        block = tuple(stacked.shape[1:]) if block is None else block
        shape, idx = (None,) + block, (layer,) + (0,) * len(block)
    return pl.BlockSpec(shape, lambda *_: idx, pipeline_mode=pl.Buffered(1) if resident else None)


def _sigmoid(x):
    return 1.0 / (1.0 + jnp.exp(-x))


def _mod_kernel(cc_ref, w_ref, b_ref, o_ref):
    cc = cc_ref[...]
    s = (cc * _sigmoid(cc)).astype(BF16)
    o_ref[0] = jnp.dot(s, w_ref[0].astype(BF16), preferred_element_type=F32) + b_ref[0]


def _modulation(cc, ada_w, ada_b):
    depth, d, n = ada_w.shape
    r = cc.shape[0]
    tn = 1024
    return pl.pallas_call(
        _mod_kernel,
        grid=(depth, n // tn),
        in_specs=[pl.BlockSpec((r, d), lambda l, j: (0, 0)),
                  pl.BlockSpec((1, d, tn), lambda l, j: (l, 0, j)),
                  pl.BlockSpec((1, 1, tn), lambda l, j: (l, 0, j))],
        out_specs=pl.BlockSpec((1, r, tn), lambda l, j: (l, 0, j)),
        out_shape=jax.ShapeDtypeStruct((depth, r, n), F32),
        compiler_params=_cparams(2),
        name="adaln_mod",
    )(cc, ada_w, ada_b.reshape(depth, 1, n))


def _swap_halves(x, half):
    lane = lax.broadcasted_iota(jnp.int32, x.shape, 1)
    first = (lane % (2 * half)) < half
    return jnp.where(first, pltpu.roll(x, LANES - half, 1), pltpu.roll(x, half, 1))


def _rope(x, cos, sin, half):
    return x * cos + _swap_halves(x, half) * sin


def _inproj_kernel(*refs, d, ctx_only, use_rope, n_cast):
    x_ref, mod_ref, nw_ref, w_ref, qnw_ref, knw_ref = refs[:6]
    pos = 6
    if use_rope:
        cr_ref, sr_ref, ca_ref, sa_ref = refs[6:10]
        pos = 10
    cast_in, refs_out = refs[pos:pos + n_cast], refs[pos + n_cast:]
    outs, cast_out = refs_out[:len(refs_out) - n_cast], refs_out[len(refs_out) - n_cast:]
    tm = x_ref.shape[1]

    for src, dst in zip(cast_in, cast_out):
        dst[...] = src[...].astype(BF16)

    mod = mod_ref[0]
    shift, scale = mod[:, 0:d], mod[:, d:2 * d]
    sub = min(tm, INPROJ_SUBTILE)
    lane = lax.broadcasted_iota(jnp.int32, (sub, LANES), 1)
    low = lane < ATT_HD

    def slab(a, s):
        return a[:, s * LANES:(s + 1) * LANES]

    def qk_norm(a, w):
        sq = a * a
        lo = jnp.sum(jnp.where(low, sq, 0.0), axis=-1, keepdims=True)
        hi = jnp.sum(jnp.where(low, 0.0, sq), axis=-1, keepdims=True)
        msq = jnp.where(low, lo, hi) * (1.0 / ATT_HD)
        return a * lax.rsqrt(msq + EPS) * w

    def subtile(r0):
        rows = slice(r0, r0 + sub)
        blk0 = r0 // ATT_BLOCK
        x = x_ref[0, rows]
        ms = jnp.mean(x * x, axis=-1, keepdims=True)
        u = ((x * lax.rsqrt(ms + EPS) * nw_ref[...]) * (1.0 + scale) + shift).astype(BF16)

        def proj(a, b):
            return jnp.dot(u, w_ref[:, a:b], preferred_element_type=F32)

        def ret_rope(a):
            return _rope(a, cr_ref[rows], sr_ref[rows], RET_DK // 2) if use_rope else a

        def att_rope(a):
            return _rope(a, ca_ref[rows], sa_ref[rows], ATT_HD // 4) if use_rope else a

        def store_kv(a, out_ref):
            sw = pltpu.roll(a, ATT_HD, 1)
            zero = jnp.zeros_like(a)
            lo_hi = ((jnp.where(low, a, zero), jnp.where(low, zero, sw)),
                     (jnp.where(low, sw, zero), jnp.where(low, zero, a)))
            for kh in range(ATT_KV_HEADS):
                for j in range(sub // ATT_BLOCK):
                    for e in range(2):
                        k0 = (2 * (blk0 + j) + e) * ATT_BLOCK
                        out_ref[kh, 0, k0:k0 + ATT_BLOCK, :] = \
                            lo_hi[kh][e][j * ATT_BLOCK:(j + 1) * ATT_BLOCK].astype(BF16)

        rk_ref, rv_ref, ak_ref, av_ref = outs[:4]
        a = proj(0, 256)
        for s in range(RET_SLABS):
            rk_ref[0, rows, s * LANES:(s + 1) * LANES] = (ret_rope(slab(a, s)) * RET_DK ** -0.5).astype(BF16)
        rv_ref[0, rows] = proj(256, 768).astype(BF16)
        a = proj(768, 1024)
        store_kv(att_rope(qk_norm(slab(a, 0), knw_ref[...])), ak_ref)
        vs = slab(a, 1)
        zq = jnp.zeros((ATT_HD, ATT_BLOCK), F32)
        for j in range(sub // ATT_BLOCK):
            vt = vs[j * ATT_BLOCK:(j + 1) * ATT_BLOCK].T
            for kh in range(ATT_KV_HEADS):
                top = vt[kh * ATT_HD:(kh + 1) * ATT_HD]
                tile = jnp.concatenate([jnp.concatenate([top, zq], axis=1), jnp.concatenate([zq, top], axis=1)],
                                       axis=0)
                av_ref[kh, 0, blk0 + j] = tile.astype(BF16)
        if ctx_only:
            return
        rq_ref, rg_ref, aq_ref, pu_ref, gt_ref = outs[4:]
        a = proj(1024, 1280)
        for s in range(RET_SLABS):
            rq_ref[0, rows, s * LANES:(s + 1) * LANES] = ret_rope(slab(a, s)).astype(BF16)
        g = proj(1280, 1792)
        rg_ref[0, rows] = (g * _sigmoid(g)).astype(BF16)
        a = proj(1792, 2304)
        for s in range(ATT_HEADS // 2):
            qs = att_rope(qk_norm(slab(a, s), qnw_ref[...])) * (ATT_HD ** -0.5 * LOG2E)
            for j in range(sub // ATT_BLOCK):
                aq_ref[0, blk0 + j, s // 2, :, (s % 2) * ATT_BLOCK:(s % 2 + 1) * ATT_BLOCK] = \
                    qs[j * ATT_BLOCK:(j + 1) * ATT_BLOCK].T.astype(BF16)
        pu_ref[0, rows] = proj(2304, 2816).astype(BF16)
        for s in range(6):
            gt_ref[0, rows, s * 512:(s + 1) * 512] = proj(2816 + s * 512, 2816 + (s + 1) * 512).astype(BF16)

    for r0 in range(0, tm, sub):
        subtile(r0)


def _inproj(x3, mod3, mod_row, layer, nw, w, qnw, knw, tables, *, tm, ctx_only, cast=()):
    bx, l, d = x3.shape
    ncol = CTX_SIDE_COLS if ctx_only else w.shape[2]
    use_rope = tables is not None
    grid = (bx, l // tm)
    row = lambda b, i: (b, i, 0)
    in_specs = [pl.BlockSpec((1, tm, d), row),
                pl.BlockSpec((1, 1, mod3.shape[2]), lambda b, i: (mod_row(b), 0, 0)),
                _layer_block(nw, layer),
                _layer_block(w, layer, (d, ncol), resident=True),
                _layer_block(qnw, layer),
                _layer_block(knw, layer)]
    args = [x3, mod3, nw, w, qnw, knw]
    if use_rope:
        in_specs += [pl.BlockSpec((tm, LANES), lambda b, i: (i, 0))] * 4
        args += list(tables)
    kv_spec = pl.BlockSpec((ATT_KV_HEADS, 1, 2 * tm, LANES), lambda b, i: (0, b, i, 0))
    kv_shape = jax.ShapeDtypeStruct((ATT_KV_HEADS, bx, 2 * l, LANES), BF16)
    tok = lambda wd: (pl.BlockSpec((1, tm, wd), row), jax.ShapeDtypeStruct((bx, l, wd), BF16))
    nblk, tblk = l // ATT_BLOCK, tm // ATT_BLOCK
    vt_spec = pl.BlockSpec((ATT_KV_HEADS, 1, tblk, LANES, 2 * ATT_BLOCK), lambda b, i: (0, b, i, 0, 0))
    vt_shape = jax.ShapeDtypeStruct((ATT_KV_HEADS, bx, nblk, LANES, 2 * ATT_BLOCK), BF16)
    qt_spec = pl.BlockSpec((1, tblk, ATT_KV_HEADS, LANES, 2 * ATT_BLOCK), lambda b, i: (b, i, 0, 0, 0))
    qt_shape = jax.ShapeDtypeStruct((bx, nblk, ATT_KV_HEADS, LANES, 2 * ATT_BLOCK), BF16)
    outs = [tok(256), tok(512), (kv_spec, kv_shape), (vt_spec, vt_shape)]
    if not ctx_only:
        outs += [tok(256), tok(512), (qt_spec, qt_shape), tok(512), tok(3072)]
    steps = bx * (l // tm)
    for wf in cast:
        k, n = wf.shape[1:]
        rps = max(16, k // steps)
        per = steps // (k // rps)
        in_specs.append(pl.BlockSpec((None, rps, n), lambda b, i, per=per: (layer, (b * (l // tm) + i) // per, 0)))
        outs.append((pl.BlockSpec((rps, n), lambda b, i, per=per: ((b * (l // tm) + i) // per, 0)),
                     jax.ShapeDtypeStruct((k, n), BF16)))
        args.append(wf)
    return pl.pallas_call(
        functools.partial(_inproj_kernel, d=d, ctx_only=ctx_only, use_rope=use_rope, n_cast=len(cast)),
        grid=grid, in_specs=in_specs, out_specs=[o[0] for o in outs], out_shape=[o[1] for o in outs],
        compiler_params=_cparams(2),
        name="inproj_ctx" if not use_rope else "inproj",
    )(*args)


def _ret_kernel(*refs, nc, n, ctx_out):
    lg_ref = refs[0]
    if ctx_out:
        qc_ref, kc_ref, vc_ref, gc_ref, q_ref, k_ref, v_ref, g_ref = refs[1:9]
        puc_ref, pu_ref, pw_ref, psc_ref, yc_ref, y_ref, ypc_ref, yp_ref = refs[9:17]
        scr = refs[17:]
    else:
        kc_ref, vc_ref, q_ref, k_ref, v_ref, g_ref = refs[1:7]
        pu_ref, pw_ref, psc_ref, y_ref, yp_ref = refs[7:12]
        qc_ref = gc_ref = yc_ref = puc_ref = ypc_ref = None
        scr = refs[12:]
    dmat_ref, zf_ref, zb_ref, xf_ref, xb_ref, gf_ref, gb_ref, rf_ref, rb_ref, st_ref, uf_ref, pa_ref, pb_ref = scr

    if ctx_out:
        _pool_tokens(puc_ref, pw_ref, psc_ref, ypc_ref, pa_ref, pb_ref)
    _pool_tokens(pu_ref, pw_ref, psc_ref, yp_ref, pa_ref, pb_ref)
    C = RET_CHUNK
    W2 = 2 * RET_DV

    ri = lax.broadcasted_iota(jnp.int32, (C, LANES), 0)
    ci = lax.broadcasted_iota(jnp.int32, (C, LANES), 1)
    low = ci < RET_DK
    r2 = lax.broadcasted_iota(jnp.int32, (C, W2), 0)
    c2 = lax.broadcasted_iota(jnp.int32, (C, W2), 1)
    even_col = c2 < RET_DV
    diag = (r2 < RET_DK) == even_col
    r2f = r2.astype(F32)
    rel = (r2 - (c2 % RET_DV)).astype(F32)
    for s in range(RET_SLABS):
        lgf2 = jnp.where(even_col, lg_ref[0, 2 * s], lg_ref[0, 2 * s + 1])
        lgb2 = jnp.where(even_col, lg_ref[1, 2 * s], lg_ref[1, 2 * s + 1])
        dmat_ref[s] = jnp.where(rel >= 0, jnp.exp(jnp.maximum(rel, 0.0) * lgf2), 0.0) \
            + jnp.where(rel <= 0, jnp.exp(jnp.maximum(-rel, 0.0) * lgb2), 0.0)
        zf_ref[s] = jnp.exp((C - 1 - r2f) * lgf2)
        zb_ref[s] = jnp.exp(r2f * lgb2)
        gf_ref[s] = jnp.exp(jnp.full((C, W2), C, F32) * lgf2)
        gb_ref[s] = jnp.exp(jnp.full((C, W2), C, F32) * lgb2)
        lgf1 = jnp.where(low, lg_ref[0, 2 * s], lg_ref[0, 2 * s + 1])
        lgb1 = jnp.where(low, lg_ref[1, 2 * s], lg_ref[1, 2 * s + 1])
        rif = ri.astype(F32)
        xf_ref[s] = jnp.exp((rif + 1.0) * lgf1)
        xb_ref[s] = jnp.exp((C - rif) * lgb1)
        rf_ref[s] = jnp.zeros((C, W2), F32)
        rb_ref[s] = jnp.zeros((C, W2), F32)

    def bwd_chunk(k_blk, v_blk, t):
        for s in range(RET_SLABS):
            st_ref[t, s, C:2 * C, :] = rb_ref[s].astype(BF16)
            vf = v_blk[:, s * W2:(s + 1) * W2].astype(F32)
            vz = jnp.concatenate([(vf * zf_ref[s]).astype(BF16), (vf * zb_ref[s]).astype(BF16)], axis=1)
            inc = lax.dot_general(k_blk[:, s * LANES:(s + 1) * LANES], vz, (((0,), (0,)), ((), ())),
                                  preferred_element_type=F32)
            uf_ref[t, s] = jnp.where(diag, inc[:, 0:W2], 0.0)
            rb_ref[s] = gb_ref[s] * rb_ref[s] + jnp.where(diag, inc[:, W2:2 * W2], 0.0)

    def fwd_chunk(q_blk, k_blk, v_blk, g_blk, t, store):
        for s in range(RET_SLABS):
            qs = q_blk[:, s * LANES:(s + 1) * LANES]
            ks = k_blk[:, s * LANES:(s + 1) * LANES]
            v2 = v_blk[:, s * W2:(s + 1) * W2]
            zk = jnp.zeros_like(ks)
            k2 = jnp.concatenate([jnp.where(low, ks, zk), jnp.where(low, zk, ks)], axis=0)
            sc = lax.dot_general(qs, k2, (((1,), (1,)), ((), ())), preferred_element_type=F32) * dmat_ref[s]
            zv = jnp.zeros_like(v2)
            vv = jnp.concatenate([jnp.where(even_col, v2, zv), jnp.where(even_col, zv, v2)], axis=0)
            y2 = jnp.dot(sc.astype(BF16), vv, preferred_element_type=F32)
            qf = qs.astype(F32)
            qx = jnp.concatenate([(qf * xf_ref[s]).astype(BF16), (qf * xb_ref[s]).astype(BF16)], axis=1)
            st_ref[t, s, 0:C, :] = rf_ref[s].astype(BF16)
            y2 = y2 + jnp.dot(qx, st_ref[t, s], preferred_element_type=F32)
            if store is not None:
                for e in range(2):
                    h = 2 * s + e
                    y = y2[:, e * RET_DV:(e + 1) * RET_DV]
                    gh = g_blk[:, h * RET_DV:(h + 1) * RET_DV].astype(F32)
                    yn = y * lax.rsqrt(jnp.mean(y * y, axis=-1, keepdims=True) + EPS)
                    store(h, (gh * yn).astype(BF16))
            rf_ref[s] = gf_ref[s] * rf_ref[s] + uf_ref[t, s]

    chunk = lambda t: slice(t * C, (t + 1) * C)

    for t in reversed(range(nc)):
        bwd_chunk(kc_ref[0, chunk(t), :], vc_ref[0, chunk(t), :], t)
    for j in reversed(range(n)):
        bwd_chunk(k_ref[0, chunk(j), :], v_ref[0, chunk(j), :], nc + j)

    def store_to(out_ref, sl):
        def store(h, val):
            out_ref[0, sl, h * RET_DV:(h + 1) * RET_DV] = val
        return store

    for t in range(nc):
        sl = chunk(t)
        if ctx_out:
            fwd_chunk(qc_ref[0, sl, :], kc_ref[0, sl, :], vc_ref[0, sl, :], gc_ref[0, sl, :], t, store_to(yc_ref, sl))
        else:
            for s in range(RET_SLABS):
                rf_ref[s] = gf_ref[s] * rf_ref[s] + uf_ref[t, s]
    for j in range(n):
        sl = chunk(j)
        fwd_chunk(q_ref[0, sl, :], k_ref[0, sl, :], v_ref[0, sl, :], g_ref[0, sl, :], nc + j, store_to(y_ref, sl))


def _retention_pool(log_gamma, ctx_parts, lat_parts, pool_parts, layer, pool_w, pool_scale, *, ctx_out):
    b, s, _ = lat_parts[0].shape
    lc = ctx_parts[-1].shape[1]
    nc, n = lc // RET_CHUNK, s // RET_CHUNK
    full = lambda a: pl.BlockSpec((1,) + a.shape[1:], lambda i: (i, 0, 0))
    in_specs = [pl.BlockSpec(memory_space=pltpu.SMEM)] + [full(a) for a in (*ctx_parts, *lat_parts, *pool_parts)] \
        + [_layer_block(pool_w, layer), _layer_block(pool_scale, layer)]
    vw = RET_HEADS * RET_DV
    lens = ([lc] if ctx_out else []) + [s]
    out_shape = [jax.ShapeDtypeStruct((b, ln, vw), BF16) for ln in lens] \
        + [jax.ShapeDtypeStruct((b, ln, POOL_GROUPS * POOL_GW), BF16) for ln in lens]
    out_specs = [full(o) for o in out_shape]
    wide = (RET_SLABS, RET_CHUNK, 2 * RET_DV)
    narrow = (RET_SLABS, RET_CHUNK, LANES)
    scratch = [pltpu.VMEM(wide, F32),
               pltpu.VMEM(wide, F32), pltpu.VMEM(wide, F32),
               pltpu.VMEM(narrow, F32), pltpu.VMEM(narrow, F32),
               pltpu.VMEM(wide, F32), pltpu.VMEM(wide, F32),
               pltpu.VMEM(wide, F32), pltpu.VMEM(wide, F32),
               pltpu.VMEM((nc + n, RET_SLABS, 2 * RET_CHUNK, 2 * RET_DV), BF16),
               pltpu.VMEM((nc + n, RET_SLABS, RET_CHUNK, 2 * RET_DV), F32),
               pltpu.VMEM((s + 32, LANES), F32), pltpu.VMEM((s + 32, LANES), F32)]
    res = pl.pallas_call(
        functools.partial(_ret_kernel, nc=nc, n=n, ctx_out=ctx_out),
        grid=(b,), in_specs=in_specs, out_specs=out_specs, out_shape=out_shape,
        scratch_shapes=scratch, compiler_params=_cparams(1),
        name="retention_pool",
    )(log_gamma, *ctx_parts, *lat_parts, *pool_parts, pool_w, pool_scale)
    return res if ctx_out else (None, res[0], None, res[1])


def _pool_tokens(u_ref, w_ref, sc_ref, o_ref, pa_ref, pb_ref):
    l = u_ref.shape[1]
    pad = 16
    t8 = lax.broadcasted_iota(jnp.int32, (8, LANES), 0)
    zeros_edge = jnp.zeros((pad, LANES), F32)
    ext = l + 16
    for g, w in enumerate(POOL_WINDOWS):
        ug = u_ref[0, :, g * POOL_GW:(g + 1) * POOL_GW].astype(F32)
        for buf in (pa_ref, pb_ref):
            buf[0:pad, :] = zeros_edge
            buf[pad + l:pad + l + pad, :] = zeros_edge
        pa_ref[pad:pad + l, :] = ug
        pb_ref[8:8 + ext, :] = pa_ref[7:7 + ext, :] + pa_ref[8:8 + ext, :]
        src, dst = pb_ref, pa_ref
        step = 1
        while 2 * step < w:
            dst[8:8 + ext, :] = src[8 - step:8 - step + ext, :] + src[8 + step:8 + step + ext, :]
            src, dst = dst, src
            step *= 2
        hw = w // 2
        top, bot = slice(pad, pad + 8), slice(pad + l - 8, pad + l)
        src[top, :] = src[top, :] * jnp.where(t8 < hw, w / (t8 + hw).astype(F32), 1.0)
        tb = l - 8 + t8
        src[bot, :] = src[bot, :] * jnp.where(tb + hw > l, w / (l - tb + hw).astype(F32), 1.0)
        pooled = src[pad:pad + l, :] * (1.0 / w)
        mixed = jnp.dot((pooled - ug).astype(BF16), w_ref[g], preferred_element_type=F32)
        o_ref[0, :, g * POOL_GW:(g + 1) * POOL_GW] = (mixed * sc_ref[:, g * POOL_GW:(g + 1) * POOL_GW]).astype(BF16)


def _attn_kernel(*refs, local, nq, ncb):
    sink_ref = refs[0]
    if local:
        q_ref, k_ref, v_ref, kc_ref, vc_ref, o_ref, pall_ref = refs[1:8]
        slot_a, slot_b = refs[8:10], refs[10:12]
    else:
        q_ref, kc_ref, vc_ref, o_ref, pall_ref = refs[1:6]
        slot_a = refs[6:8]
        k_ref = v_ref = None
    BLK = ATT_BLOCK
    KB = 2 * BLK
    nloc = 3 if local else 0
    kj = lax.broadcasted_iota(jnp.int32, (BLK, 2 * BLK), 0)
    qi = lax.broadcasted_iota(jnp.int32, (BLK, 2 * BLK), 1) % BLK
    tri_prev = jnp.where(kj >= qi, 0.0, NEG_BIG)
    tri_next = jnp.where(kj <= qi, 0.0, NEG_BIG)
    head_row = lax.broadcasted_iota(jnp.int32, (LANES, BLK), 0) < ATT_HD
    nkeys = (nloc + ncb) * KB
    orow = lax.broadcasted_iota(jnp.int32, (16, nkeys), 0)
    ocol = lax.broadcasted_iota(jnp.int32, (16, nkeys), 1)
    ones_rows = jnp.where(orow == (ocol % KB) // BLK, 1.0, 0.0).astype(BF16)

    def block_rows(nblk):
        nb = nblk
        ds = lambda blk, size: pl.ds(blk * size, size)
        neigh = [max(nb - 1, 0), nb, min(nb + 1, nq - 1)]
        return nb, ds(nb, BLK), neigh[:nloc], [ds(blk, KB) for blk in neigh[:nloc]]

    def scores(nblk, slot):
        s_ref, m_ref = slot
        nb, _, _, windows = block_rows(nblk)
        if local:
            bias_prev = tri_prev + (NEG_BIG if nb == 0 else 0.0)
            bias_next = tri_next + (NEG_BIG if nb == nq - 1 else 0.0)
        for kh in range(ATT_KV_HEADS):
            qt = q_ref[0, nb, kh]
            kcat = jnp.concatenate([k_ref[kh, 0, windows[j], :] for j in range(nloc)] + [kc_ref[kh, 0]], axis=0)
            sc = jnp.dot(kcat, qt, preferred_element_type=F32)
            mx = [None, None]
            for j in range(nloc + ncb):
                for e in range(2):
                    krow = slice((2 * j + e) * BLK, (2 * j + e + 1) * BLK)
                    c = sc[krow]
                    if local and j == 0:
                        c = c + bias_prev
                    if local and j == 2:
                        c = c + bias_next
                    s_ref[kh, krow, :] = c
                    cm = jnp.max(c.reshape(BLK // 8, 8, 2 * BLK), axis=0)
                    mx[e] = cm if mx[e] is None else jnp.maximum(mx[e], cm)
            for e in range(2):
                m_ref[kh, e] = mx[e]

    def softmax_pv(nblk, slot):
        s_ref, m_ref = slot
        p_ref = pall_ref.at[nblk]
        nb, rows, neigh, _ = block_rows(nblk)
        for kh in range(ATT_KV_HEADS):
            sink_term = [[None, None], [None, None]]
            for s in range(2):
                scols = slice(s * BLK, (s + 1) * BLK)
                for e in range(2):
                    sink = sink_ref[kh * 4 + 2 * s + e] * LOG2E
                    m = jnp.maximum(jnp.max(m_ref[kh, e, :, scols], axis=0, keepdims=True), sink)
                    for j in range(nloc + ncb):
                        kr = slice((2 * j + e) * BLK, (2 * j + e + 1) * BLK)
                        p_ref[kh, kr, scols] = jnp.exp2(s_ref[kh, kr, scols] - m).astype(BF16)
                    sink_term[s][e] = jnp.exp2(sink - m)
            vcat = jnp.concatenate([v_ref[kh, 0, neigh[j]] for j in range(nloc)]
                                   + [vc_ref[kh, 0, j] for j in range(ncb)], axis=1)
            ot = jnp.dot(jnp.concatenate([vcat, ones_rows], axis=0), p_ref[kh],
                         preferred_element_type=F32)
            for s in range(2):
                scols = slice(s * BLK, (s + 1) * BLK)
                inv = [1.0 / (ot[LANES + e:LANES + e + 1, scols] + sink_term[s][e]) for e in range(2)]
                scale = jnp.where(head_row, inv[0], inv[1])
                o_ref[0, rows, (2 * kh + s) * LANES:(2 * kh + s + 1) * LANES] = \
                    (ot[0:LANES, scols] * scale).T.astype(BF16)

    if not local:
        for nblk in range(nq):
            scores(nblk, slot_a)
            softmax_pv(nblk, slot_a)
        return

    scores(0, slot_a)
    for nblk in range(nq):
        cur, nxt = (slot_a, slot_b) if nblk % 2 == 0 else (slot_b, slot_a)
        if nblk + 1 < nq:
            scores(nblk + 1, nxt)
        softmax_pv(nblk, cur)


def _attention(sink, q, k, v, kc, vc):
    b, nq = q.shape[:2]
    l = nq * ATT_BLOCK
    qw = ATT_HEADS * ATT_HD
    local = k is not None
    ncb = vc.shape[2]
    tok = pl.BlockSpec((1, l, qw), lambda i: (i, 0, 0))
    lead = lambda a: pl.BlockSpec((1,) + a.shape[1:], lambda i: (i,) + (0,) * (a.ndim - 1))
    second = lambda a: pl.BlockSpec((a.shape[0], 1) + a.shape[2:], lambda i: (0, i) + (0,) * (a.ndim - 2))
    kvs = ([k, v] if local else []) + [kc, vc]
    args = [sink, q] + kvs
    in_specs = [pl.BlockSpec(memory_space=pltpu.SMEM), lead(q)] + [second(a) for a in kvs]
    nkb = (3 if local else 0) + ncb
    tile = (ATT_KV_HEADS, nkb * 2 * ATT_BLOCK, 2 * ATT_BLOCK)
    slot = [pltpu.VMEM(tile, F32), pltpu.VMEM((ATT_KV_HEADS, 2, 8, 2 * ATT_BLOCK), F32)]
    scratch = [pltpu.VMEM((nq,) + tile, BF16)] + slot * (2 if local else 1)
    return pl.pallas_call(
        functools.partial(_attn_kernel, local=local, nq=nq, ncb=ncb),
        grid=(b,), in_specs=in_specs, out_specs=tok,
        out_shape=jax.ShapeDtypeStruct((b, l, qw), BF16),
        scratch_shapes=scratch,
        compiler_params=_cparams(1),
        name="attn_window" if local else "attn_ctx",
    )(*args)


def _merge_mlp_kernel(x_ref, yr_ref, yp_ref, ya_ref, gt_ref, mod_ref, nw_ref,
                      wr_ref, wp_ref, wa_ref, wo_ref, w1_ref, w2_ref, o_ref, *, d, hid_chunk):
    mod = mod_ref[0]
    g1, sh2, sc2, g2 = mod[:, 2 * d:3 * d], mod[:, 3 * d:4 * d], mod[:, 4 * d:5 * d], mod[:, 5 * d:6 * d]
    y = None
    for i, (y_ref, w_ref) in enumerate(((yr_ref, wr_ref), (yp_ref, wp_ref), (ya_ref, wa_ref))):
        proj = jnp.dot(y_ref[0], w_ref[...], preferred_element_type=F32)
        term = _sigmoid(gt_ref[0, :, i * d:(i + 1) * d].astype(F32)) * proj
        y = term if y is None else y + term
    mix = jnp.dot(y.astype(BF16), wo_ref[...], preferred_element_type=F32)
    x1 = x_ref[0] + g1 * mix
    ms = jnp.mean(x1 * x1, axis=-1, keepdims=True)
    h = ((x1 * lax.rsqrt(ms + EPS) * nw_ref[...]) * (1.0 + sc2) + sh2).astype(BF16)
    acc = None
    hidden = w1_ref.shape[1]
    for c in range(hidden // hid_chunk):
        a = jnp.dot(h, w1_ref[:, c * hid_chunk:(c + 1) * hid_chunk], preferred_element_type=F32)
        a = jnp.maximum(a, 0.0)
        part = jnp.dot((a * a).astype(BF16), w2_ref[c * hid_chunk:(c + 1) * hid_chunk, :], preferred_element_type=F32)
        acc = part if acc is None else acc + part
    o_ref[0] = x1 + g2 * acc


def _merge_mlp(x3, y_ret, y_pool, y_att, gates, mod3, mod_row, layer, nw, weights, *, tm):
    bx, l, d = x3.shape
    row = lambda b, i: (b, i, 0)
    bw = y_ret.shape[2]
    in_specs = [pl.BlockSpec((1, tm, d), row)] + [pl.BlockSpec((1, tm, bw), row)] * 3 + [
        pl.BlockSpec((1, tm, N_BRANCH * d), row),
        pl.BlockSpec((1, 1, mod3.shape[2]), lambda b, i: (mod_row(b), 0, 0)),
        _layer_block(nw, layer)] + [_layer_block(w, None, resident=True) for w in weights]
    return pl.pallas_call(
        functools.partial(_merge_mlp_kernel, d=d, hid_chunk=1024),
        grid=(bx, l // tm), in_specs=in_specs, out_specs=pl.BlockSpec((1, tm, d), row),
        out_shape=jax.ShapeDtypeStruct((bx, l, d), F32),
        compiler_params=_cparams(2),
        name="merge_mlp",
    )(x3, y_ret, y_pool, y_att, gates, mod3, nw, *weights)


def _rope_tables(s):
    pos = jnp.arange(s, dtype=F32)
    half = RET_DK // 2
    ang = pos[:, None] * (RET_ROPE_BASE ** (-jnp.arange(half, dtype=F32) / half))[None, :]
    cos_r = jnp.tile(jnp.concatenate([jnp.cos(ang), jnp.cos(ang)], axis=1), (1, LANES // RET_DK))
    sin_r = jnp.tile(jnp.concatenate([-jnp.sin(ang), jnp.sin(ang)], axis=1), (1, LANES // RET_DK))
    q = ATT_HD // 4
    freq = ROPE_BASE ** (-jnp.arange(q, dtype=F32) / q)
    ar = (jnp.arange(s) // GRID_W).astype(F32)[:, None] * freq[None, :]
    ac = (jnp.arange(s) % GRID_W).astype(F32)[:, None] * freq[None, :]
    cos_a = jnp.concatenate([jnp.cos(ar), jnp.cos(ar), jnp.cos(ac), jnp.cos(ac)], axis=1)
    sin_a = jnp.concatenate([-jnp.sin(ar), jnp.sin(ar), -jnp.sin(ac), jnp.sin(ac)], axis=1)
    return cos_r, sin_r, jnp.tile(cos_a, (1, LANES // ATT_HD)), jnp.tile(sin_a, (1, LANES // ATT_HD))


def kernel(x, c, ctx, c_ctx, norm1_w, norm2_w, ada_w, ada_b, w_in, ret_decay, pool_w, pool_scale, q_norm_w, k_norm_w, attn_sink, w_ret_out, w_pool_out, w_attn_out, w_out, w_mlp1, w_mlp2):
    b, s, d = x.shape
    lc = ctx.shape[1]
    depth = w_in.shape[0]
    tm = 512 if s % 512 == 0 else 256
    tmc = 512 if (b * lc) % 512 == 0 else 256
    tm_in = 2 * INPROJ_SUBTILE if s % (2 * INPROJ_SUBTILE) == 0 else tm

    rows = -(-(b + 1) // 8) * 8
    cc = jnp.zeros((rows, d), F32).at[:b].set(c).at[b].set(c_ctx)
    mod3 = _modulation(cc, ada_w, ada_b).reshape(depth * rows, 1, 6 * d)

    tables = _rope_tables(s)
    xc = ctx.reshape(1, b * lc, d)

    log_gamma = jnp.log1p(-jnp.exp2(-ret_decay.astype(F32)))
    w_in_b = w_in.astype(BF16)
    nw1, nw2 = norm1_w.reshape(depth, 1, d), norm2_w.reshape(depth, 1, d)
    qnw = jnp.tile(q_norm_w, (1, LANES // ATT_HD)).reshape(depth, 1, LANES)
    knw = jnp.tile(k_norm_w, (1, LANES // ATT_HD)).reshape(depth, 1, LANES)
    pw = pool_w.astype(BF16)
    psc = pool_scale.reshape(depth, 1, -1)
    merge_w_f32 = (w_ret_out, w_pool_out, w_attn_out, w_out, w_mlp1, w_mlp2)
    per_batch = lambda p: p.reshape(b, lc, p.shape[-1])
    ncb = lc // ATT_BLOCK

    for l in range(depth):
        last = l == depth - 1
        lat_row = lambda bi, l=l: l * rows + bi
        ctx_row = lambda bi, l=l: l * rows + b

        cparts = _inproj(xc, mod3, ctx_row, l, nw1, w_in_b, qnw, knw, None, tm=tmc, ctx_only=last)
        rk_c, rv_c = per_batch(cparts[0]), per_batch(cparts[1])
        ak_c = cparts[2].reshape(ATT_KV_HEADS, b, 2 * lc, LANES)
        av_c = cparts[3].reshape(ATT_KV_HEADS, b, ncb, LANES, 2 * ATT_BLOCK)

        rk, rv, ak, av, rq, rg, aq, pu, gt, *weights = _inproj(x, mod3, lat_row, l, nw1, w_in_b, qnw, knw, tables,
                                                               tm=tm_in, ctx_only=False, cast=merge_w_f32)
        if last:
            _, y_ret, _, y_pool = _retention_pool(log_gamma[l], (rk_c, rv_c), (rq, rk, rv, rg), (pu,),
                                                  l, pw, psc, ctx_out=False)
        else:
            rq_c, rg_c, pu_c, gt_c = [per_batch(cparts[i]) for i in (4, 5, 7, 8)]
            aq_c = cparts[6].reshape(b, ncb, ATT_KV_HEADS, LANES, 2 * ATT_BLOCK)
            y_ret_c, y_ret, y_pool_c, y_pool = _retention_pool(log_gamma[l], (rq_c, rk_c, rv_c, rg_c),
                                                               (rq, rk, rv, rg), (pu_c, pu), l, pw, psc, ctx_out=True)
        y_att = _attention(attn_sink[l], aq, ak, av, ak_c, av_c)
        x = _merge_mlp(x, y_ret, y_pool, y_att, gt, mod3, lat_row, l, nw2, weights, tm=tm)

        if not last:
            y_att_c = _attention(attn_sink[l], aq_c, None, None, ak_c, av_c)
            flat = lambda a: a.reshape(1, b * lc, a.shape[-1])
            xc = _merge_mlp(xc, flat(y_ret_c), flat(y_pool_c), flat(y_att_c), flat(gt_c), mod3, ctx_row, l, nw2,
                            weights, tm=tmc)
    return x
```

```python
import functools

import jax
import jax.numpy as jnp
from jax import lax
from jax.experimental import pallas as pl
from jax.experimental.pallas import tpu as pltpu

F32 = jnp.float32
BF16 = jnp.bfloat16

EPS = 1e-6
LOG2E = 1.4426950408889634
LANES = 128
RET_HEADS, RET_DK, RET_DV, RET_CHUNK = 4, 64, 128, 128
RET_SLABS = RET_HEADS // 2
RET_ROPE_BASE = 10000.0
POOL_GROUPS, POOL_GW = 4, 128
POOL_WINDOWS = (2, 4, 8, 16)
ATT_HEADS, ATT_KV_HEADS, ATT_HD = 8, 2, 64
ATT_WINDOW = ATT_BLOCK = 128
ROPE_BASE = 10000.0
GRID_W = 64
N_BRANCH = 3
CTX_SIDE_COLS = 1024
INPROJ_SUBTILE = 512
NEG_BIG = -1e30
VMEM_LIMIT = 62 * 1024 * 1024


def _cparams(n_axes):
    return pltpu.CompilerParams(dimension_semantics=("arbitrary",) * n_axes,
                                vmem_limit_bytes=VMEM_LIMIT)


def _layer_block(stacked, layer, block=None, resident=False):
    if layer is None:
        block = tuple(stacked.shape) if block is None else block
        shape, idx = block, (0,) * len(block)
    else:
        block = tuple(stacked.shape[1:]) if block is None else block
        shape, idx = (None,) + block, (layer,) + (0,) * len(block)
    return pl.BlockSpec(shape, lambda *_: idx, pipeline_mode=pl.Buffered(1) if resident else None)


def _sigmoid(x):
    return 1.0 / (1.0 + jnp.exp(-x))


def _mod_kernel(cc_ref, w_ref, b_ref, o_ref):
    cc = cc_ref[...]
    s = (cc * _sigmoid(cc)).astype(BF16)
    o_ref[0] = jnp.dot(s, w_ref[0].astype(BF16), preferred_element_type=F32) + b_ref[0]


def _modulation(cc, ada_w, ada_b):
    depth, d, n = ada_w.shape
    r = cc.shape[0]
    tn = 1024
    return pl.pallas_call(
        _mod_kernel,
        grid=(depth, n // tn),
        in_specs=[pl.BlockSpec((r, d), lambda l, j: (0, 0)),
                  pl.BlockSpec((1, d, tn), lambda l, j: (l, 0, j)),
                  pl.BlockSpec((1, 1, tn), lambda l, j: (l, 0, j))],
        out_specs=pl.BlockSpec((1, r, tn), lambda l, j: (l, 0, j)),
        out_shape=jax.ShapeDtypeStruct((depth, r, n), F32),
        compiler_params=_cparams(2),
        name="adaln_mod",
    )(cc, ada_w, ada_b.reshape(depth, 1, n))


def _swap_halves(x, half):
    lane = lax.broadcasted_iota(jnp.int32, x.shape, 1)
    first = (lane % (2 * half)) < half
    return jnp.where(first, pltpu.roll(x, LANES - half, 1), pltpu.roll(x, half, 1))


def _rope(x, cos, sin, half):
    return x * cos + _swap_halves(x, half) * sin


def _inproj_kernel(*refs, d, ctx_only, use_rope, n_cast):
    x_ref, mod_ref, nw_ref, w_ref, qnw_ref, knw_ref = refs[:6]
    pos = 6
    if use_rope:
        cr_ref, sr_ref, ca_ref, sa_ref = refs[6:10]
        pos = 10
    cast_in, refs_out = refs[pos:pos + n_cast], refs[pos + n_cast:]
    outs, cast_out = refs_out[:len(refs_out) - n_cast], refs_out[len(refs_out) - n_cast:]
    tm = x_ref.shape[1]

    for src, dst in zip(cast_in, cast_out):
        dst[...] = src[...].astype(BF16)

    mod = mod_ref[0]
    shift, scale = mod[:, 0:d], mod[:, d:2 * d]
    sub = min(tm, INPROJ_SUBTILE)
    lane = lax.broadcasted_iota(jnp.int32, (sub, LANES), 1)
    low = lane < ATT_HD

    def slab(a, s):
        return a[:, s * LANES:(s + 1) * LANES]

    def qk_norm(a, w):
        sq = a * a
        lo = jnp.sum(jnp.where(low, sq, 0.0), axis=-1, keepdims=True)
        hi = jnp.sum(jnp.where(low, 0.0, sq), axis=-1, keepdims=True)
        msq = jnp.where(low, lo, hi) * (1.0 / ATT_HD)
        return a * lax.rsqrt(msq + EPS) * w

    def subtile(r0):
        rows = slice(r0, r0 + sub)
        blk0 = r0 // ATT_BLOCK
        x = x_ref[0, rows]
        ms = jnp.mean(x * x, axis=-1, keepdims=True)
        u = ((x * lax.rsqrt(ms + EPS) * nw_ref[...]) * (1.0 + scale) + shift).astype(BF16)

        def proj(a, b):
            return jnp.dot(u, w_ref[:, a:b], preferred_element_type=F32)

        def ret_rope(a):
            return _rope(a, cr_ref[rows], sr_ref[rows], RET_DK // 2) if use_rope else a

        def att_rope(a):
            return _rope(a, ca_ref[rows], sa_ref[rows], ATT_HD // 4) if use_rope else a

        def store_kv(a, out_ref):
            sw = pltpu.roll(a, ATT_HD, 1)
            zero = jnp.zeros_like(a)
            lo_hi = ((jnp.where(low, a, zero), jnp.where(low, zero, sw)),
                     (jnp.where(low, sw, zero), jnp.where(low, zero, a)))
            for kh in range(ATT_KV_HEADS):
                for j in range(sub // ATT_BLOCK):
                    for e in range(2):
                        k0 = (2 * (blk0 + j) + e) * ATT_BLOCK
                        out_ref[kh, 0, k0:k0 + ATT_BLOCK, :] = \
                            lo_hi[kh][e][j * ATT_BLOCK:(j + 1) * ATT_BLOCK].astype(BF16)

        rk_ref, rv_ref, ak_ref, av_ref = outs[:4]
        a = proj(0, 256)
        for s in range(RET_SLABS):
            rk_ref[0, rows, s * LANES:(s + 1) * LANES] = (ret_rope(slab(a, s)) * RET_DK ** -0.5).astype(BF16)
        rv_ref[0, rows] = proj(256, 768).astype(BF16)
        a = proj(768, 1024)
        store_kv(att_rope(qk_norm(slab(a, 0), knw_ref[...])), ak_ref)
        vs = slab(a, 1)
        zq = jnp.zeros((ATT_HD, ATT_BLOCK), F32)
        for j in range(sub // ATT_BLOCK):
            vt = vs[j * ATT_BLOCK:(j + 1) * ATT_BLOCK].T
            for kh in range(ATT_KV_HEADS):
                top = vt[kh * ATT_HD:(kh + 1) * ATT_HD]
                tile = jnp.concatenate([jnp.concatenate([top, zq], axis=1), jnp.concatenate([zq, top], axis=1)],
                                       axis=0)
                av_ref[kh, 0, blk0 + j] = tile.astype(BF16)
        if ctx_only:
            return
        rq_ref, rg_ref, aq_ref, pu_ref, gt_ref = outs[4:]
        a = proj(1024, 1280)
        for s in range(RET_SLABS):
            rq_ref[0, rows, s * LANES:(s + 1) * LANES] = ret_rope(slab(a, s)).astype(BF16)
        g = proj(1280, 1792)
        rg_ref[0, rows] = (g * _sigmoid(g)).astype(BF16)
        a = proj(1792, 2304)
        for s in range(ATT_HEADS // 2):
            qs = att_rope(qk_norm(slab(a, s), qnw_ref[...])) * (ATT_HD ** -0.5 * LOG2E)
            for j in range(sub // ATT_BLOCK):
                aq_ref[0, blk0 + j, s // 2, :, (s % 2) * ATT_BLOCK:(s % 2 + 1) * ATT_BLOCK] = \
                    qs[j * ATT_BLOCK:(j + 1) * ATT_BLOCK].T.astype(BF16)
        pu_ref[0, rows] = proj(2304, 2816).astype(BF16)
        for s in range(6):
            gt_ref[0, rows, s * 512:(s + 1) * 512] = proj(2816 + s * 512, 2816 + (s + 1) * 512).astype(BF16)

    for r0 in range(0, tm, sub):
        subtile(r0)


def _inproj(x3, mod3, mod_row, layer, nw, w, qnw, knw, tables, *, tm, ctx_only, cast=()):
    bx, l, d = x3.shape
    ncol = CTX_SIDE_COLS if ctx_only else w.shape[2]
    use_rope = tables is not None
    grid = (bx, l // tm)
    row = lambda b, i: (b, i, 0)
    in_specs = [pl.BlockSpec((1, tm, d), row),
                pl.BlockSpec((1, 1, mod3.shape[2]), lambda b, i: (mod_row(b), 0, 0)),
                _layer_block(nw, layer),
                _layer_block(w, layer, (d, ncol), resident=True),
                _layer_block(qnw, layer),
                _layer_block(knw, layer)]
    args = [x3, mod3, nw, w, qnw, knw]
    if use_rope:
        in_specs += [pl.BlockSpec((tm, LANES), lambda b, i: (i, 0))] * 4
        args += list(tables)
    kv_spec = pl.BlockSpec((ATT_KV_HEADS, 1, 2 * tm, LANES), lambda b, i: (0, b, i, 0))
    kv_shape = jax.ShapeDtypeStruct((ATT_KV_HEADS, bx, 2 * l, LANES), BF16)
    tok = lambda wd: (pl.BlockSpec((1, tm, wd), row), jax.ShapeDtypeStruct((bx, l, wd), BF16))
    nblk, tblk = l // ATT_BLOCK, tm // ATT_BLOCK
    vt_spec = pl.BlockSpec((ATT_KV_HEADS, 1, tblk, LANES, 2 * ATT_BLOCK), lambda b, i: (0, b, i, 0, 0))
    vt_shape = jax.ShapeDtypeStruct((ATT_KV_HEADS, bx, nblk, LANES, 2 * ATT_BLOCK), BF16)
    qt_spec = pl.BlockSpec((1, tblk, ATT_KV_HEADS, LANES, 2 * ATT_BLOCK), lambda b, i: (b, i, 0, 0, 0))
    qt_shape = jax.ShapeDtypeStruct((bx, nblk, ATT_KV_HEADS, LANES, 2 * ATT_BLOCK), BF16)
    outs = [tok(256), tok(512), (kv_spec, kv_shape), (vt_spec, vt_shape)]
    if not ctx_only:
        outs += [tok(256), tok(512), (qt_spec, qt_shape), tok(512), tok(3072)]
    steps = bx * (l // tm)
    for wf in cast:
        k, n = wf.shape[1:]
        rps = max(16, k // steps)
        per = steps // (k // rps)
        in_specs.append(pl.BlockSpec((None, rps, n), lambda b, i, per=per: (layer, (b * (l // tm) + i) // per, 0)))
        outs.append((pl.BlockSpec((rps, n), lambda b, i, per=per: ((b * (l // tm) + i) // per, 0)),
                     jax.ShapeDtypeStruct((k, n), BF16)))
        args.append(wf)
    return pl.pallas_call(
        functools.partial(_inproj_kernel, d=d, ctx_only=ctx_only, use_rope=use_rope, n_cast=len(cast)),
        grid=grid, in_specs=in_specs, out_specs=[o[0] for o in outs], out_shape=[o[1] for o in outs],
        compiler_params=_cparams(2),
        name="inproj_ctx" if not use_rope else "inproj",
    )(*args)


def _ret_kernel(*refs, nc, n, ctx_out):
    lg_ref = refs[0]
    if ctx_out:
        qc_ref, kc_ref, vc_ref, gc_ref, q_ref, k_ref, v_ref, g_ref = refs[1:9]
        puc_ref, pu_ref, pw_ref, psc_ref, yc_ref, y_ref, ypc_ref, yp_ref = refs[9:17]
        scr = refs[17:]
    else:
        kc_ref, vc_ref, q_ref, k_ref, v_ref, g_ref = refs[1:7]
        pu_ref, pw_ref, psc_ref, y_ref, yp_ref = refs[7:12]
        qc_ref = gc_ref = yc_ref = puc_ref = ypc_ref = None
        scr = refs[12:]
    dmat_ref, zf_ref, zb_ref, xf_ref, xb_ref, gf_ref, gb_ref, rf_ref, rb_ref, st_ref, uf_ref, pa_ref, pb_ref = scr

    if ctx_out:
        _pool_tokens(puc_ref, pw_ref, psc_ref, ypc_ref, pa_ref, pb_ref)
    _pool_tokens(pu_ref, pw_ref, psc_ref, yp_ref, pa_ref, pb_ref)
    C = RET_CHUNK
    W2 = 2 * RET_DV

    ri = lax.broadcasted_iota(jnp.int32, (C, LANES), 0)
    ci = lax.broadcasted_iota(jnp.int32, (C, LANES), 1)
    low = ci < RET_DK
    r2 = lax.broadcasted_iota(jnp.int32, (C, W2), 0)
    c2 = lax.broadcasted_iota(jnp.int32, (C, W2), 1)
    even_col = c2 < RET_DV
    diag = (r2 < RET_DK) == even_col
    r2f = r2.astype(F32)
    rel = (r2 - (c2 % RET_DV)).astype(F32)
    for s in range(RET_SLABS):
        lgf2 = jnp.where(even_col, lg_ref[0, 2 * s], lg_ref[0, 2 * s + 1])
        lgb2 = jnp.where(even_col, lg_ref[1, 2 * s], lg_ref[1, 2 * s + 1])
        dmat_ref[s] = jnp.where(rel >= 0, jnp.exp(jnp.maximum(rel, 0.0) * lgf2), 0.0) \
            + jnp.where(rel <= 0, jnp.exp(jnp.maximum(-rel, 0.0) * lgb2), 0.0)
        zf_ref[s] = jnp.exp((C - 1 - r2f) * lgf2)
        zb_ref[s] = jnp.exp(r2f * lgb2)
        gf_ref[s] = jnp.exp(jnp.full((C, W2), C, F32) * lgf2)
        gb_ref[s] = jnp.exp(jnp.full((C, W2), C, F32) * lgb2)
        lgf1 = jnp.where(low, lg_ref[0, 2 * s], lg_ref[0, 2 * s + 1])
        lgb1 = jnp.where(low, lg_ref[1, 2 * s], lg_ref[1, 2 * s + 1])
        rif = ri.astype(F32)
        xf_ref[s] = jnp.exp((rif + 1.0) * lgf1)
        xb_ref[s] = jnp.exp((C - rif) * lgb1)
        rf_ref[s] = jnp.zeros((C, W2), F32)
        rb_ref[s] = jnp.zeros((C, W2), F32)

    def bwd_chunk(k_blk, v_blk, t):
        for s in range(RET_SLABS):
            st_ref[t, s, C:2 * C, :] = rb_ref[s].astype(BF16)
            vf = v_blk[:, s * W2:(s + 1) * W2].astype(F32)
            vz = jnp.concatenate([(vf * zf_ref[s]).astype(BF16), (vf * zb_ref[s]).astype(BF16)], axis=1)
            inc = lax.dot_general(k_blk[:, s * LANES:(s + 1) * LANES], vz, (((0,), (0,)), ((), ())),
                                  preferred_element_type=F32)
            uf_ref[t, s] = jnp.where(diag, inc[:, 0:W2], 0.0)
            rb_ref[s] = gb_ref[s] * rb_ref[s] + jnp.where(diag, inc[:, W2:2 * W2], 0.0)

    def fwd_chunk(q_blk, k_blk, v_blk, g_blk, t, store):
        for s in range(RET_SLABS):
            qs = q_blk[:, s * LANES:(s + 1) * LANES]
            ks = k_blk[:, s * LANES:(s + 1) * LANES]
            v2 = v_blk[:, s * W2:(s + 1) * W2]
            zk = jnp.zeros_like(ks)
            k2 = jnp.concatenate([jnp.where(low, ks, zk), jnp.where(low, zk, ks)], axis=0)
            sc = lax.dot_general(qs, k2, (((1,), (1,)), ((), ())), preferred_element_type=F32) * dmat_ref[s]
            zv = jnp.zeros_like(v2)
            vv = jnp.concatenate([jnp.where(even_col, v2, zv), jnp.where(even_col, zv, v2)], axis=0)
            y2 = jnp.dot(sc.astype(BF16), vv, preferred_element_type=F32)
            qf = qs.astype(F32)
            qx = jnp.concatenate([(qf * xf_ref[s]).astype(BF16), (qf * xb_ref[s]).astype(BF16)], axis=1)
            st_ref[t, s, 0:C, :] = rf_ref[s].astype(BF16)
            y2 = y2 + jnp.dot(qx, st_ref[t, s], preferred_element_type=F32)
            if store is not None:
                for e in range(2):
                    h = 2 * s + e
                    y = y2[:, e * RET_DV:(e + 1) * RET_DV]
                    gh = g_blk[:, h * RET_DV:(h + 1) * RET_DV].astype(F32)
                    yn = y * lax.rsqrt(jnp.mean(y * y, axis=-1, keepdims=True) + EPS)
                    store(h, (gh * yn).astype(BF16))
            rf_ref[s] = gf_ref[s] * rf_ref[s] + uf_ref[t, s]

    chunk = lambda t: slice(t * C, (t + 1) * C)

    for t in reversed(range(nc)):
        bwd_chunk(kc_ref[0, chunk(t), :], vc_ref[0, chunk(t), :], t)
    for j in reversed(range(n)):
        bwd_chunk(k_ref[0, chunk(j), :], v_ref[0, chunk(j), :], nc + j)

    def store_to(out_ref, sl):
        def store(h, val):
            out_ref[0, sl, h * RET_DV:(h + 1) * RET_DV] = val
        return store

    for t in range(nc):
        sl = chunk(t)
        if ctx_out:
            fwd_chunk(qc_ref[0, sl, :], kc_ref[0, sl, :], vc_ref[0, sl, :], gc_ref[0, sl, :], t, store_to(yc_ref, sl))
        else:
            for s in range(RET_SLABS):
                rf_ref[s] = gf_ref[s] * rf_ref[s] + uf_ref[t, s]
    for j in range(n):
        sl = chunk(j)
        fwd_chunk(q_ref[0, sl, :], k_ref[0, sl, :], v_ref[0, sl, :], g_ref[0, sl, :], nc + j, store_to(y_ref, sl))


def _retention_pool(log_gamma, ctx_parts, lat_parts, pool_parts, layer, pool_w, pool_scale, *, ctx_out):
    b, s, _ = lat_parts[0].shape
    lc = ctx_parts[-1].shape[1]
    nc, n = lc // RET_CHUNK, s // RET_CHUNK
    full = lambda a: pl.BlockSpec((1,) + a.shape[1:], lambda i: (i, 0, 0))
    in_specs = [pl.BlockSpec(memory_space=pltpu.SMEM)] + [full(a) for a in (*ctx_parts, *lat_parts, *pool_parts)] \
        + [_layer_block(pool_w, layer), _layer_block(pool_scale, layer)]
    vw = RET_HEADS * RET_DV
    lens = ([lc] if ctx_out else []) + [s]
    out_shape = [jax.ShapeDtypeStruct((b, ln, vw), BF16) for ln in lens] \
        + [jax.ShapeDtypeStruct((b, ln, POOL_GROUPS * POOL_GW), BF16) for ln in lens]
    out_specs = [full(o) for o in out_shape]
    wide = (RET_SLABS, RET_CHUNK, 2 * RET_DV)
    narrow = (RET_SLABS, RET_CHUNK, LANES)
    scratch = [pltpu.VMEM(wide, F32),
               pltpu.VMEM(wide, F32), pltpu.VMEM(wide, F32),
               pltpu.VMEM(narrow, F32), pltpu.VMEM(narrow, F32),
               pltpu.VMEM(wide, F32), pltpu.VMEM(wide, F32),
               pltpu.VMEM(wide, F32), pltpu.VMEM(wide, F32),
               pltpu.VMEM((nc + n, RET_SLABS, 2 * RET_CHUNK, 2 * RET_DV), BF16),
               pltpu.VMEM((nc + n, RET_SLABS, RET_CHUNK, 2 * RET_DV), F32),
               pltpu.VMEM((s + 32, LANES), F32), pltpu.VMEM((s + 32, LANES), F32)]
    res = pl.pallas_call(
        functools.partial(_ret_kernel, nc=nc, n=n, ctx_out=ctx_out),
        grid=(b,), in_specs=in_specs, out_specs=out_specs, out_shape=out_shape,
        scratch_shapes=scratch, compiler_params=_cparams(1),
        name="retention_pool",
    )(log_gamma, *ctx_parts, *lat_parts, *pool_parts, pool_w, pool_scale)
    return res if ctx_out else (None, res[0], None, res[1])


def _pool_tokens(u_ref, w_ref, sc_ref, o_ref, pa_ref, pb_ref):
    l = u_ref.shape[1]
    pad = 16
    t8 = lax.broadcasted_iota(jnp.int32, (8, LANES), 0)
    zeros_edge = jnp.zeros((pad, LANES), F32)
    ext = l + 16
    for g, w in enumerate(POOL_WINDOWS):
        ug = u_ref[0, :, g * POOL_GW:(g + 1) * POOL_GW].astype(F32)
        for buf in (pa_ref, pb_ref):
            buf[0:pad, :] = zeros_edge
            buf[pad + l:pad + l + pad, :] = zeros_edge
        pa_ref[pad:pad + l, :] = ug
        pb_ref[8:8 + ext, :] = pa_ref[7:7 + ext, :] + pa_ref[8:8 + ext, :]
        src, dst = pb_ref, pa_ref
        step = 1
        while 2 * step < w:
            dst[8:8 + ext, :] = src[8 - step:8 - step + ext, :] + src[8 + step:8 + step + ext, :]
            src, dst = dst, src
            step *= 2
        hw = w // 2
        top, bot = slice(pad, pad + 8), slice(pad + l - 8, pad + l)
        src[top, :] = src[top, :] * jnp.where(t8 < hw, w / (t8 + hw).astype(F32), 1.0)
        tb = l - 8 + t8
        src[bot, :] = src[bot, :] * jnp.where(tb + hw > l, w / (l - tb + hw).astype(F32), 1.0)
        pooled = src[pad:pad + l, :] * (1.0 / w)
        mixed = jnp.dot((pooled - ug).astype(BF16), w_ref[g], preferred_element_type=F32)
        o_ref[0, :, g * POOL_GW:(g + 1) * POOL_GW] = (mixed * sc_ref[:, g * POOL_GW:(g + 1) * POOL_GW]).astype(BF16)


def _attn_kernel(*refs, local, nq, ncb):
    sink_ref = refs[0]
    if local:
        q_ref, k_ref, v_ref, kc_ref, vc_ref, o_ref, pall_ref = refs[1:8]
        slot_a, slot_b = refs[8:10], refs[10:12]
    else:
        q_ref, kc_ref, vc_ref, o_ref, pall_ref = refs[1:6]
        slot_a = refs[6:8]
        k_ref = v_ref = None
    BLK = ATT_BLOCK
    KB = 2 * BLK
    nloc = 3 if local else 0
    kj = lax.broadcasted_iota(jnp.int32, (BLK, 2 * BLK), 0)
    qi = lax.broadcasted_iota(jnp.int32, (BLK, 2 * BLK), 1) % BLK
    tri_prev = jnp.where(kj >= qi, 0.0, NEG_BIG)
    tri_next = jnp.where(kj <= qi, 0.0, NEG_BIG)
    head_row = lax.broadcasted_iota(jnp.int32, (LANES, BLK), 0) < ATT_HD
    nkeys = (nloc + ncb) * KB
    orow = lax.broadcasted_iota(jnp.int32, (16, nkeys), 0)
    ocol = lax.broadcasted_iota(jnp.int32, (16, nkeys), 1)
    ones_rows = jnp.where(orow == (ocol % KB) // BLK, 1.0, 0.0).astype(BF16)

    def neighbours(nb):
        if not local:
            return []
        return [(blk, bias) for blk, bias in ((nb - 1, tri_prev), (nb, None), (nb + 1, tri_next)) if 0 <= blk < nq]

    def scores(nb, slot):
        s_ref, m_ref = slot
        near = neighbours(nb)
        for kh in range(ATT_KV_HEADS):
            qt = q_ref[0, nb, kh]
            kcat = jnp.concatenate([k_ref[kh, 0, blk * KB:(blk + 1) * KB, :] for blk, _ in near] + [kc_ref[kh, 0]],
                                   axis=0)
            sc = jnp.dot(kcat, qt, preferred_element_type=F32)
            mx = [None, None]
            for j in range(len(near) + ncb):
                for e in range(2):
                    krow = slice((2 * j + e) * BLK, (2 * j + e + 1) * BLK)
                    c = sc[krow]
                    if j < len(near) and near[j][1] is not None:
                        c = c + near[j][1]
                    s_ref[kh, krow, :] = c
                    cm = jnp.max(c.reshape(BLK // 8, 8, 2 * BLK), axis=0)
                    mx[e] = cm if mx[e] is None else jnp.maximum(mx[e], cm)
            for e in range(2):
                m_ref[kh, e] = mx[e]

    def softmax_pv(nb, slot):
        s_ref, m_ref = slot
        p_ref = pall_ref.at[nb]
        rows = slice(nb * BLK, (nb + 1) * BLK)
        near = neighbours(nb)
        nkb = len(near) + ncb
        for kh in range(ATT_KV_HEADS):
            sink_term = [[None, None], [None, None]]
            for s in range(2):
                scols = slice(s * BLK, (s + 1) * BLK)
                for e in range(2):
                    sink = sink_ref[kh * 4 + 2 * s + e] * LOG2E
                    m = jnp.maximum(jnp.max(m_ref[kh, e, :, scols], axis=0, keepdims=True), sink)
                    for j in range(nkb):
                        kr = slice((2 * j + e) * BLK, (2 * j + e + 1) * BLK)
                        p_ref[kh, kr, scols] = jnp.exp2(s_ref[kh, kr, scols] - m).astype(BF16)
                    sink_term[s][e] = jnp.exp2(sink - m)
            vcat = jnp.concatenate([v_ref[kh, 0, blk] for blk, _ in near]
                                   + [vc_ref[kh, 0, j] for j in range(ncb)], axis=1)
            ot = jnp.dot(jnp.concatenate([vcat, ones_rows[:, 0:nkb * KB]], axis=0), p_ref[kh, 0:nkb * KB, :],
                         preferred_element_type=F32)
            for s in range(2):
                scols = slice(s * BLK, (s + 1) * BLK)
                inv = [1.0 / (ot[LANES + e:LANES + e + 1, scols] + sink_term[s][e]) for e in range(2)]
                scale = jnp.where(head_row, inv[0], inv[1])
                o_ref[0, rows, (2 * kh + s) * LANES:(2 * kh + s + 1) * LANES] = \
                    (ot[0:LANES, scols] * scale).T.astype(BF16)

    if not local:
        for nblk in range(nq):
            scores(nblk, slot_a)
            softmax_pv(nblk, slot_a)
        return

    scores(0, slot_a)
    for nblk in range(nq):
        cur, nxt = (slot_a, slot_b) if nblk % 2 == 0 else (slot_b, slot_a)
        if nblk + 1 < nq:
            scores(nblk + 1, nxt)
        softmax_pv(nblk, cur)


def _attention(sink, q, k, v, kc, vc):
    b, nq = q.shape[:2]
    l = nq * ATT_BLOCK
    qw = ATT_HEADS * ATT_HD
    local = k is not None
    ncb = vc.shape[2]
    tok = pl.BlockSpec((1, l, qw), lambda i: (i, 0, 0))
    lead = lambda a: pl.BlockSpec((1,) + a.shape[1:], lambda i: (i,) + (0,) * (a.ndim - 1))
    second = lambda a: pl.BlockSpec((a.shape[0], 1) + a.shape[2:], lambda i: (0, i) + (0,) * (a.ndim - 2))
    kvs = ([k, v] if local else []) + [kc, vc]
    args = [sink, q] + kvs
    in_specs = [pl.BlockSpec(memory_space=pltpu.SMEM), lead(q)] + [second(a) for a in kvs]
    nkb = (3 if local else 0) + ncb
    tile = (ATT_KV_HEADS, nkb * 2 * ATT_BLOCK, 2 * ATT_BLOCK)
    slot = [pltpu.VMEM(tile, F32), pltpu.VMEM((ATT_KV_HEADS, 2, 8, 2 * ATT_BLOCK), F32)]
    scratch = [pltpu.VMEM((nq,) + tile, BF16)] + slot * (2 if local else 1)
    return pl.pallas_call(
        functools.partial(_attn_kernel, local=local, nq=nq, ncb=ncb),
        grid=(b,), in_specs=in_specs, out_specs=tok,
        out_shape=jax.ShapeDtypeStruct((b, l, qw), BF16),
        scratch_shapes=scratch,
        compiler_params=_cparams(1),
        name="attn_window" if local else "attn_ctx",
    )(*args)


def _merge_mlp_kernel(x_ref, yr_ref, yp_ref, ya_ref, gt_ref, mod_ref, nw_ref,
                      wr_ref, wp_ref, wa_ref, wo_ref, w1_ref, w2_ref, o_ref, *, d, hid_chunk):
    mod = mod_ref[0]
    g1, sh2, sc2, g2 = mod[:, 2 * d:3 * d], mod[:, 3 * d:4 * d], mod[:, 4 * d:5 * d], mod[:, 5 * d:6 * d]
    y = None
    for i, (y_ref, w_ref) in enumerate(((yr_ref, wr_ref), (yp_ref, wp_ref), (ya_ref, wa_ref))):
        proj = jnp.dot(y_ref[0], w_ref[...], preferred_element_type=F32)
        term = _sigmoid(gt_ref[0, :, i * d:(i + 1) * d].astype(F32)) * proj
        y = term if y is None else y + term
    mix = jnp.dot(y.astype(BF16), wo_ref[...], preferred_element_type=F32)
    x1 = x_ref[0] + g1 * mix
    ms = jnp.mean(x1 * x1, axis=-1, keepdims=True)
    h = ((x1 * lax.rsqrt(ms + EPS) * nw_ref[...]) * (1.0 + sc2) + sh2).astype(BF16)
    acc = None
    hidden = w1_ref.shape[1]
    for c in range(hidden // hid_chunk):
        a = jnp.dot(h, w1_ref[:, c * hid_chunk:(c + 1) * hid_chunk], preferred_element_type=F32)
        a = jnp.maximum(a, 0.0)
        part = jnp.dot((a * a).astype(BF16), w2_ref[c * hid_chunk:(c + 1) * hid_chunk, :], preferred_element_type=F32)
        acc = part if acc is None else acc + part
    o_ref[0] = x1 + g2 * acc


def _merge_mlp(x3, y_ret, y_pool, y_att, gates, mod3, mod_row, layer, nw, weights, *, tm):
    bx, l, d = x3.shape
    row = lambda b, i: (b, i, 0)
    bw = y_ret.shape[2]
    in_specs = [pl.BlockSpec((1, tm, d), row)] + [pl.BlockSpec((1, tm, bw), row)] * 3 + [
        pl.BlockSpec((1, tm, N_BRANCH * d), row),
        pl.BlockSpec((1, 1, mod3.shape[2]), lambda b, i: (mod_row(b), 0, 0)),
        _layer_block(nw, layer)] + [_layer_block(w, None, resident=True) for w in weights]
    return pl.pallas_call(
        functools.partial(_merge_mlp_kernel, d=d, hid_chunk=1024),
        grid=(bx, l // tm), in_specs=in_specs, out_specs=pl.BlockSpec((1, tm, d), row),
        out_shape=jax.ShapeDtypeStruct((bx, l, d), F32),
        compiler_params=_cparams(2),
        name="merge_mlp",
    )(x3, y_ret, y_pool, y_att, gates, mod3, nw, *weights)


def _rope_tables(s):
    pos = jnp.arange(s, dtype=F32)
    half = RET_DK // 2
    ang = pos[:, None] * (RET_ROPE_BASE ** (-jnp.arange(half, dtype=F32) / half))[None, :]
    cos_r = jnp.tile(jnp.concatenate([jnp.cos(ang), jnp.cos(ang)], axis=1), (1, LANES // RET_DK))
    sin_r = jnp.tile(jnp.concatenate([-jnp.sin(ang), jnp.sin(ang)], axis=1), (1, LANES // RET_DK))
    q = ATT_HD // 4
    freq = ROPE_BASE ** (-jnp.arange(q, dtype=F32) / q)
    ar = (jnp.arange(s) // GRID_W).astype(F32)[:, None] * freq[None, :]
    ac = (jnp.arange(s) % GRID_W).astype(F32)[:, None] * freq[None, :]
    cos_a = jnp.concatenate([jnp.cos(ar), jnp.cos(ar), jnp.cos(ac), jnp.cos(ac)], axis=1)
    sin_a = jnp.concatenate([-jnp.sin(ar), jnp.sin(ar), -jnp.sin(ac), jnp.sin(ac)], axis=1)
    return cos_r, sin_r, jnp.tile(cos_a, (1, LANES // ATT_HD)), jnp.tile(sin_a, (1, LANES // ATT_HD))


def kernel(x, c, ctx, c_ctx, norm1_w, norm2_w, ada_w, ada_b, w_in, ret_decay, pool_w, pool_scale, q_norm_w, k_norm_w, attn_sink, w_ret_out, w_pool_out, w_attn_out, w_out, w_mlp1, w_mlp2):
    b, s, d = x.shape
    lc = ctx.shape[1]
    depth = w_in.shape[0]
    tm = 512 if s % 512 == 0 else 256
    tmc = 512 if (b * lc) % 512 == 0 else 256
    tm_in = 2 * INPROJ_SUBTILE if s % (2 * INPROJ_SUBTILE) == 0 else tm

    rows = -(-(b + 1) // 8) * 8
    cc = jnp.zeros((rows, d), F32).at[:b].set(c).at[b].set(c_ctx)
    mod3 = _modulation(cc, ada_w, ada_b).reshape(depth * rows, 1, 6 * d)

    tables = _rope_tables(s)
    xc = ctx.reshape(1, b * lc, d)

    log_gamma = jnp.log1p(-jnp.exp2(-ret_decay.astype(F32)))
    w_in_b = w_in.astype(BF16)
    nw1, nw2 = norm1_w.reshape(depth, 1, d), norm2_w.reshape(depth, 1, d)
    qnw = jnp.tile(q_norm_w, (1, LANES // ATT_HD)).reshape(depth, 1, LANES)
    knw = jnp.tile(k_norm_w, (1, LANES // ATT_HD)).reshape(depth, 1, LANES)
    pw = pool_w.astype(BF16)
    psc = pool_scale.reshape(depth, 1, -1)
    merge_w_f32 = (w_ret_out, w_pool_out, w_attn_out, w_out, w_mlp1, w_mlp2)
    per_batch = lambda p: p.reshape(b, lc, p.shape[-1])
    ncb = lc // ATT_BLOCK

    for l in range(depth):
        last = l == depth - 1
        lat_row = lambda bi, l=l: l * rows + bi
        ctx_row = lambda bi, l=l: l * rows + b

        cparts = _inproj(xc, mod3, ctx_row, l, nw1, w_in_b, qnw, knw, None, tm=tmc, ctx_only=last)
        rk_c, rv_c = per_batch(cparts[0]), per_batch(cparts[1])
        ak_c = cparts[2].reshape(ATT_KV_HEADS, b, 2 * lc, LANES)
        av_c = cparts[3].reshape(ATT_KV_HEADS, b, ncb, LANES, 2 * ATT_BLOCK)

        rk, rv, ak, av, rq, rg, aq, pu, gt, *weights = _inproj(x, mod3, lat_row, l, nw1, w_in_b, qnw, knw, tables,
                                                               tm=tm_in, ctx_only=False, cast=merge_w_f32)
        if last:
            _, y_ret, _, y_pool = _retention_pool(log_gamma[l], (rk_c, rv_c), (rq, rk, rv, rg), (pu,),
                                                  l, pw, psc, ctx_out=False)
        else:
            rq_c, rg_c, pu_c, gt_c = [per_batch(cparts[i]) for i in (4, 5, 7, 8)]
            aq_c = cparts[6].reshape(b, ncb, ATT_KV_HEADS, LANES, 2 * ATT_BLOCK)
            y_ret_c, y_ret, y_pool_c, y_pool = _retention_pool(log_gamma[l], (rq_c, rk_c, rv_c, rg_c),
                                                               (rq, rk, rv, rg), (pu_c, pu), l, pw, psc, ctx_out=True)
        y_att = _attention(attn_sink[l], aq, ak, av, ak_c, av_c)
        x = _merge_mlp(x, y_ret, y_pool, y_att, gt, mod3, lat_row, l, nw2, weights, tm=tm)

        if not last:
            y_att_c = _attention(attn_sink[l], aq_c, None, None, ak_c, av_c)
            flat = lambda a: a.reshape(1, b * lc, a.shape[-1])
            xc = _merge_mlp(xc, flat(y_ret_c), flat(y_pool_c), flat(y_att_c), flat(gt_c), mod3, ctx_row, l, nw2,
                            weights, tm=tmc)
    return x
```

```python
import functools

import jax
import jax.numpy as jnp
from jax import lax
from jax.experimental import pallas as pl
from jax.experimental.pallas import tpu as pltpu

F32 = jnp.float32
BF16 = jnp.bfloat16

EPS = 1e-6
LOG2E = 1.4426950408889634
LANES = 128
RET_HEADS, RET_DK, RET_DV, RET_CHUNK = 4, 64, 128, 128
RET_SLABS = RET_HEADS // 2
RET_ROPE_BASE = 10000.0
POOL_GROUPS, POOL_GW = 4, 128
POOL_WINDOWS = (2, 4, 8, 16)
ATT_HEADS, ATT_KV_HEADS, ATT_HD = 8, 2, 64
ATT_WINDOW = ATT_BLOCK = 128
ROPE_BASE = 10000.0
GRID_W = 64
N_BRANCH = 3
CTX_SIDE_COLS = 1024
INPROJ_SUBTILE = 512
NEG_BIG = -1e30
VMEM_LIMIT = 62 * 1024 * 1024


def _cparams(n_axes):
    return pltpu.CompilerParams(dimension_semantics=("arbitrary",) * n_axes,
                                vmem_limit_bytes=VMEM_LIMIT)


def _layer_block(stacked, layer, block=None, resident=False):
    if layer is None:
        block = tuple(stacked.shape) if block is None else block
        shape, idx = block, (0,) * len(block)
    else:
        block = tuple(stacked.shape[1:]) if block is None else block
        shape, idx = (None,) + block, (layer,) + (0,) * len(block)
    return pl.BlockSpec(shape, lambda *_: idx, pipeline_mode=pl.Buffered(1) if resident else None)


def _sigmoid(x):
    return 0.5 * jnp.tanh(0.5 * x) + 0.5


def _mod_kernel(cc_ref, w_ref, b_ref, o_ref):
    cc = cc_ref[...]
    s = (cc * _sigmoid(cc)).astype(BF16)
    o_ref[0] = jnp.dot(s, w_ref[0].astype(BF16), preferred_element_type=F32) + b_ref[0]


def _modulation(cc, ada_w, ada_b):
    depth, d, n = ada_w.shape
    r = cc.shape[0]
    tn = 1024
    return pl.pallas_call(
        _mod_kernel,
        grid=(depth, n // tn),
        in_specs=[pl.BlockSpec((r, d), lambda l, j: (0, 0)),
                  pl.BlockSpec((1, d, tn), lambda l, j: (l, 0, j)),
                  pl.BlockSpec((1, 1, tn), lambda l, j: (l, 0, j))],
        out_specs=pl.BlockSpec((1, r, tn), lambda l, j: (l, 0, j)),
        out_shape=jax.ShapeDtypeStruct((depth, r, n), F32),
        compiler_params=_cparams(2),
        name="adaln_mod",
    )(cc, ada_w, ada_b.reshape(depth, 1, n))


def _swap_halves(x, half):
    lane = lax.broadcasted_iota(jnp.int32, x.shape, 1)
    first = (lane % (2 * half)) < half
    return jnp.where(first, pltpu.roll(x, LANES - half, 1), pltpu.roll(x, half, 1))


def _rope(x, cos, sin, half):
    return x * cos + _swap_halves(x, half) * sin


def _inproj_kernel(*refs, d, ctx_only, use_rope, n_cast):
    x_ref, mod_ref, nw_ref, w_ref, qnw_ref, knw_ref = refs[:6]
    pos = 6
    if use_rope:
        cr_ref, sr_ref, ca_ref, sa_ref = refs[6:10]
        pos = 10
    cast_in, refs_out = refs[pos:pos + n_cast], refs[pos + n_cast:]
    outs, cast_out = refs_out[:len(refs_out) - n_cast], refs_out[len(refs_out) - n_cast:]
    tm = x_ref.shape[1]

    for src, dst in zip(cast_in, cast_out):
        dst[...] = src[...].astype(BF16)

    mod = mod_ref[0]
    shift, scale = mod[:, 0:d], mod[:, d:2 * d]
    sub = min(tm, INPROJ_SUBTILE)
    lane = lax.broadcasted_iota(jnp.int32, (sub, LANES), 1)
    low = lane < ATT_HD

    def slab(a, s):
        return a[:, s * LANES:(s + 1) * LANES]

    def qk_norm(a, w):
        sq = a * a
        lo = jnp.sum(jnp.where(low, sq, 0.0), axis=-1, keepdims=True)
        hi = jnp.sum(jnp.where(low, 0.0, sq), axis=-1, keepdims=True)
        msq = jnp.where(low, lo, hi) * (1.0 / ATT_HD)
        return a * lax.rsqrt(msq + EPS) * w

    def subtile(r0):
        rows = slice(r0, r0 + sub)
        blk0 = r0 // ATT_BLOCK
        x = x_ref[0, rows]
        ms = jnp.mean(x * x, axis=-1, keepdims=True)
        u = ((x * lax.rsqrt(ms + EPS) * nw_ref[...]) * (1.0 + scale) + shift).astype(BF16)

        def proj(a, b):
            return jnp.dot(u, w_ref[:, a:b], preferred_element_type=F32)

        def ret_rope(a):
            return _rope(a, cr_ref[rows], sr_ref[rows], RET_DK // 2) if use_rope else a

        def att_rope(a):
            return _rope(a, ca_ref[rows], sa_ref[rows], ATT_HD // 4) if use_rope else a

        def store_kv(a, out_ref):
            sw = pltpu.roll(a, ATT_HD, 1)
            zero = jnp.zeros_like(a)
            lo_hi = ((jnp.where(low, a, zero), jnp.where(low, zero, sw)),
                     (jnp.where(low, sw, zero), jnp.where(low, zero, a)))
            for kh in range(ATT_KV_HEADS):
                for j in range(sub // ATT_BLOCK):
                    for e in range(2):
                        k0 = (2 * (blk0 + j) + e) * ATT_BLOCK
                        out_ref[kh, 0, k0:k0 + ATT_BLOCK, :] = \
                            lo_hi[kh][e][j * ATT_BLOCK:(j + 1) * ATT_BLOCK].astype(BF16)

        rk_ref, rv_ref, ak_ref, av_ref = outs[:4]
        a = proj(0, 256)
        for s in range(RET_SLABS):
            rk_ref[0, rows, s * LANES:(s + 1) * LANES] = (ret_rope(slab(a, s)) * RET_DK ** -0.5).astype(BF16)
        rv_ref[0, rows] = proj(256, 768).astype(BF16)
        a = proj(768, 1024)
        store_kv(att_rope(qk_norm(slab(a, 0), knw_ref[...])), ak_ref)
        vs = slab(a, 1)
        zq = jnp.zeros((ATT_HD, ATT_BLOCK), F32)
        for j in range(sub // ATT_BLOCK):
            vt = vs[j * ATT_BLOCK:(j + 1) * ATT_BLOCK].T
            for kh in range(ATT_KV_HEADS):
                top = vt[kh * ATT_HD:(kh + 1) * ATT_HD]
                tile = jnp.concatenate([jnp.concatenate([top, zq], axis=1), jnp.concatenate([zq, top], axis=1)],
                                       axis=0)
                av_ref[kh, 0, blk0 + j] = tile.astype(BF16)
        if ctx_only:
            return
        rq_ref, rg_ref, aq_ref, pu_ref, gt_ref = outs[4:]
        a = proj(1024, 1280)
        for s in range(RET_SLABS):
            rq_ref[0, rows, s * LANES:(s + 1) * LANES] = ret_rope(slab(a, s)).astype(BF16)
        g = proj(1280, 1792)
        rg_ref[0, rows] = (g * _sigmoid(g)).astype(BF16)
        a = proj(1792, 2304)
        for s in range(ATT_HEADS // 2):
            qs = att_rope(qk_norm(slab(a, s), qnw_ref[...])) * (ATT_HD ** -0.5 * LOG2E)
            for j in range(sub // ATT_BLOCK):
                aq_ref[0, blk0 + j, s // 2, :, (s % 2) * ATT_BLOCK:(s % 2 + 1) * ATT_BLOCK] = \
                    qs[j * ATT_BLOCK:(j + 1) * ATT_BLOCK].T.astype(BF16)
        pu_ref[0, rows] = proj(2304, 2816).astype(BF16)
        for s in range(6):
            gt_ref[0, rows, s * 512:(s + 1) * 512] = proj(2816 + s * 512, 2816 + (s + 1) * 512).astype(BF16)

    for r0 in range(0, tm, sub):
        subtile(r0)


def _inproj(x3, mod3, mod_row, layer, nw, w, qnw, knw, tables, *, tm, ctx_only, cast=()):
    bx, l, d = x3.shape
    ncol = CTX_SIDE_COLS if ctx_only else w.shape[2]
    use_rope = tables is not None
    grid = (bx, l // tm)
    row = lambda b, i: (b, i, 0)
    in_specs = [pl.BlockSpec((1, tm, d), row),
                pl.BlockSpec((1, 1, mod3.shape[2]), lambda b, i: (mod_row(b), 0, 0)),
                _layer_block(nw, layer),
                _layer_block(w, layer, (d, ncol), resident=True),
                _layer_block(qnw, layer),
                _layer_block(knw, layer)]
    args = [x3, mod3, nw, w, qnw, knw]
    if use_rope:
        in_specs += [pl.BlockSpec((tm, LANES), lambda b, i: (i, 0))] * 4
        args += list(tables)
    kv_spec = pl.BlockSpec((ATT_KV_HEADS, 1, 2 * tm, LANES), lambda b, i: (0, b, i, 0))
    kv_shape = jax.ShapeDtypeStruct((ATT_KV_HEADS, bx, 2 * l, LANES), BF16)
    tok = lambda wd: (pl.BlockSpec((1, tm, wd), row), jax.ShapeDtypeStruct((bx, l, wd), BF16))
    nblk, tblk = l // ATT_BLOCK, tm // ATT_BLOCK
    vt_spec = pl.BlockSpec((ATT_KV_HEADS, 1, tblk, LANES, 2 * ATT_BLOCK), lambda b, i: (0, b, i, 0, 0))
    vt_shape = jax.ShapeDtypeStruct((ATT_KV_HEADS, bx, nblk, LANES, 2 * ATT_BLOCK), BF16)
    qt_spec = pl.BlockSpec((1, tblk, ATT_KV_HEADS, LANES, 2 * ATT_BLOCK), lambda b, i: (b, i, 0, 0, 0))
    qt_shape = jax.ShapeDtypeStruct((bx, nblk, ATT_KV_HEADS, LANES, 2 * ATT_BLOCK), BF16)
    outs = [tok(256), tok(512), (kv_spec, kv_shape), (vt_spec, vt_shape)]
    if not ctx_only:
        outs += [tok(256), tok(512), (qt_spec, qt_shape), tok(512), tok(3072)]
    steps = bx * (l // tm)
    for wf in cast:
        k, n = wf.shape[1:]
        rps = max(16, k // steps)
        per = steps // (k // rps)
        in_specs.append(pl.BlockSpec((None, rps, n), lambda b, i, per=per: (layer, (b * (l // tm) + i) // per, 0)))
        outs.append((pl.BlockSpec((rps, n), lambda b, i, per=per: ((b * (l // tm) + i) // per, 0)),
                     jax.ShapeDtypeStruct((k, n), BF16)))
        args.append(wf)
    return pl.pallas_call(
        functools.partial(_inproj_kernel, d=d, ctx_only=ctx_only, use_rope=use_rope, n_cast=len(cast)),
        grid=grid, in_specs=in_specs, out_specs=[o[0] for o in outs], out_shape=[o[1] for o in outs],
        compiler_params=_cparams(2),
        name="inproj_ctx" if not use_rope else "inproj",
    )(*args)


def _ret_kernel(*refs, nc, n, ctx_out):
    lg_ref = refs[0]
    if ctx_out:
        qc_ref, kc_ref, vc_ref, gc_ref, q_ref, k_ref, v_ref, g_ref = refs[1:9]
        puc_ref, pu_ref, pw_ref, psc_ref, yc_ref, y_ref, ypc_ref, yp_ref = refs[9:17]
        scr = refs[17:]
    else:
        kc_ref, vc_ref, q_ref, k_ref, v_ref, g_ref = refs[1:7]
        pu_ref, pw_ref, psc_ref, y_ref, yp_ref = refs[7:12]
        qc_ref = gc_ref = yc_ref = puc_ref = ypc_ref = None
        scr = refs[12:]
    dmat_ref, zf_ref, zb_ref, xf_ref, xb_ref, gf_ref, gb_ref, rf_ref, rb_ref, st_ref, uf_ref, pa_ref, pb_ref = scr

    if ctx_out:
        _pool_tokens(puc_ref, pw_ref, psc_ref, ypc_ref, pa_ref, pb_ref)
    _pool_tokens(pu_ref, pw_ref, psc_ref, yp_ref, pa_ref, pb_ref)
    C = RET_CHUNK
    W2 = 2 * RET_DV

    ri = lax.broadcasted_iota(jnp.int32, (C, LANES), 0)
    ci = lax.broadcasted_iota(jnp.int32, (C, LANES), 1)
    low = ci < RET_DK
    r2 = lax.broadcasted_iota(jnp.int32, (C, W2), 0)
    c2 = lax.broadcasted_iota(jnp.int32, (C, W2), 1)
    even_col = c2 < RET_DV
    diag = (r2 < RET_DK) == even_col
    r2f = r2.astype(F32)
    rel = (r2 - (c2 % RET_DV)).astype(F32)
    for s in range(RET_SLABS):
        lgf2 = jnp.where(even_col, lg_ref[0, 2 * s], lg_ref[0, 2 * s + 1])
        lgb2 = jnp.where(even_col, lg_ref[1, 2 * s], lg_ref[1, 2 * s + 1])
        dmat_ref[s] = jnp.where(rel >= 0, jnp.exp(jnp.maximum(rel, 0.0) * lgf2), 0.0) \
            + jnp.where(rel <= 0, jnp.exp(jnp.maximum(-rel, 0.0) * lgb2), 0.0)
        zf_ref[s] = jnp.exp((C - 1 - r2f) * lgf2)
        zb_ref[s] = jnp.exp(r2f * lgb2)
        gf_ref[s] = jnp.exp(jnp.full((C, W2), C, F32) * lgf2)
        gb_ref[s] = jnp.exp(jnp.full((C, W2), C, F32) * lgb2)
        lgf1 = jnp.where(low, lg_ref[0, 2 * s], lg_ref[0, 2 * s + 1])
        lgb1 = jnp.where(low, lg_ref[1, 2 * s], lg_ref[1, 2 * s + 1])
        rif = ri.astype(F32)
        xf_ref[s] = jnp.exp((rif + 1.0) * lgf1)
        xb_ref[s] = jnp.exp((C - rif) * lgb1)
        rf_ref[s] = jnp.zeros((C, W2), F32)
        rb_ref[s] = jnp.zeros((C, W2), F32)

    def bwd_chunk(k_blk, v_blk, t):
        for s in range(RET_SLABS):
            st_ref[t, s, C:2 * C, :] = rb_ref[s].astype(BF16)
            vf = v_blk[:, s * W2:(s + 1) * W2].astype(F32)
            vz = jnp.concatenate([(vf * zf_ref[s]).astype(BF16), (vf * zb_ref[s]).astype(BF16)], axis=1)
            inc = lax.dot_general(k_blk[:, s * LANES:(s + 1) * LANES], vz, (((0,), (0,)), ((), ())),
                                  preferred_element_type=F32)
            uf_ref[t, s] = jnp.where(diag, inc[:, 0:W2], 0.0)
            rb_ref[s] = gb_ref[s] * rb_ref[s] + jnp.where(diag, inc[:, W2:2 * W2], 0.0)

    def fwd_chunk(q_blk, k_blk, v_blk, g_blk, t, store):
        for s in range(RET_SLABS):
            qs = q_blk[:, s * LANES:(s + 1) * LANES]
            ks = k_blk[:, s * LANES:(s + 1) * LANES]
            v2 = v_blk[:, s * W2:(s + 1) * W2]
            zk = jnp.zeros_like(ks)
            k2 = jnp.concatenate([jnp.where(low, ks, zk), jnp.where(low, zk, ks)], axis=0)
            sc = lax.dot_general(qs, k2, (((1,), (1,)), ((), ())), preferred_element_type=F32) * dmat_ref[s]
            zv = jnp.zeros_like(v2)
            vv = jnp.concatenate([jnp.where(even_col, v2, zv), jnp.where(even_col, zv, v2)], axis=0)
            y2 = jnp.dot(sc.astype(BF16), vv, preferred_element_type=F32)
            qf = qs.astype(F32)
            qx = jnp.concatenate([(qf * xf_ref[s]).astype(BF16), (qf * xb_ref[s]).astype(BF16)], axis=1)
            st_ref[t, s, 0:C, :] = rf_ref[s].astype(BF16)
            y2 = y2 + jnp.dot(qx, st_ref[t, s], preferred_element_type=F32)
            if store is not None:
                for e in range(2):
                    h = 2 * s + e
                    y = y2[:, e * RET_DV:(e + 1) * RET_DV]
                    gh = g_blk[:, h * RET_DV:(h + 1) * RET_DV].astype(F32)
                    yn = y * lax.rsqrt(jnp.mean(y * y, axis=-1, keepdims=True) + EPS)
                    store(h, (gh * yn).astype(BF16))
            rf_ref[s] = gf_ref[s] * rf_ref[s] + uf_ref[t, s]

    chunk = lambda t: slice(t * C, (t + 1) * C)

    for t in reversed(range(nc)):
        bwd_chunk(kc_ref[0, chunk(t), :], vc_ref[0, chunk(t), :], t)
    for j in reversed(range(n)):
        bwd_chunk(k_ref[0, chunk(j), :], v_ref[0, chunk(j), :], nc + j)

    def store_to(out_ref, sl):
        def store(h, val):
            out_ref[0, sl, h * RET_DV:(h + 1) * RET_DV] = val
        return store

    for t in range(nc):
        sl = chunk(t)
        if ctx_out:
            fwd_chunk(qc_ref[0, sl, :], kc_ref[0, sl, :], vc_ref[0, sl, :], gc_ref[0, sl, :], t, store_to(yc_ref, sl))
        else:
            for s in range(RET_SLABS):
                rf_ref[s] = gf_ref[s] * rf_ref[s] + uf_ref[t, s]
    for j in range(n):
        sl = chunk(j)
        fwd_chunk(q_ref[0, sl, :], k_ref[0, sl, :], v_ref[0, sl, :], g_ref[0, sl, :], nc + j, store_to(y_ref, sl))


def _retention_pool(log_gamma, ctx_parts, lat_parts, pool_parts, layer, pool_w, pool_scale, *, ctx_out):
    b, s, _ = lat_parts[0].shape
    lc = ctx_parts[-1].shape[1]
    nc, n = lc // RET_CHUNK, s // RET_CHUNK
    full = lambda a: pl.BlockSpec((1,) + a.shape[1:], lambda i: (i, 0, 0))
    in_specs = [pl.BlockSpec(memory_space=pltpu.SMEM)] + [full(a) for a in (*ctx_parts, *lat_parts, *pool_parts)] \
        + [_layer_block(pool_w, layer), _layer_block(pool_scale, layer)]
    vw = RET_HEADS * RET_DV
    lens = ([lc] if ctx_out else []) + [s]
    out_shape = [jax.ShapeDtypeStruct((b, ln, vw), BF16) for ln in lens] \
        + [jax.ShapeDtypeStruct((b, ln, POOL_GROUPS * POOL_GW), BF16) for ln in lens]
    out_specs = [full(o) for o in out_shape]
    wide = (RET_SLABS, RET_CHUNK, 2 * RET_DV)
    narrow = (RET_SLABS, RET_CHUNK, LANES)
    scratch = [pltpu.VMEM(wide, F32),
               pltpu.VMEM(wide, F32), pltpu.VMEM(wide, F32),
               pltpu.VMEM(narrow, F32), pltpu.VMEM(narrow, F32),
               pltpu.VMEM(wide, F32), pltpu.VMEM(wide, F32),
               pltpu.VMEM(wide, F32), pltpu.VMEM(wide, F32),
               pltpu.VMEM((nc + n, RET_SLABS, 2 * RET_CHUNK, 2 * RET_DV), BF16),
               pltpu.VMEM((nc + n, RET_SLABS, RET_CHUNK, 2 * RET_DV), F32),
               pltpu.VMEM((s + 32, LANES), F32), pltpu.VMEM((s + 32, LANES), F32)]
    res = pl.pallas_call(
        functools.partial(_ret_kernel, nc=nc, n=n, ctx_out=ctx_out),
        grid=(b,), in_specs=in_specs, out_specs=out_specs, out_shape=out_shape,
        scratch_shapes=scratch, compiler_params=_cparams(1),
        name="retention_pool",
    )(log_gamma, *ctx_parts, *lat_parts, *pool_parts, pool_w, pool_scale)
    return res if ctx_out else (None, res[0], None, res[1])


def _pool_tokens(u_ref, w_ref, sc_ref, o_ref, pa_ref, pb_ref):
    l = u_ref.shape[1]
    pad = 16
    t8 = lax.broadcasted_iota(jnp.int32, (8, LANES), 0)
    zeros_edge = jnp.zeros((pad, LANES), F32)
    ext = l + 16
    for g, w in enumerate(POOL_WINDOWS):
        ug = u_ref[0, :, g * POOL_GW:(g + 1) * POOL_GW].astype(F32)
        for buf in (pa_ref, pb_ref):
            buf[0:pad, :] = zeros_edge
            buf[pad + l:pad + l + pad, :] = zeros_edge
        pa_ref[pad:pad + l, :] = ug
        pb_ref[8:8 + ext, :] = pa_ref[7:7 + ext, :] + pa_ref[8:8 + ext, :]
        src, dst = pb_ref, pa_ref
        step = 1
        while 2 * step < w:
            dst[8:8 + ext, :] = src[8 - step:8 - step + ext, :] + src[8 + step:8 + step + ext, :]
            src, dst = dst, src
            step *= 2
        hw = w // 2
        top, bot = slice(pad, pad + 8), slice(pad + l - 8, pad + l)
        src[top, :] = src[top, :] * jnp.where(t8 < hw, w / (t8 + hw).astype(F32), 1.0)
        tb = l - 8 + t8
        src[bot, :] = src[bot, :] * jnp.where(tb + hw > l, w / (l - tb + hw).astype(F32), 1.0)
        pooled = src[pad:pad + l, :] * (1.0 / w)
        mixed = jnp.dot((pooled - ug).astype(BF16), w_ref[g], preferred_element_type=F32)
        o_ref[0, :, g * POOL_GW:(g + 1) * POOL_GW] = (mixed * sc_ref[:, g * POOL_GW:(g + 1) * POOL_GW]).astype(BF16)


def _attn_kernel(*refs, local, nq, ncb):
    sink_ref = refs[0]
    if local:
        q_ref, k_ref, v_ref, kc_ref, vc_ref, o_ref, pall_ref = refs[1:8]
        slot_a, slot_b = refs[8:10], refs[10:12]
    else:
        q_ref, kc_ref, vc_ref, o_ref, pall_ref = refs[1:6]
        slot_a = refs[6:8]
        k_ref = v_ref = None
    BLK = ATT_BLOCK
    KB = 2 * BLK
    nloc = 3 if local else 0
    kj = lax.broadcasted_iota(jnp.int32, (BLK, 2 * BLK), 0)
    qi = lax.broadcasted_iota(jnp.int32, (BLK, 2 * BLK), 1) % BLK
    tri_prev = jnp.where(kj >= qi, 0.0, NEG_BIG)
    tri_next = jnp.where(kj <= qi, 0.0, NEG_BIG)
    head_row = lax.broadcasted_iota(jnp.int32, (LANES, BLK), 0) < ATT_HD
    nkeys = (nloc + ncb) * KB
    orow = lax.broadcasted_iota(jnp.int32, (16, nkeys), 0)
    ocol = lax.broadcasted_iota(jnp.int32, (16, nkeys), 1)
    ones_rows = jnp.where(orow == (ocol % KB) // BLK, 1.0, 0.0).astype(BF16)

    def neighbours(nb):
        if not local:
            return []
        return [(blk, bias) for blk, bias in ((nb - 1, tri_prev), (nb, None), (nb + 1, tri_next)) if 0 <= blk < nq]

    def scores(nb, slot):
        s_ref, m_ref = slot
        near = neighbours(nb)
        for kh in range(ATT_KV_HEADS):
            qt = q_ref[0, nb, kh]
            kcat = jnp.concatenate([k_ref[kh, 0, blk * KB:(blk + 1) * KB, :] for blk, _ in near] + [kc_ref[kh, 0]],
                                   axis=0)
            sc = jnp.dot(kcat, qt, preferred_element_type=F32)
            mx = [None, None]
            for j in range(len(near) + ncb):
                for e in range(2):
                    krow = slice((2 * j + e) * BLK, (2 * j + e + 1) * BLK)
                    c = sc[krow]
                    if j < len(near) and near[j][1] is not None:
                        c = c + near[j][1]
                    s_ref[kh, krow, :] = c
                    cm = jnp.max(c.reshape(BLK // 8, 8, 2 * BLK), axis=0)
                    mx[e] = cm if mx[e] is None else jnp.maximum(mx[e], cm)
            for e in range(2):
                m_ref[kh, e] = mx[e]

    def softmax_pv(nb, slot):
        s_ref, m_ref = slot
        p_ref = pall_ref.at[nb]
        rows = slice(nb * BLK, (nb + 1) * BLK)
        near = neighbours(nb)
        nkb = len(near) + ncb
        for kh in range(ATT_KV_HEADS):
            sink_term = [[None, None], [None, None]]
            for s in range(2):
                scols = slice(s * BLK, (s + 1) * BLK)
                for e in range(2):
                    sink = sink_ref[kh * 4 + 2 * s + e] * LOG2E
                    m = jnp.maximum(jnp.max(m_ref[kh, e, :, scols], axis=0, keepdims=True), sink)
                    for j in range(nkb):
                        kr = slice((2 * j + e) * BLK, (2 * j + e + 1) * BLK)
                        p_ref[kh, kr, scols] = jnp.exp2(s_ref[kh, kr, scols] - m).astype(BF16)
                    sink_term[s][e] = jnp.exp2(sink - m)
            vcat = jnp.concatenate([v_ref[kh, 0, blk] for blk, _ in near]
                                   + [vc_ref[kh, 0, j] for j in range(ncb)], axis=1)
            ot = jnp.dot(jnp.concatenate([vcat, ones_rows[:, 0:nkb * KB]], axis=0), p_ref[kh, 0:nkb * KB, :],
                         preferred_element_type=F32)
            for s in range(2):
                scols = slice(s * BLK, (s + 1) * BLK)
                inv = [1.0 / (ot[LANES + e:LANES + e + 1, scols] + sink_term[s][e]) for e in range(2)]
                scale = jnp.where(head_row, inv[0], inv[1])
                o_ref[0, rows, (2 * kh + s) * LANES:(2 * kh + s + 1) * LANES] = \
                    (ot[0:LANES, scols] * scale).T.astype(BF16)

    if not local:
        for nblk in range(nq):
            scores(nblk, slot_a)
            softmax_pv(nblk, slot_a)
        return

    scores(0, slot_a)
    for nblk in range(nq):
        cur, nxt = (slot_a, slot_b) if nblk % 2 == 0 else (slot_b, slot_a)
        if nblk + 1 < nq:
            scores(nblk + 1, nxt)
        softmax_pv(nblk, cur)


def _attention(sink, q, k, v, kc, vc):
    b, nq = q.shape[:2]
    l = nq * ATT_BLOCK
    qw = ATT_HEADS * ATT_HD
    local = k is not None
    ncb = vc.shape[2]
    tok = pl.BlockSpec((1, l, qw), lambda i: (i, 0, 0))
    lead = lambda a: pl.BlockSpec((1,) + a.shape[1:], lambda i: (i,) + (0,) * (a.ndim - 1))
    second = lambda a: pl.BlockSpec((a.shape[0], 1) + a.shape[2:], lambda i: (0, i) + (0,) * (a.ndim - 2))
    kvs = ([k, v] if local else []) + [kc, vc]
    args = [sink, q] + kvs
    in_specs = [pl.BlockSpec(memory_space=pltpu.SMEM), lead(q)] + [second(a) for a in kvs]
    nkb = (3 if local else 0) + ncb
    tile = (ATT_KV_HEADS, nkb * 2 * ATT_BLOCK, 2 * ATT_BLOCK)
    slot = [pltpu.VMEM(tile, F32), pltpu.VMEM((ATT_KV_HEADS, 2, 8, 2 * ATT_BLOCK), F32)]
    scratch = [pltpu.VMEM((nq,) + tile, BF16)] + slot * (2 if local else 1)
    return pl.pallas_call(
        functools.partial(_attn_kernel, local=local, nq=nq, ncb=ncb),
        grid=(b,), in_specs=in_specs, out_specs=tok,
        out_shape=jax.ShapeDtypeStruct((b, l, qw), BF16),
        scratch_shapes=scratch,
        compiler_params=_cparams(1),
        name="attn_window" if local else "attn_ctx",
    )(*args)


def _merge_mlp_kernel(x_ref, yr_ref, yp_ref, ya_ref, gt_ref, mod_ref, nw_ref,
                      wr_ref, wp_ref, wa_ref, wo_ref, w1_ref, w2_ref, o_ref, *, d, hid_chunk):
    mod = mod_ref[0]
    g1, sh2, sc2, g2 = mod[:, 2 * d:3 * d], mod[:, 3 * d:4 * d], mod[:, 4 * d:5 * d], mod[:, 5 * d:6 * d]
    y = None
    for i, (y_ref, w_ref) in enumerate(((yr_ref, wr_ref), (yp_ref, wp_ref), (ya_ref, wa_ref))):
        proj = jnp.dot(y_ref[0], w_ref[...], preferred_element_type=F32)
        term = _sigmoid(gt_ref[0, :, i * d:(i + 1) * d].astype(F32)) * proj
        y = term if y is None else y + term
    mix = jnp.dot(y.astype(BF16), wo_ref[...], preferred_element_type=F32)
    x1 = x_ref[0] + g1 * mix
    ms = jnp.mean(x1 * x1, axis=-1, keepdims=True)
    h = ((x1 * lax.rsqrt(ms + EPS) * nw_ref[...]) * (1.0 + sc2) + sh2).astype(BF16)
    acc = None
    hidden = w1_ref.shape[1]
    for c in range(hidden // hid_chunk):
        a = jnp.dot(h, w1_ref[:, c * hid_chunk:(c + 1) * hid_chunk], preferred_element_type=F32)
        a = jnp.maximum(a, 0.0)
        part = jnp.dot((a * a).astype(BF16), w2_ref[c * hid_chunk:(c + 1) * hid_chunk, :], preferred_element_type=F32)
        acc = part if acc is None else acc + part
    o_ref[0] = x1 + g2 * acc


def _merge_mlp(x3, y_ret, y_pool, y_att, gates, mod3, mod_row, layer, nw, weights, *, tm):
    bx, l, d = x3.shape
    row = lambda b, i: (b, i, 0)
    bw = y_ret.shape[2]
    in_specs = [pl.BlockSpec((1, tm, d), row)] + [pl.BlockSpec((1, tm, bw), row)] * 3 + [
        pl.BlockSpec((1, tm, N_BRANCH * d), row),
        pl.BlockSpec((1, 1, mod3.shape[2]), lambda b, i: (mod_row(b), 0, 0)),
        _layer_block(nw, layer)] + [_layer_block(w, None, resident=True) for w in weights]
    return pl.pallas_call(
        functools.partial(_merge_mlp_kernel, d=d, hid_chunk=1024),
        grid=(bx, l // tm), in_specs=in_specs, out_specs=pl.BlockSpec((1, tm, d), row),
        out_shape=jax.ShapeDtypeStruct((bx, l, d), F32),
        compiler_params=_cparams(2),
        name="merge_mlp",
    )(x3, y_ret, y_pool, y_att, gates, mod3, nw, *weights)


def _rope_tables(s):
    pos = jnp.arange(s, dtype=F32)
    half = RET_DK // 2
    ang = pos[:, None] * (RET_ROPE_BASE ** (-jnp.arange(half, dtype=F32) / half))[None, :]
    cos_r = jnp.tile(jnp.concatenate([jnp.cos(ang), jnp.cos(ang)], axis=1), (1, LANES // RET_DK))
    sin_r = jnp.tile(jnp.concatenate([-jnp.sin(ang), jnp.sin(ang)], axis=1), (1, LANES // RET_DK))
    q = ATT_HD // 4
    freq = ROPE_BASE ** (-jnp.arange(q, dtype=F32) / q)
    ar = (jnp.arange(s) // GRID_W).astype(F32)[:, None] * freq[None, :]
    ac = (jnp.arange(s) % GRID_W).astype(F32)[:, None] * freq[None, :]
    cos_a = jnp.concatenate([jnp.cos(ar), jnp.cos(ar), jnp.cos(ac), jnp.cos(ac)], axis=1)
    sin_a = jnp.concatenate([-jnp.sin(ar), jnp.sin(ar), -jnp.sin(ac), jnp.sin(ac)], axis=1)
    return cos_r, sin_r, jnp.tile(cos_a, (1, LANES // ATT_HD)), jnp.tile(sin_a, (1, LANES // ATT_HD))


def kernel(x, c, ctx, c_ctx, norm1_w, norm2_w, ada_w, ada_b, w_in, ret_decay, pool_w, pool_scale, q_norm_w, k_norm_w, attn_sink, w_ret_out, w_pool_out, w_attn_out, w_out, w_mlp1, w_mlp2):
    b, s, d = x.shape
    lc = ctx.shape[1]
    depth = w_in.shape[0]
    tm = 512 if s % 512 == 0 else 256
    tmc = 512 if (b * lc) % 512 == 0 else 256
    tm_in = 2 * INPROJ_SUBTILE if s % (2 * INPROJ_SUBTILE) == 0 else tm

    rows = -(-(b + 1) // 8) * 8
    cc = jnp.zeros((rows, d), F32).at[:b].set(c).at[b].set(c_ctx)
    mod3 = _modulation(cc, ada_w, ada_b).reshape(depth * rows, 1, 6 * d)

    tables = _rope_tables(s)
    xc = ctx.reshape(1, b * lc, d)

    log_gamma = jnp.log1p(-jnp.exp2(-ret_decay.astype(F32)))
    w_in_b = w_in.astype(BF16)
    nw1, nw2 = norm1_w.reshape(depth, 1, d), norm2_w.reshape(depth, 1, d)
    qnw = jnp.tile(q_norm_w, (1, LANES // ATT_HD)).reshape(depth, 1, LANES)
    knw = jnp.tile(k_norm_w, (1, LANES // ATT_HD)).reshape(depth, 1, LANES)
    pw = pool_w.astype(BF16)
    psc = pool_scale.reshape(depth, 1, -1)
    merge_w_f32 = (w_ret_out, w_pool_out, w_attn_out, w_out, w_mlp1, w_mlp2)
    per_batch = lambda p: p.reshape(b, lc, p.shape[-1])
    ncb = lc // ATT_BLOCK

    for l in range(depth):
        last = l == depth - 1
        lat_row = lambda bi, l=l: l * rows + bi
        ctx_row = lambda bi, l=l: l * rows + b

        cparts = _inproj(xc, mod3, ctx_row, l, nw1, w_in_b, qnw, knw, None, tm=tmc, ctx_only=last)
        rk_c, rv_c = per_batch(cparts[0]), per_batch(cparts[1])
        ak_c = cparts[2].reshape(ATT_KV_HEADS, b, 2 * lc, LANES)
        av_c = cparts[3].reshape(ATT_KV_HEADS, b, ncb, LANES, 2 * ATT_BLOCK)

        rk, rv, ak, av, rq, rg, aq, pu, gt, *weights = _inproj(x, mod3, lat_row, l, nw1, w_in_b, qnw, knw, tables,
                                                               tm=tm_in, ctx_only=False, cast=merge_w_f32)
        if last:
            _, y_ret, _, y_pool = _retention_pool(log_gamma[l], (rk_c, rv_c), (rq, rk, rv, rg), (pu,),
                                                  l, pw, psc, ctx_out=False)
        else:
            rq_c, rg_c, pu_c, gt_c = [per_batch(cparts[i]) for i in (4, 5, 7, 8)]
            aq_c = cparts[6].reshape(b, ncb, ATT_KV_HEADS, LANES, 2 * ATT_BLOCK)
            y_ret_c, y_ret, y_pool_c, y_pool = _retention_pool(log_gamma[l], (rq_c, rk_c, rv_c, rg_c),
                                                               (rq, rk, rv, rg), (pu_c, pu), l, pw, psc, ctx_out=True)
        y_att = _attention(attn_sink[l], aq, ak, av, ak_c, av_c)
        x = _merge_mlp(x, y_ret, y_pool, y_att, gt, mod3, lat_row, l, nw2, weights, tm=tm)

        if not last:
            y_att_c = _attention(attn_sink[l], aq_c, None, None, ak_c, av_c)
            flat = lambda a: a.reshape(1, b * lc, a.shape[-1])
            xc = _merge_mlp(xc, flat(y_ret_c), flat(y_pool_c), flat(y_att_c), flat(gt_c), mod3, ctx_row, l, nw2,
                            weights, tm=tmc)
    return x
```

```python
import functools

import jax
import jax.numpy as jnp
from jax import lax
from jax.experimental import pallas as pl
from jax.experimental.pallas import tpu as pltpu

F32 = jnp.float32
BF16 = jnp.bfloat16

EPS = 1e-6
LOG2E = 1.4426950408889634
LANES = 128
RET_HEADS, RET_DK, RET_DV, RET_CHUNK = 4, 64, 128, 128
RET_SLABS = RET_HEADS // 2
RET_ROPE_BASE = 10000.0
POOL_GROUPS, POOL_GW = 4, 128
POOL_WINDOWS = (2, 4, 8, 16)
ATT_HEADS, ATT_KV_HEADS, ATT_HD = 8, 2, 64
ATT_WINDOW = ATT_BLOCK = 128
ROPE_BASE = 10000.0
GRID_W = 64
N_BRANCH = 3
CTX_SIDE_COLS = 1024
INPROJ_SUBTILE = 512
NEG_BIG = -1e30
VMEM_LIMIT = 62 * 1024 * 1024


def _cparams(n_axes):
    return pltpu.CompilerParams(dimension_semantics=("parallel",) * n_axes,
                                vmem_limit_bytes=VMEM_LIMIT)


def _layer_block(stacked, layer, block=None, resident=False):
    if layer is None:
        block = tuple(stacked.shape) if block is None else block
        shape, idx = block, (0,) * len(block)
    else:
        block = tuple(stacked.shape[1:]) if block is None else block
        shape, idx = (None,) + block, (layer,) + (0,) * len(block)
    return pl.BlockSpec(shape, lambda *_: idx, pipeline_mode=pl.Buffered(1) if resident else None)


def _sigmoid(x):
    return 0.5 * jnp.tanh(0.5 * x) + 0.5


def _mod_kernel(cc_ref, w_ref, b_ref, o_ref):
    cc = cc_ref[...]
    s = (cc * _sigmoid(cc)).astype(BF16)
    o_ref[0] = jnp.dot(s, w_ref[0].astype(BF16), preferred_element_type=F32) + b_ref[0]


def _modulation(cc, ada_w, ada_b):
    depth, d, n = ada_w.shape
    r = cc.shape[0]
    tn = 1024
    return pl.pallas_call(
        _mod_kernel,
        grid=(depth, n // tn),
        in_specs=[pl.BlockSpec((r, d), lambda l, j: (0, 0)),
                  pl.BlockSpec((1, d, tn), lambda l, j: (l, 0, j)),
                  pl.BlockSpec((1, 1, tn), lambda l, j: (l, 0, j))],
        out_specs=pl.BlockSpec((1, r, tn), lambda l, j: (l, 0, j)),
        out_shape=jax.ShapeDtypeStruct((depth, r, n), F32),
        compiler_params=_cparams(2),
        name="adaln_mod",
    )(cc, ada_w, ada_b.reshape(depth, 1, n))


def _swap_halves(x, half):
    lane = lax.broadcasted_iota(jnp.int32, x.shape, 1)
    first = (lane % (2 * half)) < half
    return jnp.where(first, pltpu.roll(x, LANES - half, 1), pltpu.roll(x, half, 1))


def _rope(x, cos, sin, half):
    return x * cos + _swap_halves(x, half) * sin


def _inproj_kernel(*refs, d, ctx_only, use_rope, n_cast):
    x_ref, mod_ref, nw_ref, w_ref, qnw_ref, knw_ref = refs[:6]
    pos = 6
    if use_rope:
        cr_ref, sr_ref, ca_ref, sa_ref = refs[6:10]
        pos = 10
    cast_in, refs_out = refs[pos:pos + n_cast], refs[pos + n_cast:]
    outs, cast_out = refs_out[:len(refs_out) - n_cast], refs_out[len(refs_out) - n_cast:]
    tm = x_ref.shape[1]

    for src, dst in zip(cast_in, cast_out):
        dst[...] = src[...].astype(BF16)

    mod = mod_ref[0]
    shift, scale = mod[:, 0:d], mod[:, d:2 * d]
    sub = min(tm, INPROJ_SUBTILE)
    lane = lax.broadcasted_iota(jnp.int32, (sub, LANES), 1)
    low = lane < ATT_HD

    def slab(a, s):
        return a[:, s * LANES:(s + 1) * LANES]

    def qk_norm(a, w):
        sq = a * a
        lo = jnp.sum(jnp.where(low, sq, 0.0), axis=-1, keepdims=True)
        hi = jnp.sum(jnp.where(low, 0.0, sq), axis=-1, keepdims=True)
        msq = jnp.where(low, lo, hi) * (1.0 / ATT_HD)
        return a * lax.rsqrt(msq + EPS) * w

    def subtile(r0):
        rows = slice(r0, r0 + sub)
        blk0 = r0 // ATT_BLOCK
        x = x_ref[0, rows]
        ms = jnp.mean(x * x, axis=-1, keepdims=True)
        u = ((x * lax.rsqrt(ms + EPS) * nw_ref[...]) * (1.0 + scale) + shift).astype(BF16)

        def proj(a, b):
            return jnp.dot(u, w_ref[:, a:b], preferred_element_type=F32)

        def ret_rope(a):
            return _rope(a, cr_ref[rows], sr_ref[rows], RET_DK // 2) if use_rope else a

        def att_rope(a):
            return _rope(a, ca_ref[rows], sa_ref[rows], ATT_HD // 4) if use_rope else a

        def store_kv(a, out_ref):
            sw = pltpu.roll(a, ATT_HD, 1)
            zero = jnp.zeros_like(a)
            lo_hi = ((jnp.where(low, a, zero), jnp.where(low, zero, sw)),
                     (jnp.where(low, sw, zero), jnp.where(low, zero, a)))
            for kh in range(ATT_KV_HEADS):
                for j in range(sub // ATT_BLOCK):
                    for e in range(2):
                        k0 = (2 * (blk0 + j) + e) * ATT_BLOCK
                        out_ref[kh, 0, k0:k0 + ATT_BLOCK, :] = \
                            lo_hi[kh][e][j * ATT_BLOCK:(j + 1) * ATT_BLOCK].astype(BF16)

        rk_ref, rv_ref, ak_ref, av_ref = outs[:4]
        a = proj(0, 256)
        for s in range(RET_SLABS):
            rk_ref[0, rows, s * LANES:(s + 1) * LANES] = (ret_rope(slab(a, s)) * RET_DK ** -0.5).astype(BF16)
        rv_ref[0, rows] = proj(256, 768).astype(BF16)
        a = proj(768, 1024)
        store_kv(att_rope(qk_norm(slab(a, 0), knw_ref[...])), ak_ref)
        vs = slab(a, 1)
        zq = jnp.zeros((ATT_HD, ATT_BLOCK), F32)
        for j in range(sub // ATT_BLOCK):
            vt = vs[j * ATT_BLOCK:(j + 1) * ATT_BLOCK].T
            for kh in range(ATT_KV_HEADS):
                top = vt[kh * ATT_HD:(kh + 1) * ATT_HD]
                tile = jnp.concatenate([jnp.concatenate([top, zq], axis=1), jnp.concatenate([zq, top], axis=1)],
                                       axis=0)
                av_ref[kh, 0, blk0 + j] = tile.astype(BF16)
        if ctx_only:
            return
        rq_ref, rg_ref, aq_ref, pu_ref, gt_ref = outs[4:]
        a = proj(1024, 1280)
        for s in range(RET_SLABS):
            rq_ref[0, rows, s * LANES:(s + 1) * LANES] = ret_rope(slab(a, s)).astype(BF16)
        g = proj(1280, 1792)
        rg_ref[0, rows] = (g * _sigmoid(g)).astype(BF16)
        a = proj(1792, 2304)
        for s in range(ATT_HEADS // 2):
            qs = att_rope(qk_norm(slab(a, s), qnw_ref[...])) * (ATT_HD ** -0.5 * LOG2E)
            for j in range(sub // ATT_BLOCK):
                aq_ref[0, blk0 + j, s // 2, :, (s % 2) * ATT_BLOCK:(s % 2 + 1) * ATT_BLOCK] = \
                    qs[j * ATT_BLOCK:(j + 1) * ATT_BLOCK].T.astype(BF16)
        pu_ref[0, rows] = proj(2304, 2816).astype(BF16)
        for s in range(6):
            gt_ref[0, rows, s * 512:(s + 1) * 512] = proj(2816 + s * 512, 2816 + (s + 1) * 512).astype(BF16)

    for r0 in range(0, tm, sub):
        subtile(r0)


def _inproj(x3, mod3, mod_row, layer, nw, w, qnw, knw, tables, *, tm, ctx_only, cast=()):
    bx, l, d = x3.shape
    ncol = CTX_SIDE_COLS if ctx_only else w.shape[2]
    use_rope = tables is not None
    grid = (bx, l // tm)
    row = lambda b, i: (b, i, 0)
    in_specs = [pl.BlockSpec((1, tm, d), row),
                pl.BlockSpec((1, 1, mod3.shape[2]), lambda b, i: (mod_row(b), 0, 0)),
                _layer_block(nw, layer),
                _layer_block(w, layer, (d, ncol), resident=True),
                _layer_block(qnw, layer),
                _layer_block(knw, layer)]
    args = [x3, mod3, nw, w, qnw, knw]
    if use_rope:
        in_specs += [pl.BlockSpec((tm, LANES), lambda b, i: (i, 0))] * 4
        args += list(tables)
    kv_spec = pl.BlockSpec((ATT_KV_HEADS, 1, 2 * tm, LANES), lambda b, i: (0, b, i, 0))
    kv_shape = jax.ShapeDtypeStruct((ATT_KV_HEADS, bx, 2 * l, LANES), BF16)
    tok = lambda wd: (pl.BlockSpec((1, tm, wd), row), jax.ShapeDtypeStruct((bx, l, wd), BF16))
    nblk, tblk = l // ATT_BLOCK, tm // ATT_BLOCK
    vt_spec = pl.BlockSpec((ATT_KV_HEADS, 1, tblk, LANES, 2 * ATT_BLOCK), lambda b, i: (0, b, i, 0, 0))
    vt_shape = jax.ShapeDtypeStruct((ATT_KV_HEADS, bx, nblk, LANES, 2 * ATT_BLOCK), BF16)
    qt_spec = pl.BlockSpec((1, tblk, ATT_KV_HEADS, LANES, 2 * ATT_BLOCK), lambda b, i: (b, i, 0, 0, 0))
    qt_shape = jax.ShapeDtypeStruct((bx, nblk, ATT_KV_HEADS, LANES, 2 * ATT_BLOCK), BF16)
    outs = [tok(256), tok(512), (kv_spec, kv_shape), (vt_spec, vt_shape)]
    if not ctx_only:
        outs += [tok(256), tok(512), (qt_spec, qt_shape), tok(512), tok(3072)]
    steps = bx * (l // tm)
    for wf in cast:
        k, n = wf.shape[1:]
        rps = max(16, k // steps)
        per = steps // (k // rps)
        in_specs.append(pl.BlockSpec((None, rps, n), lambda b, i, per=per: (layer, (b * (l // tm) + i) // per, 0)))
        outs.append((pl.BlockSpec((rps, n), lambda b, i, per=per: ((b * (l // tm) + i) // per, 0)),
                     jax.ShapeDtypeStruct((k, n), BF16)))
        args.append(wf)
    return pl.pallas_call(
        functools.partial(_inproj_kernel, d=d, ctx_only=ctx_only, use_rope=use_rope, n_cast=len(cast)),
        grid=grid, in_specs=in_specs, out_specs=[o[0] for o in outs], out_shape=[o[1] for o in outs],
        compiler_params=_cparams(2),
        name="inproj_ctx" if not use_rope else "inproj",
    )(*args)


def _ret_kernel(*refs, nc, n, ctx_out):
    lg_ref = refs[0]
    if ctx_out:
        qc_ref, kc_ref, vc_ref, gc_ref, q_ref, k_ref, v_ref, g_ref = refs[1:9]
        puc_ref, pu_ref, pw_ref, psc_ref, yc_ref, y_ref, ypc_ref, yp_ref = refs[9:17]
        scr = refs[17:]
    else:
        kc_ref, vc_ref, q_ref, k_ref, v_ref, g_ref = refs[1:7]
        pu_ref, pw_ref, psc_ref, y_ref, yp_ref = refs[7:12]
        qc_ref = gc_ref = yc_ref = puc_ref = ypc_ref = None
        scr = refs[12:]
    dmat_ref, zf_ref, zb_ref, xf_ref, xb_ref, gf_ref, gb_ref, rf_ref, rb_ref, st_ref, uf_ref, pa_ref, pb_ref = scr

    if ctx_out:
        _pool_tokens(puc_ref, pw_ref, psc_ref, ypc_ref, pa_ref, pb_ref)
    _pool_tokens(pu_ref, pw_ref, psc_ref, yp_ref, pa_ref, pb_ref)
    C = RET_CHUNK
    W2 = 2 * RET_DV

    ri = lax.broadcasted_iota(jnp.int32, (C, LANES), 0)
    ci = lax.broadcasted_iota(jnp.int32, (C, LANES), 1)
    low = ci < RET_DK
    r2 = lax.broadcasted_iota(jnp.int32, (C, W2), 0)
    c2 = lax.broadcasted_iota(jnp.int32, (C, W2), 1)
    even_col = c2 < RET_DV
    diag = (r2 < RET_DK) == even_col
    r2f = r2.astype(F32)
    rel = (r2 - (c2 % RET_DV)).astype(F32)
    for s in range(RET_SLABS):
        lgf2 = jnp.where(even_col, lg_ref[0, 2 * s], lg_ref[0, 2 * s + 1])
        lgb2 = jnp.where(even_col, lg_ref[1, 2 * s], lg_ref[1, 2 * s + 1])
        dmat_ref[s] = jnp.where(rel >= 0, jnp.exp(jnp.maximum(rel, 0.0) * lgf2), 0.0) \
            + jnp.where(rel <= 0, jnp.exp(jnp.maximum(-rel, 0.0) * lgb2), 0.0)
        zf_ref[s] = jnp.exp((C - 1 - r2f) * lgf2)
        zb_ref[s] = jnp.exp(r2f * lgb2)
        gf_ref[s] = jnp.exp(jnp.full((C, W2), C, F32) * lgf2)
        gb_ref[s] = jnp.exp(jnp.full((C, W2), C, F32) * lgb2)
        lgf1 = jnp.where(low, lg_ref[0, 2 * s], lg_ref[0, 2 * s + 1])
        lgb1 = jnp.where(low, lg_ref[1, 2 * s], lg_ref[1, 2 * s + 1])
        rif = ri.astype(F32)
        xf_ref[s] = jnp.exp((rif + 1.0) * lgf1)
        xb_ref[s] = jnp.exp((C - rif) * lgb1)
        rf_ref[s] = jnp.zeros((C, W2), F32)
        rb_ref[s] = jnp.zeros((C, W2), F32)

    def bwd_chunk(k_blk, v_blk, t):
        for s in range(RET_SLABS):
            st_ref[t, s, C:2 * C, :] = rb_ref[s].astype(BF16)
            vf = v_blk[:, s * W2:(s + 1) * W2].astype(F32)
            vz = jnp.concatenate([(vf * zf_ref[s]).astype(BF16), (vf * zb_ref[s]).astype(BF16)], axis=1)
            inc = lax.dot_general(k_blk[:, s * LANES:(s + 1) * LANES], vz, (((0,), (0,)), ((), ())),
                                  preferred_element_type=F32)
            uf_ref[t, s] = jnp.where(diag, inc[:, 0:W2], 0.0)
            rb_ref[s] = gb_ref[s] * rb_ref[s] + jnp.where(diag, inc[:, W2:2 * W2], 0.0)

    def fwd_chunk(q_blk, k_blk, v_blk, g_blk, t, store):
        for s in range(RET_SLABS):
            qs = q_blk[:, s * LANES:(s + 1) * LANES]
            ks = k_blk[:, s * LANES:(s + 1) * LANES]
            v2 = v_blk[:, s * W2:(s + 1) * W2]
            zk = jnp.zeros_like(ks)
            k2 = jnp.concatenate([jnp.where(low, ks, zk), jnp.where(low, zk, ks)], axis=0)
            sc = lax.dot_general(qs, k2, (((1,), (1,)), ((), ())), preferred_element_type=F32) * dmat_ref[s]
            zv = jnp.zeros_like(v2)
            vv = jnp.concatenate([jnp.where(even_col, v2, zv), jnp.where(even_col, zv, v2)], axis=0)
            y2 = jnp.dot(sc.astype(BF16), vv, preferred_element_type=F32)
            qf = qs.astype(F32)
            qx = jnp.concatenate([(qf * xf_ref[s]).astype(BF16), (qf * xb_ref[s]).astype(BF16)], axis=1)
            st_ref[t, s, 0:C, :] = rf_ref[s].astype(BF16)
            y2 = y2 + jnp.dot(qx, st_ref[t, s], preferred_element_type=F32)
            if store is not None:
                for e in range(2):
                    h = 2 * s + e
                    y = y2[:, e * RET_DV:(e + 1) * RET_DV]
                    gh = g_blk[:, h * RET_DV:(h + 1) * RET_DV].astype(F32)
                    yn = y * lax.rsqrt(jnp.mean(y * y, axis=-1, keepdims=True) + EPS)
                    store(h, (gh * yn).astype(BF16))
            rf_ref[s] = gf_ref[s] * rf_ref[s] + uf_ref[t, s]

    chunk = lambda t: slice(t * C, (t + 1) * C)

    for t in reversed(range(nc)):
        bwd_chunk(kc_ref[0, chunk(t), :], vc_ref[0, chunk(t), :], t)
    for j in reversed(range(n)):
        bwd_chunk(k_ref[0, chunk(j), :], v_ref[0, chunk(j), :], nc + j)

    def store_to(out_ref, sl):
        def store(h, val):
            out_ref[0, sl, h * RET_DV:(h + 1) * RET_DV] = val
        return store

    for t in range(nc):
        sl = chunk(t)
        if ctx_out:
            fwd_chunk(qc_ref[0, sl, :], kc_ref[0, sl, :], vc_ref[0, sl, :], gc_ref[0, sl, :], t, store_to(yc_ref, sl))
        else:
            for s in range(RET_SLABS):
                rf_ref[s] = gf_ref[s] * rf_ref[s] + uf_ref[t, s]
    for j in range(n):
        sl = chunk(j)
        fwd_chunk(q_ref[0, sl, :], k_ref[0, sl, :], v_ref[0, sl, :], g_ref[0, sl, :], nc + j, store_to(y_ref, sl))


def _retention_pool(log_gamma, ctx_parts, lat_parts, pool_parts, layer, pool_w, pool_scale, *, ctx_out):
    b, s, _ = lat_parts[0].shape
    lc = ctx_parts[-1].shape[1]
    nc, n = lc // RET_CHUNK, s // RET_CHUNK
    full = lambda a: pl.BlockSpec((1,) + a.shape[1:], lambda i: (i, 0, 0))
    in_specs = [pl.BlockSpec(memory_space=pltpu.SMEM)] + [full(a) for a in (*ctx_parts, *lat_parts, *pool_parts)] \
        + [_layer_block(pool_w, layer), _layer_block(pool_scale, layer)]
    vw = RET_HEADS * RET_DV
    lens = ([lc] if ctx_out else []) + [s]
    out_shape = [jax.ShapeDtypeStruct((b, ln, vw), BF16) for ln in lens] \
        + [jax.ShapeDtypeStruct((b, ln, POOL_GROUPS * POOL_GW), BF16) for ln in lens]
    out_specs = [full(o) for o in out_shape]
    wide = (RET_SLABS, RET_CHUNK, 2 * RET_DV)
    narrow = (RET_SLABS, RET_CHUNK, LANES)
    scratch = [pltpu.VMEM(wide, F32),
               pltpu.VMEM(wide, F32), pltpu.VMEM(wide, F32),
               pltpu.VMEM(narrow, F32), pltpu.VMEM(narrow, F32),
               pltpu.VMEM(wide, F32), pltpu.VMEM(wide, F32),
               pltpu.VMEM(wide, F32), pltpu.VMEM(wide, F32),
               pltpu.VMEM((nc + n, RET_SLABS, 2 * RET_CHUNK, 2 * RET_DV), BF16),
               pltpu.VMEM((nc + n, RET_SLABS, RET_CHUNK, 2 * RET_DV), F32),
               pltpu.VMEM((s + 32, LANES), F32), pltpu.VMEM((s + 32, LANES), F32)]
    res = pl.pallas_call(
        functools.partial(_ret_kernel, nc=nc, n=n, ctx_out=ctx_out),
        grid=(b,), in_specs=in_specs, out_specs=out_specs, out_shape=out_shape,
        scratch_shapes=scratch, compiler_params=_cparams(1),
        name="retention_pool",
    )(log_gamma, *ctx_parts, *lat_parts, *pool_parts, pool_w, pool_scale)
    return res if ctx_out else (None, res[0], None, res[1])


def _pool_tokens(u_ref, w_ref, sc_ref, o_ref, pa_ref, pb_ref):
    l = u_ref.shape[1]
    pad = 16
    t8 = lax.broadcasted_iota(jnp.int32, (8, LANES), 0)
    zeros_edge = jnp.zeros((pad, LANES), F32)
    ext = l + 16
    for g, w in enumerate(POOL_WINDOWS):
        ug = u_ref[0, :, g * POOL_GW:(g + 1) * POOL_GW].astype(F32)
        for buf in (pa_ref, pb_ref):
            buf[0:pad, :] = zeros_edge
            buf[pad + l:pad + l + pad, :] = zeros_edge
        pa_ref[pad:pad + l, :] = ug
        pb_ref[8:8 + ext, :] = pa_ref[7:7 + ext, :] + pa_ref[8:8 + ext, :]
        src, dst = pb_ref, pa_ref
        step = 1
        while 2 * step < w:
            dst[8:8 + ext, :] = src[8 - step:8 - step + ext, :] + src[8 + step:8 + step + ext, :]
            src, dst = dst, src
            step *= 2
        hw = w // 2
        top, bot = slice(pad, pad + 8), slice(pad + l - 8, pad + l)
        src[top, :] = src[top, :] * jnp.where(t8 < hw, w / (t8 + hw).astype(F32), 1.0)
        tb = l - 8 + t8
        src[bot, :] = src[bot, :] * jnp.where(tb + hw > l, w / (l - tb + hw).astype(F32), 1.0)
        pooled = src[pad:pad + l, :] * (1.0 / w)
        mixed = jnp.dot((pooled - ug).astype(BF16), w_ref[g], preferred_element_type=F32)
        o_ref[0, :, g * POOL_GW:(g + 1) * POOL_GW] = (mixed * sc_ref[:, g * POOL_GW:(g + 1) * POOL_GW]).astype(BF16)


def _attn_kernel(*refs, local, nq, ncb):
    sink_ref = refs[0]
    if local:
        q_ref, k_ref, v_ref, kc_ref, vc_ref, o_ref, pall_ref = refs[1:8]
        slot_a, slot_b = refs[8:10], refs[10:12]
    else:
        q_ref, kc_ref, vc_ref, o_ref, pall_ref = refs[1:6]
        slot_a = refs[6:8]
        k_ref = v_ref = None
    BLK = ATT_BLOCK
    KB = 2 * BLK
    nloc = 3 if local else 0
    kj = lax.broadcasted_iota(jnp.int32, (BLK, 2 * BLK), 0)
    qi = lax.broadcasted_iota(jnp.int32, (BLK, 2 * BLK), 1) % BLK
    tri_prev = jnp.where(kj >= qi, 0.0, NEG_BIG)
    tri_next = jnp.where(kj <= qi, 0.0, NEG_BIG)
    head_row = lax.broadcasted_iota(jnp.int32, (LANES, BLK), 0) < ATT_HD
    nkeys = (nloc + ncb) * KB
    orow = lax.broadcasted_iota(jnp.int32, (16, nkeys), 0)
    ocol = lax.broadcasted_iota(jnp.int32, (16, nkeys), 1)
    ones_rows = jnp.where(orow == (ocol % KB) // BLK, 1.0, 0.0).astype(BF16)

    def neighbours(nb):
        if not local:
            return []
        return [(blk, bias) for blk, bias in ((nb - 1, tri_prev), (nb, None), (nb + 1, tri_next)) if 0 <= blk < nq]

    def scores(nb, slot):
        s_ref, m_ref = slot
        near = neighbours(nb)
        for kh in range(ATT_KV_HEADS):
            qt = q_ref[0, nb, kh]
            kcat = jnp.concatenate([k_ref[kh, 0, blk * KB:(blk + 1) * KB, :] for blk, _ in near] + [kc_ref[kh, 0]],
                                   axis=0)
            sc = jnp.dot(kcat, qt, preferred_element_type=F32)
            mx = [None, None]
            for j in range(len(near) + ncb):
                for e in range(2):
                    krow = slice((2 * j + e) * BLK, (2 * j + e + 1) * BLK)
                    c = sc[krow]
                    if j < len(near) and near[j][1] is not None:
                        c = c + near[j][1]
                    s_ref[kh, krow, :] = c
                    cm = jnp.max(c.reshape(BLK // 8, 8, 2 * BLK), axis=0)
                    mx[e] = cm if mx[e] is None else jnp.maximum(mx[e], cm)
            for e in range(2):
                m_ref[kh, e] = mx[e]

    def softmax_pv(nb, slot):
        s_ref, m_ref = slot
        p_ref = pall_ref.at[nb]
        rows = slice(nb * BLK, (nb + 1) * BLK)
        near = neighbours(nb)
        nkb = len(near) + ncb
        for kh in range(ATT_KV_HEADS):
            sink_term = [[None, None], [None, None]]
            for s in range(2):
                scols = slice(s * BLK, (s + 1) * BLK)
                for e in range(2):
                    sink = sink_ref[kh * 4 + 2 * s + e] * LOG2E
                    m = jnp.maximum(jnp.max(m_ref[kh, e, :, scols], axis=0, keepdims=True), sink)
                    for j in range(nkb):
                        kr = slice((2 * j + e) * BLK, (2 * j + e + 1) * BLK)
                        p_ref[kh, kr, scols] = jnp.exp2(s_ref[kh, kr, scols] - m).astype(BF16)
                    sink_term[s][e] = jnp.exp2(sink - m)
            vcat = jnp.concatenate([v_ref[kh, 0, blk] for blk, _ in near]
                                   + [vc_ref[kh, 0, j] for j in range(ncb)], axis=1)
            ot = jnp.dot(jnp.concatenate([vcat, ones_rows[:, 0:nkb * KB]], axis=0), p_ref[kh, 0:nkb * KB, :],
                         preferred_element_type=F32)
            for s in range(2):
                scols = slice(s * BLK, (s + 1) * BLK)
                inv = [1.0 / (ot[LANES + e:LANES + e + 1, scols] + sink_term[s][e]) for e in range(2)]
                scale = jnp.where(head_row, inv[0], inv[1])
                o_ref[0, rows, (2 * kh + s) * LANES:(2 * kh + s + 1) * LANES] = \
                    (ot[0:LANES, scols] * scale).T.astype(BF16)

    if not local:
        for nblk in range(nq):
            scores(nblk, slot_a)
            softmax_pv(nblk, slot_a)
        return

    scores(0, slot_a)
    for nblk in range(nq):
        cur, nxt = (slot_a, slot_b) if nblk % 2 == 0 else (slot_b, slot_a)
        if nblk + 1 < nq:
            scores(nblk + 1, nxt)
        softmax_pv(nblk, cur)


def _attention(sink, q, k, v, kc, vc):
    b, nq = q.shape[:2]
    l = nq * ATT_BLOCK
    qw = ATT_HEADS * ATT_HD
    local = k is not None
    ncb = vc.shape[2]
    tok = pl.BlockSpec((1, l, qw), lambda i: (i, 0, 0))
    lead = lambda a: pl.BlockSpec((1,) + a.shape[1:], lambda i: (i,) + (0,) * (a.ndim - 1))
    second = lambda a: pl.BlockSpec((a.shape[0], 1) + a.shape[2:], lambda i: (0, i) + (0,) * (a.ndim - 2))
    kvs = ([k, v] if local else []) + [kc, vc]
    args = [sink, q] + kvs
    in_specs = [pl.BlockSpec(memory_space=pltpu.SMEM), lead(q)] + [second(a) for a in kvs]
    nkb = (3 if local else 0) + ncb
    tile = (ATT_KV_HEADS, nkb * 2 * ATT_BLOCK, 2 * ATT_BLOCK)
    slot = [pltpu.VMEM(tile, F32), pltpu.VMEM((ATT_KV_HEADS, 2, 8, 2 * ATT_BLOCK), F32)]
    scratch = [pltpu.VMEM((nq,) + tile, BF16)] + slot * (2 if local else 1)
    return pl.pallas_call(
        functools.partial(_attn_kernel, local=local, nq=nq, ncb=ncb),
        grid=(b,), in_specs=in_specs, out_specs=tok,
        out_shape=jax.ShapeDtypeStruct((b, l, qw), BF16),
        scratch_shapes=scratch,
        compiler_params=_cparams(1),
        name="attn_window" if local else "attn_ctx",
    )(*args)


def _merge_mlp_kernel(x_ref, yr_ref, yp_ref, ya_ref, gt_ref, mod_ref, nw_ref,
                      wr_ref, wp_ref, wa_ref, wo_ref, w1_ref, w2_ref, o_ref, *, d, hid_chunk):
    mod = mod_ref[0]
    g1, sh2, sc2, g2 = mod[:, 2 * d:3 * d], mod[:, 3 * d:4 * d], mod[:, 4 * d:5 * d], mod[:, 5 * d:6 * d]
    y = None
    for i, (y_ref, w_ref) in enumerate(((yr_ref, wr_ref), (yp_ref, wp_ref), (ya_ref, wa_ref))):
        proj = jnp.dot(y_ref[0], w_ref[...], preferred_element_type=F32)
        term = _sigmoid(gt_ref[0, :, i * d:(i + 1) * d].astype(F32)) * proj
        y = term if y is None else y + term
    mix = jnp.dot(y.astype(BF16), wo_ref[...], preferred_element_type=F32)
    x1 = x_ref[0] + g1 * mix
    ms = jnp.mean(x1 * x1, axis=-1, keepdims=True)
    h = ((x1 * lax.rsqrt(ms + EPS) * nw_ref[...]) * (1.0 + sc2) + sh2).astype(BF16)
    acc = None
    hidden = w1_ref.shape[1]
    for c in range(hidden // hid_chunk):
        a = jnp.dot(h, w1_ref[:, c * hid_chunk:(c + 1) * hid_chunk], preferred_element_type=F32)
        a = jnp.maximum(a, 0.0)
        part = jnp.dot((a * a).astype(BF16), w2_ref[c * hid_chunk:(c + 1) * hid_chunk, :], preferred_element_type=F32)
        acc = part if acc is None else acc + part
    o_ref[0] = x1 + g2 * acc


def _merge_mlp(x3, y_ret, y_pool, y_att, gates, mod3, mod_row, layer, nw, weights, *, tm):
    bx, l, d = x3.shape
    row = lambda b, i: (b, i, 0)
    bw = y_ret.shape[2]
    in_specs = [pl.BlockSpec((1, tm, d), row)] + [pl.BlockSpec((1, tm, bw), row)] * 3 + [
        pl.BlockSpec((1, tm, N_BRANCH * d), row),
        pl.BlockSpec((1, 1, mod3.shape[2]), lambda b, i: (mod_row(b), 0, 0)),
        _layer_block(nw, layer)] + [_layer_block(w, None, resident=True) for w in weights]
    return pl.pallas_call(
        functools.partial(_merge_mlp_kernel, d=d, hid_chunk=1024),
        grid=(bx, l // tm), in_specs=in_specs, out_specs=pl.BlockSpec((1, tm, d), row),
        out_shape=jax.ShapeDtypeStruct((bx, l, d), F32),
        compiler_params=_cparams(2),
        name="merge_mlp",
    )(x3, y_ret, y_pool, y_att, gates, mod3, nw, *weights)


def _rope_tables(s):
    pos = jnp.arange(s, dtype=F32)
    half = RET_DK // 2
    ang = pos[:, None] * (RET_ROPE_BASE ** (-jnp.arange(half, dtype=F32) / half))[None, :]
    cos_r = jnp.tile(jnp.concatenate([jnp.cos(ang), jnp.cos(ang)], axis=1), (1, LANES // RET_DK))
    sin_r = jnp.tile(jnp.concatenate([-jnp.sin(ang), jnp.sin(ang)], axis=1), (1, LANES // RET_DK))
    q = ATT_HD // 4
    freq = ROPE_BASE ** (-jnp.arange(q, dtype=F32) / q)
    ar = (jnp.arange(s) // GRID_W).astype(F32)[:, None] * freq[None, :]
    ac = (jnp.arange(s) % GRID_W).astype(F32)[:, None] * freq[None, :]
    cos_a = jnp.concatenate([jnp.cos(ar), jnp.cos(ar), jnp.cos(ac), jnp.cos(ac)], axis=1)
    sin_a = jnp.concatenate([-jnp.sin(ar), jnp.sin(ar), -jnp.sin(ac), jnp.sin(ac)], axis=1)
    return cos_r, sin_r, jnp.tile(cos_a, (1, LANES // ATT_HD)), jnp.tile(sin_a, (1, LANES // ATT_HD))


def kernel(x, c, ctx, c_ctx, norm1_w, norm2_w, ada_w, ada_b, w_in, ret_decay, pool_w, pool_scale, q_norm_w, k_norm_w, attn_sink, w_ret_out, w_pool_out, w_attn_out, w_out, w_mlp1, w_mlp2):
    b, s, d = x.shape
    lc = ctx.shape[1]
    depth = w_in.shape[0]
    tm = 512 if s % 512 == 0 else 256
    tmc = 512 if (b * lc) % 512 == 0 else 256
    tm_in = 2 * INPROJ_SUBTILE if s % (2 * INPROJ_SUBTILE) == 0 else tm

    rows = -(-(b + 1) // 8) * 8
    cc = jnp.zeros((rows, d), F32).at[:b].set(c).at[b].set(c_ctx)
    mod3 = _modulation(cc, ada_w, ada_b).reshape(depth * rows, 1, 6 * d)

    tables = _rope_tables(s)
    xc = ctx.reshape(1, b * lc, d)

    log_gamma = jnp.log1p(-jnp.exp2(-ret_decay.astype(F32)))
    w_in_b = w_in.astype(BF16)
    nw1, nw2 = norm1_w.reshape(depth, 1, d), norm2_w.reshape(depth, 1, d)
    qnw = jnp.tile(q_norm_w, (1, LANES // ATT_HD)).reshape(depth, 1, LANES)
    knw = jnp.tile(k_norm_w, (1, LANES // ATT_HD)).reshape(depth, 1, LANES)
    pw = pool_w.astype(BF16)
    psc = pool_scale.reshape(depth, 1, -1)
    merge_w_f32 = (w_ret_out, w_pool_out, w_attn_out, w_out, w_mlp1, w_mlp2)
    per_batch = lambda p: p.reshape(b, lc, p.shape[-1])
    ncb = lc // ATT_BLOCK

    for l in range(depth):
        last = l == depth - 1
        lat_row = lambda bi, l=l: l * rows + bi
        ctx_row = lambda bi, l=l: l * rows + b

        cparts = _inproj(xc, mod3, ctx_row, l, nw1, w_in_b, qnw, knw, None, tm=tmc, ctx_only=last)
        rk_c, rv_c = per_batch(cparts[0]), per_batch(cparts[1])
        ak_c = cparts[2].reshape(ATT_KV_HEADS, b, 2 * lc, LANES)
        av_c = cparts[3].reshape(ATT_KV_HEADS, b, ncb, LANES, 2 * ATT_BLOCK)

        rk, rv, ak, av, rq, rg, aq, pu, gt, *weights = _inproj(x, mod3, lat_row, l, nw1, w_in_b, qnw, knw, tables,
                                                               tm=tm_in, ctx_only=False, cast=merge_w_f32)
        if last:
            _, y_ret, _, y_pool = _retention_pool(log_gamma[l], (rk_c, rv_c), (rq, rk, rv, rg), (pu,),
                                                  l, pw, psc, ctx_out=False)
        else:
            rq_c, rg_c, pu_c, gt_c = [per_batch(cparts[i]) for i in (4, 5, 7, 8)]
            aq_c = cparts[6].reshape(b, ncb, ATT_KV_HEADS, LANES, 2 * ATT_BLOCK)
            y_ret_c, y_ret, y_pool_c, y_pool = _retention_pool(log_gamma[l], (rq_c, rk_c, rv_c, rg_c),
                                                               (rq, rk, rv, rg), (pu_c, pu), l, pw, psc, ctx_out=True)
        y_att = _attention(attn_sink[l], aq, ak, av, ak_c, av_c)
        x = _merge_mlp(x, y_ret, y_pool, y_att, gt, mod3, lat_row, l, nw2, weights, tm=tm)

        if not last:
            y_att_c = _attention(attn_sink[l], aq_c, None, None, ak_c, av_c)
            flat = lambda a: a.reshape(1, b * lc, a.shape[-1])
            xc = _merge_mlp(xc, flat(y_ret_c), flat(y_pool_c), flat(y_att_c), flat(gt_c), mod3, ctx_row, l, nw2,
                            weights, tm=tmc)
    return x
```
